```python
import jax, jax.numpy as jnp
from jax import lax
import numpy as np

D_MODEL = 1024
BATCH = 8
SEQ = 4096
DEPTH = 2

HEAD_DIM = 64
MIX_HEADS = D_MODEL // HEAD_DIM
A_HEADS = MIX_HEADS // 2
B_HEADS = MIX_HEADS - A_HEADS
C_HEADS = MIX_HEADS
C_KV_HEADS = max(1, C_HEADS // 8)
A_WIDTH = A_HEADS * HEAD_DIM
B_WIDTH = B_HEADS * HEAD_DIM
C_WIDTH = C_HEADS * HEAD_DIM
C_KV_WIDTH = C_KV_HEADS * HEAD_DIM
EVEN_IN = 4 * A_WIDTH + 4 * B_WIDTH
ODD_IN = 2 * C_WIDTH + 2 * C_KV_WIDTH
DILATED_PAIRS = ((128, 1), (512, 4), (2048, 16))
MOBA_BLOCK = 256
MOBA_TOPK = 3
MOBA_QCHUNK = 32
C_WINDOW = 128
BAND_BLOCK = 128
ROPE_THETA = 10000.0
NORM_EPS = 1e-6
NEG_INF = -1e30
N_EVEN = (DEPTH + 1) // 2
N_ODD = DEPTH // 2

kernel_name = "hybrid_dilated_moba_swa_sink_trunk"


def rms_norm(x, g):
    xf = x.astype(jnp.float32)
    xf = xf * lax.rsqrt(jnp.mean(xf * xf, axis=-1, keepdims=True) + NORM_EPS)
    return xf.astype(x.dtype) * g


def rope_tables(seq):
    pos = jnp.arange(seq, dtype=jnp.float32)
    inv_freq = ROPE_THETA ** (-jnp.arange(0, HEAD_DIM, 2, dtype=jnp.float32) / HEAD_DIM)
    ang = pos[:, None] * inv_freq[None, :]
    ang = jnp.concatenate([ang, ang], axis=-1)
    return jnp.cos(ang), jnp.sin(ang)


def apply_rope(t, cos, sin):
    tf = t.astype(jnp.float32)
    t1, t2 = jnp.split(tf, 2, axis=-1)
    rot = jnp.concatenate([-t2, t1], axis=-1)
    return (tf * cos + rot * sin).astype(t.dtype)


def qk_prep(t, g, cos, sin):
    return apply_rope(rms_norm(t, g), cos, sin)


def to_heads(t, n_heads):
    b, s, _ = t.shape
    return t.reshape(b, s, n_heads, HEAD_DIM).transpose(0, 2, 1, 3)


def from_heads(t):
    b, h, s, d = t.shape
    return t.transpose(0, 2, 1, 3).reshape(b, s, h * d)


def split_cols(t, widths):
    parts, start = [], 0
    for w in widths:
        parts.append(t[..., start:start + w])
        start += w
    return parts


def banded_attention(q, k, v, max_dist, sinks=None):
    n, hk, g, seq, d = q.shape
    nq = -(-seq // BAND_BLOCK)
    lp = nq * BAND_BLOCK
    qb = jnp.pad(q, ((0, 0), (0, 0), (0, 0), (0, lp - seq), (0, 0))).reshape(n, hk, g, nq, BAND_BLOCK, d)
    kv_pad = ((0, 0), (0, 0), (BAND_BLOCK, lp - seq), (0, 0))

    def windows(t):
        tb = jnp.pad(t, kv_pad).reshape(n, hk, nq + 1, BAND_BLOCK, d)
        return jnp.concatenate([tb[:, :, :-1], tb[:, :, 1:]], axis=3)

    kw, vw = windows(k), windows(v)
    s = jnp.einsum('nhgcqd,nhckd->nhgcqk', qb, kw, preferred_element_type=jnp.float32) * (d ** -0.5)
    blk = jnp.arange(nq)[:, None, None]
    qi = jnp.arange(BAND_BLOCK)[None, :, None]
    kj = jnp.arange(2 * BAND_BLOCK)[None, None, :]
    dist = BAND_BLOCK + qi - kj
    kpos = (blk - 1) * BAND_BLOCK + kj
    mask = (dist >= 0) & (dist <= max_dist) & (kpos >= 0)
    s = jnp.where(mask, s, NEG_INF)
    m = jnp.max(s, axis=-1)
    if sinks is not None:
        sk = sinks.astype(jnp.float32)[None, :, :, None, None]
        m = jnp.maximum(m, sk)
    p = jnp.exp(s - m[..., None])
    l = jnp.sum(p, axis=-1)
    if sinks is not None:
        l = l + jnp.exp(sk - m)
    o = jnp.einsum('nhgcqk,nhckd->nhgcqd', p, vw.astype(jnp.float32)) / l[..., None]
    lse = m + jnp.log(l)
    o = o.reshape(n, hk, g, lp, d)[:, :, :, :seq]
    lse = lse.reshape(n, hk, g, lp)[:, :, :, :seq]
    return o, lse


def dilated_mixture_attention(q, k, v):
    b, h, seq, d = q.shape
    outs, lses = [], []
    for window, dil in DILATED_PAIRS:
        sub = seq // dil

        def to_sub(t):
            return t.reshape(b, h, sub, dil, d).transpose(0, 3, 1, 2, 4).reshape(b * dil, h, sub, d)

        o, lse = banded_attention(to_sub(q)[:, :, None], to_sub(k), to_sub(v), window // dil)
        outs.append(o[:, :, 0].reshape(b, dil, h, sub, d).transpose(0, 2, 3, 1, 4).reshape(b, h, seq, d))
        lses.append(lse[:, :, 0].reshape(b, dil, h, sub).transpose(0, 2, 3, 1).reshape(b, h, seq))
    w = jax.nn.softmax(jnp.stack(lses, axis=0), axis=0)
    return jnp.sum(w[..., None] * jnp.stack(outs, axis=0), axis=0)


def moba_attention(q, k, v):
    b, h, seq, d = q.shape
    nb = -(-seq // MOBA_BLOCK)
    sp = nb * MOBA_BLOCK
    pad = ((0, 0), (0, 0), (0, sp - seq), (0, 0))
    qp, kp, vp = jnp.pad(q, pad), jnp.pad(k, pad), jnp.pad(v, pad)
    kb = kp.reshape(b, h, nb, MOBA_BLOCK, d)
    vb = vp.reshape(b, h, nb, MOBA_BLOCK, d)
    scale = d ** -0.5
    own_blk = jnp.arange(sp) // MOBA_BLOCK
    n_sel = min(MOBA_TOPK, nb - 1)
    nc = sp // MOBA_QCHUNK

    def to_chunks(t):
        return jnp.moveaxis(t.reshape(b, h, nc, MOBA_QCHUNK, t.shape[-1]), 2, 0)

    xs = [jnp.arange(nc), to_chunks(qp)]
    if n_sel > 0:
        kmean = jnp.mean(kb.astype(jnp.float32), axis=3)
        gate = jnp.einsum('bhsd,bhnd->bhsn', qp.astype(jnp.float32), kmean)
        past = jnp.arange(nb)[None, :] < own_blk[:, None]
        gate = jnp.where(past, gate, NEG_INF)
        _, sel = lax.top_k(gate, n_sel)
        sel_valid = sel < own_blk[:, None]
        xs += [to_chunks(sel), to_chunks(sel_valid)]
    bi = jnp.arange(b)[:, None, None]
    hi = jnp.arange(h)[None, :, None]
    offs = jnp.arange(MOBA_BLOCK)

    def one_chunk(args):
        c, qc = args[0], args[1]
        qpos = c * MOBA_QCHUNK + jnp.arange(MOBA_QCHUNK)
        blk = (c * MOBA_QCHUNK) // MOBA_BLOCK
        k_own = lax.dynamic_slice_in_dim(kp, blk * MOBA_BLOCK, MOBA_BLOCK, axis=2)
        v_own = lax.dynamic_slice_in_dim(vp, blk * MOBA_BLOCK, MOBA_BLOCK, axis=2)
        s_own = jnp.einsum('bhqd,bhkd->bhqk', qc, k_own, preferred_element_type=jnp.float32) * scale
        causal = (blk * MOBA_BLOCK + offs)[None, :] <= qpos[:, None]
        s_own = jnp.where(causal, s_own, NEG_INF)
        if n_sel == 0:
            p = jax.nn.softmax(s_own, axis=-1)
            return jnp.einsum('bhqk,bhkd->bhqd', p, v_own.astype(jnp.float32))
        selc, validc = args[2], args[3]
        flat = selc.reshape(b, h, MOBA_QCHUNK * n_sel)
        k_sel = kb[bi, hi, flat].reshape(b, h, MOBA_QCHUNK, n_sel * MOBA_BLOCK, d)
        v_sel = vb[bi, hi, flat].reshape(b, h, MOBA_QCHUNK, n_sel * MOBA_BLOCK, d)
        s_sel = jnp.einsum('bhqd,bhqkd->bhqk', qc, k_sel, preferred_element_type=jnp.float32) * scale
        s_sel = jnp.where(jnp.repeat(validc, MOBA_BLOCK, axis=-1), s_sel, NEG_INF)
        p = jax.nn.softmax(jnp.concatenate([s_sel, s_own], axis=-1), axis=-1)
        nsk = n_sel * MOBA_BLOCK
        return (jnp.einsum('bhqk,bhqkd->bhqd', p[..., :nsk], v_sel.astype(jnp.float32))
                + jnp.einsum('bhqk,bhkd->bhqd', p[..., nsk:], v_own.astype(jnp.float32)))

    out = lax.map(one_chunk, tuple(xs))
    return jnp.moveaxis(out, 0, 2).reshape(b, h, sp, d)[:, :, :seq]


def sliding_window_sink_attention(q, k, v, sinks):
    b, hq, seq, d = q.shape
    g = hq // C_KV_HEADS
    o, _ = banded_attention(q.reshape(b, C_KV_HEADS, g, seq, d), k, v, C_WINDOW - 1,
                            sinks.reshape(C_KV_HEADS, g))
    return o.reshape(b, hq, seq, d)


def even_layer(x, g_norm, w_in, w_out, qn_a, kn_a, qn_b, kn_b, cos, sin):
    h = rms_norm(x, g_norm)
    proj = h @ w_in
    qa, ka, va, za, qb, kb, vb, zb = split_cols(proj, [A_WIDTH] * 4 + [B_WIDTH] * 4)
    qa = qk_prep(to_heads(qa, A_HEADS), qn_a, cos, sin)
    ka = qk_prep(to_heads(ka, A_HEADS), kn_a, cos, sin)
    qb = qk_prep(to_heads(qb, B_HEADS), qn_b, cos, sin)
    kb = qk_prep(to_heads(kb, B_HEADS), kn_b, cos, sin)
    oa = dilated_mixture_attention(qa, ka, to_heads(va, A_HEADS))
    ob = moba_attention(qb, kb, to_heads(vb, B_HEADS))
    y = from_heads(jnp.concatenate([oa, ob], axis=1)).astype(x.dtype)
    z = jnp.concatenate([za, zb], axis=-1)
    return x + (y * jax.nn.silu(z)) @ w_out


def odd_layer(x, g_norm, w_in, w_out, qn_c, kn_c, sinks, cos, sin):
    h = rms_norm(x, g_norm)
    proj = h @ w_in
    qc, kc, vc, zc = split_cols(proj, [C_WIDTH, C_KV_WIDTH, C_KV_WIDTH, C_WIDTH])
    qc = qk_prep(to_heads(qc, C_HEADS), qn_c, cos, sin)
    kc = qk_prep(to_heads(kc, C_KV_HEADS), kn_c, cos, sin)
    oc = sliding_window_sink_attention(qc, kc, to_heads(vc, C_KV_HEADS), sinks)
    y = from_heads(oc).astype(x.dtype)
    return x + (y * jax.nn.silu(zc)) @ w_out


def setup_inputs(seed: int = 0) -> dict:
    key = jax.random.key(seed)
    ks = jax.random.split(key, 16)

    def normal(k, shape, scale):
        return jax.random.normal(k, shape, jnp.float32) * scale

    def gain(k, shape):
        return 1.0 + 0.05 * jax.random.normal(k, shape, jnp.float32)

    return {
        'x': normal(ks[0], (BATCH, SEQ, D_MODEL), 1.0),
        'norm_even': gain(ks[1], (N_EVEN, D_MODEL)),
        'w_in_even': normal(ks[2], (N_EVEN, D_MODEL, EVEN_IN), D_MODEL ** -0.5),
        'w_out_even': normal(ks[3], (N_EVEN, A_WIDTH + B_WIDTH, D_MODEL), (A_WIDTH + B_WIDTH) ** -0.5),
        'qnorm_a': gain(ks[4], (N_EVEN, HEAD_DIM)),
        'knorm_a': gain(ks[5], (N_EVEN, HEAD_DIM)),
        'qnorm_b': gain(ks[6], (N_EVEN, HEAD_DIM)),
        'knorm_b': gain(ks[7], (N_EVEN, HEAD_DIM)),
        'norm_odd': gain(ks[8], (N_ODD, D_MODEL)),
        'w_in_odd': normal(ks[9], (N_ODD, D_MODEL, ODD_IN), D_MODEL ** -0.5),
        'w_out_odd': normal(ks[10], (N_ODD, C_WIDTH, D_MODEL), C_WIDTH ** -0.5),
        'qnorm_c': gain(ks[11], (N_ODD, HEAD_DIM)),
        'knorm_c': gain(ks[12], (N_ODD, HEAD_DIM)),
        'sinks_c': normal(ks[13], (N_ODD, C_HEADS), 1.0),
    }


def reference(x, norm_even, w_in_even, w_out_even, qnorm_a, knorm_a, qnorm_b, knorm_b,
              norm_odd, w_in_odd, w_out_odd, qnorm_c, knorm_c, sinks_c):
    cos, sin = rope_tables(x.shape[1])
    for layer in range(DEPTH):
        i = layer // 2
        if layer % 2 == 0:
            x = even_layer(x, norm_even[i], w_in_even[i], w_out_even[i],
                           qnorm_a[i], knorm_a[i], qnorm_b[i], knorm_b[i], cos, sin)
        else:
            x = odd_layer(x, norm_odd[i], w_in_odd[i], w_out_odd[i],
                          qnorm_c[i], knorm_c[i], sinks_c[i], cos, sin)
    return x
```

```python
import functools

import jax
import jax.numpy as jnp
from jax import lax
from jax.experimental import pallas as pl
from jax.experimental.pallas import tpu as pltpu

D_MODEL = 1024
HEAD_DIM = 64
LANES = 128
PAIR = LANES // HEAD_DIM
A_WIDTH = 512
B_WIDTH = 512
C_WIDTH = 1024
C_KV_HEADS = 2
C_GROUP = 8
DILATIONS = (1, 4, 16)
BAND = 128
MOBA_BLOCK = 256
MOBA_TOPK = 3
ROPE_THETA = 10000.0
NORM_EPS = 1e-6
NEG = -1e30
VMEM_LIMIT = 56 * 1024 * 1024
PROJ_ROWS = 512
PROJ_COLS = 512

_NT = (((1,), (1,)), ((), ()))


def _lane_ids(shape):
    return lax.broadcasted_iota(jnp.int32, shape, len(shape) - 1)


def _split_heads_rows(t):
    lane = _lane_ids(t.shape)
    zero = jnp.zeros_like(t)
    return jnp.concatenate([jnp.where(lane < HEAD_DIM, t, zero),
                            jnp.where(lane >= HEAD_DIM, t, zero)], axis=0)


def _merge_heads_rows(t2):
    n = t2.shape[0] // 2
    lane = _lane_ids((n, LANES))
    return jnp.where(lane < HEAD_DIM, t2[:n], t2[n:])


def _inproj_kernel(x_ref, g_ref, w_ref, gain_ref, cos_ref, sin_ref, o_ref, *, rope_tiles):
    x = x_ref[...]
    ms = jnp.mean(x * x, axis=-1, keepdims=True)
    h = (x * lax.rsqrt(ms + NORM_EPS) * g_ref[...]).astype(jnp.bfloat16)
    n_cols = o_ref.shape[1]
    lane = _lane_ids((x.shape[0], LANES))
    head0 = lane < HEAD_DIM
    first_half = (lane % HEAD_DIM) < (HEAD_DIM // 2)
    for c0 in range(0, n_cols, PROJ_COLS):
        r = jnp.dot(h, w_ref[:, c0:c0 + PROJ_COLS], preferred_element_type=jnp.float32)
        for t0 in range(0, PROJ_COLS, LANES):
            col = c0 + t0
            t = r[:, t0:t0 + LANES]
            if col // LANES in rope_tiles:
                sq = t * t
                ss0 = jnp.sum(jnp.where(head0, sq, 0.0), axis=-1, keepdims=True)
                ss1 = jnp.sum(jnp.where(head0, 0.0, sq), axis=-1, keepdims=True)
                inv = jnp.where(head0, lax.rsqrt(ss0 / HEAD_DIM + NORM_EPS),
                                lax.rsqrt(ss1 / HEAD_DIM + NORM_EPS))
                t = t * inv * gain_ref[:, col:col + LANES]
                rot = jnp.where(first_half, pltpu.roll(t, LANES - HEAD_DIM // 2, 1),
                                pltpu.roll(t, HEAD_DIM // 2, 1))
                t = t * cos_ref[...] + rot * sin_ref[...]
            o_ref[:, col:col + LANES] = t.astype(o_ref.dtype)


def _inproj(x2, g, w, gain, cos, sin, rope_tiles, seq):
    rows, d = x2.shape
    n_cols = w.shape[1]
    seq_tiles = seq // PROJ_ROWS
    return pl.pallas_call(
        functools.partial(_inproj_kernel, rope_tiles=rope_tiles),
        grid=(rows // PROJ_ROWS,),
        in_specs=[
            pl.BlockSpec((PROJ_ROWS, d), lambda i: (i, 0)),
            pl.BlockSpec((1, d), lambda i: (0, 0)),
            pl.BlockSpec((d, n_cols), lambda i: (0, 0)),
            pl.BlockSpec((1, n_cols), lambda i: (0, 0)),
            pl.BlockSpec((PROJ_ROWS, LANES), lambda i: (i % seq_tiles, 0)),
            pl.BlockSpec((PROJ_ROWS, LANES), lambda i: (i % seq_tiles, 0)),
        ],
        out_specs=pl.BlockSpec((PROJ_ROWS, n_cols), lambda i: (i, 0)),
        out_shape=jax.ShapeDtypeStruct((rows, n_cols), jnp.bfloat16),
        compiler_params=pltpu.CompilerParams(
            dimension_semantics=("parallel",), vmem_limit_bytes=VMEM_LIMIT),
        name="inproj",
    )(x2, g, w, gain, cos, sin)


def _outproj_kernel(*refs, n_parts):
    x_ref = refs[0]
    y_refs = refs[1:1 + n_parts]
    z_refs = refs[1 + n_parts:1 + 2 * n_parts]
    w_ref = refs[1 + 2 * n_parts]
    o_ref = refs[2 + 2 * n_parts]
    acc = x_ref[...]
    k0 = 0
    for y_ref, z_ref in zip(y_refs, z_refs):
        z = z_ref[...].astype(jnp.float32)
        gated = y_ref[...].astype(jnp.float32) * (z / (1.0 + jnp.exp(-z)))
        kw = y_ref.shape[1]
        acc = acc + jnp.dot(gated.astype(jnp.bfloat16), w_ref[k0:k0 + kw, :],
                            preferred_element_type=jnp.float32)
        k0 += kw
    o_ref[...] = acc


def _outproj(x2, ys, zs, w, part_width):
    rows, d = x2.shape
    n_parts = len(ys)
    in_specs = [pl.BlockSpec((PROJ_ROWS, d), lambda i: (i, 0))]
    for _, blk in list(ys) + list(zs):
        in_specs.append(pl.BlockSpec((PROJ_ROWS, part_width), lambda i, blk=blk: (i, blk)))
    in_specs.append(pl.BlockSpec(w.shape, lambda i: (0, 0)))
    return pl.pallas_call(
        functools.partial(_outproj_kernel, n_parts=n_parts),
        grid=(rows // PROJ_ROWS,),
        in_specs=in_specs,
        out_specs=pl.BlockSpec((PROJ_ROWS, d), lambda i: (i, 0)),
        out_shape=jax.ShapeDtypeStruct((rows, d), jnp.float32),
        compiler_params=pltpu.CompilerParams(
            dimension_semantics=("parallel",), vmem_limit_bytes=VMEM_LIMIT),
        name="outproj",
    )(x2, *[y for y, _ in ys], *[z for z, _ in zs], w)


def _band_block(q, kw, vw, first_key, first_block, sink_col):
    q2 = _split_heads_rows(q)
    s = lax.dot_general(q2, kw, _NT, preferred_element_type=jnp.float32)
    qi = lax.broadcasted_iota(jnp.int32, s.shape, 0) % BAND
    kj = lax.broadcasted_iota(jnp.int32, s.shape, 1)
    k_lo = jnp.where(first_block, BAND, 0)
    mask = (kj >= qi + first_key) & (kj <= qi + BAND) & (kj >= k_lo)
    s = jnp.where(mask, s, NEG)
    m = jnp.max(s, axis=-1, keepdims=True)
    if sink_col is not None:
        m = jnp.maximum(m, sink_col)
    p = jnp.exp(s - m)
    l = jnp.sum(p, axis=-1, keepdims=True)
    if sink_col is not None:
        l = l + jnp.exp(sink_col - m)
    pv = jnp.dot(p.astype(jnp.bfloat16), vw, preferred_element_type=jnp.float32)
    acc = _merge_heads_rows(pv)
    m_b = _merge_heads_rows(jnp.broadcast_to(m, pv.shape))
    l_b = _merge_heads_rows(jnp.broadcast_to(l, pv.shape))
    return acc, m_b, l_b


def _attn_a_kernel(q_ref, k_ref, v_ref, o_ref, qf, kf, vf, qd, kd, vd, m_s, l_s, acc_s):
    seq = q_ref.shape[0]
    n_blocks = seq // BAND
    qf[...] = q_ref[...].astype(jnp.float32)
    kf[...] = k_ref[...].astype(jnp.float32)
    vf[...] = v_ref[...].astype(jnp.float32)
    zeros = jnp.zeros((BAND, LANES), jnp.bfloat16)
    kd[0:BAND, :] = zeros
    vd[0:BAND, :] = zeros

    for dil in DILATIONS:
        sub = seq // dil
        blocks_per_sub = sub // BAND
        if dil == 1:
            qd[...] = q_ref[...]
            kd[BAND:, :] = k_ref[...]
            vd[BAND:, :] = v_ref[...]
        else:
            for r in range(dil):
                rows = pl.ds(r, sub, stride=dil)
                qd[r * sub:(r + 1) * sub, :] = qf[rows, :].astype(jnp.bfloat16)
                kd[BAND + r * sub:BAND + (r + 1) * sub, :] = kf[rows, :].astype(jnp.bfloat16)
                vd[BAND + r * sub:BAND + (r + 1) * sub, :] = vf[rows, :].astype(jnp.bfloat16)

        def body(n, carry, dil=dil, blocks_per_sub=blocks_per_sub):
            r = n // blocks_per_sub
            c = n % blocks_per_sub
            row0 = pl.multiple_of(n * BAND, BAND)
            q = qd[pl.ds(row0, BAND), :]
            kw = kd[pl.ds(row0, 2 * BAND), :]
            vw = vd[pl.ds(row0, 2 * BAND), :]
            acc, m_b, l_b = _band_block(q, kw, vw, 0, c == 0, None)
            if dil == 1:
                rows = pl.ds(row0, BAND)
                m_s[rows, :] = m_b
                l_s[rows, :] = l_b
                acc_s[rows, :] = acc
            else:
                rows = pl.ds(c * (BAND * dil) + r, BAND, stride=dil)
                m_old = m_s[rows, :]
                m_new = jnp.maximum(m_old, m_b)
                a_old = jnp.exp(m_old - m_new)
                a_new = jnp.exp(m_b - m_new)
                m_s[rows, :] = m_new
                l_s[rows, :] = a_old * l_s[rows, :] + a_new * l_b
                acc_s[rows, :] = a_old * acc_s[rows, :] + a_new * acc
            return carry

        lax.fori_loop(0, n_blocks, body, 0)

    o_ref[...] = (acc_s[...] / l_s[...]).astype(o_ref.dtype)


def _attn_a(proj, batch, seq):
    n_pairs = A_WIDTH // LANES
    blocks_per_batch = 1

    def col_spec(group):
        return pl.BlockSpec((seq, LANES), lambda b, p, group=group: (b, group * n_pairs + p))

    f32 = jnp.float32
    bf16 = jnp.bfloat16
    del blocks_per_batch
    return pl.pallas_call(
        _attn_a_kernel,
        grid=(batch, n_pairs),
        in_specs=[col_spec(0), col_spec(1), col_spec(2)],
        out_specs=pl.BlockSpec((seq, LANES), lambda b, p: (b, p)),
        out_shape=jax.ShapeDtypeStruct((batch * seq, A_WIDTH), bf16),
        scratch_shapes=[
            pltpu.VMEM((seq, LANES), f32), pltpu.VMEM((seq, LANES), f32), pltpu.VMEM((seq, LANES), f32),
            pltpu.VMEM((seq, LANES), bf16),
            pltpu.VMEM((seq + BAND, LANES), bf16), pltpu.VMEM((seq + BAND, LANES), bf16),
            pltpu.VMEM((seq, LANES), f32), pltpu.VMEM((seq, LANES), f32), pltpu.VMEM((seq, LANES), f32),
        ],
        compiler_params=pltpu.CompilerParams(
            dimension_semantics=("parallel", "parallel"), vmem_limit_bytes=VMEM_LIMIT),
        name="attn_dilated",
    )(proj, proj, proj)


def _attn_c_kernel(sinks_ref, q_ref, k_ref, v_ref, o_ref, kd, vd):
    seq = q_ref.shape[0]
    pair = pl.program_id(1)
    zeros = jnp.zeros((BAND, LANES), jnp.bfloat16)
    kd[0:BAND, :] = zeros
    vd[0:BAND, :] = zeros
    kd[BAND:, :] = k_ref[...]
    vd[BAND:, :] = v_ref[...]
    row = lax.broadcasted_iota(jnp.int32, (2 * BAND, 1), 0)
    sink_col = jnp.where(row < BAND, sinks_ref[PAIR * pair], sinks_ref[PAIR * pair + 1])

    def body(n, carry):
        row0 = pl.multiple_of(n * BAND, BAND)
        q = q_ref[pl.ds(row0, BAND), :]
        kw = kd[pl.ds(row0, 2 * BAND), :]
        vw = vd[pl.ds(row0, 2 * BAND), :]
        acc, _, l_b = _band_block(q, kw, vw, 1, n == 0, sink_col)
        o_ref[pl.ds(row0, BAND), :] = (acc / l_b).astype(o_ref.dtype)
        return carry

    lax.fori_loop(0, seq // BAND, body, 0)


def _attn_c(proj, sinks, batch, seq):
    n_pairs = C_WIDTH // LANES
    pairs_per_kv = C_GROUP // PAIR
    k_tile0 = C_WIDTH // LANES
    v_tile0 = k_tile0 + C_KV_HEADS
    bf16 = jnp.bfloat16
    return pl.pallas_call(
        _attn_c_kernel,
        grid=(batch, n_pairs),
        in_specs=[
            pl.BlockSpec(memory_space=pltpu.SMEM),
            pl.BlockSpec((seq, LANES), lambda b, p: (b, p)),
            pl.BlockSpec((seq, LANES), lambda b, p: (b, k_tile0 + p // pairs_per_kv)),
            pl.BlockSpec((seq, LANES), lambda b, p: (b, v_tile0 + p // pairs_per_kv)),
        ],
        out_specs=pl.BlockSpec((seq, LANES), lambda b, p: (b, p)),
        out_shape=jax.ShapeDtypeStruct((batch * seq, C_WIDTH), bf16),
        scratch_shapes=[pltpu.VMEM((seq + BAND, LANES), bf16), pltpu.VMEM((seq + BAND, LANES), bf16)],
        compiler_params=pltpu.CompilerParams(
            dimension_semantics=("parallel", "parallel"), vmem_limit_bytes=VMEM_LIMIT),
        name="attn_swa",
    )(sinks, proj, proj, proj)


def _moba_kernel(q_ref, k_ref, v_ref, o_ref, vt, kmean, bias_s, m_s, l_s, acc_s):
    seq = q_ref.shape[0]
    nb = seq // MOBA_BLOCK
    nq2 = PAIR * MOBA_BLOCK

    def prep(j, carry):
        rows = pl.ds(pl.multiple_of(j * MOBA_BLOCK, MOBA_BLOCK), MOBA_BLOCK)
        vt[:, rows] = v_ref[rows, :].astype(jnp.float32).T.astype(jnp.bfloat16)
        kmean[pl.ds(j, 1), :] = jnp.mean(k_ref[rows, :].astype(jnp.float32), axis=0, keepdims=True)
        return carry

    lax.fori_loop(0, nb, prep, 0)
    km = kmean[...]
    km_hi = km.astype(jnp.bfloat16)
    km_lo = (km - km_hi.astype(jnp.float32)).astype(jnp.bfloat16)
    blk_id = lax.broadcasted_iota(jnp.int32, (nb, nq2), 0)
    key_id = lax.broadcasted_iota(jnp.int32, (MOBA_BLOCK, nq2), 0)
    qry_id = lax.broadcasted_iota(jnp.int32, (MOBA_BLOCK, nq2), 1) % MOBA_BLOCK

    def scores_t(kb, q2):
        rows = pl.ds(pl.multiple_of(kb * MOBA_BLOCK, MOBA_BLOCK), MOBA_BLOCK)
        return lax.dot_general(k_ref[rows, :], q2, _NT, preferred_element_type=jnp.float32)

    def pv_t(kb, p):
        rows = pl.ds(pl.multiple_of(kb * MOBA_BLOCK, MOBA_BLOCK), MOBA_BLOCK)
        return jnp.dot(vt[:, rows], p.astype(jnp.bfloat16), preferred_element_type=jnp.float32)

    def per_query_block(qb, carry):
        qrows = pl.ds(pl.multiple_of(qb * MOBA_BLOCK, MOBA_BLOCK), MOBA_BLOCK)
        q2 = _split_heads_rows(q_ref[qrows, :])
        gate = (lax.dot_general(km_hi, q2, _NT, preferred_element_type=jnp.float32)
                + lax.dot_general(km_lo, q2, _NT, preferred_element_type=jnp.float32))
        past = blk_id < qb
        gate = jnp.where(past, gate, NEG)
        rank = jnp.zeros((nb, nq2), jnp.int32)
        for j in range(nb - 1):
            gj = gate[j:j + 1, :]
            beats = (gj > gate) | ((gj == gate) & (j < blk_id))
            rank = rank + jnp.where(beats & (j < qb), 1, 0)
        bias_s[...] = jnp.where(past & (rank < MOBA_TOPK), 0.0, NEG)

        s = scores_t(qb, q2)
        s = jnp.where(key_id <= qry_id, s, NEG)
        m0 = jnp.max(s, axis=0, keepdims=True)
        p = jnp.exp(s - m0)
        m_s[...] = m0
        l_s[...] = jnp.sum(p, axis=0, keepdims=True)
        acc_s[...] = pv_t(qb, p)

        def past_block(kb, c):
            s = scores_t(kb, q2)
            b = bias_s[pl.ds(kb, 1), :]
            m_old = m_s[...]
            m_new = jnp.maximum(m_old, jnp.max(s, axis=0, keepdims=True) + b)
            alpha = jnp.exp(m_old - m_new)
            p = jnp.exp(s - (m_new - b))
            m_s[...] = m_new
            l_s[...] = alpha * l_s[...] + jnp.sum(p, axis=0, keepdims=True)
            acc_s[...] = alpha * acc_s[...] + pv_t(kb, p)
            return c

        lax.fori_loop(0, qb, past_block, 0)

        inv_l = 1.0 / l_s[...]
        o_t = jnp.concatenate(
            [acc_s[0:HEAD_DIM, 0:MOBA_BLOCK] * inv_l[:, 0:MOBA_BLOCK],
             acc_s[HEAD_DIM:LANES, MOBA_BLOCK:nq2] * inv_l[:, MOBA_BLOCK:nq2]], axis=0)
        o_ref[qrows, :] = o_t.T.astype(o_ref.dtype)
        return carry

    lax.fori_loop(0, nb, per_query_block, 0)


def _attn_b(proj, batch, seq):
    n_pairs = B_WIDTH // LANES
    tile0 = 4 * (A_WIDTH // LANES)

    def col_spec(group):
        return pl.BlockSpec((seq, LANES), lambda b, p, group=group: (b, tile0 + group * n_pairs + p))

    nb = seq // MOBA_BLOCK
    nq2 = PAIR * MOBA_BLOCK
    f32 = jnp.float32
    return pl.pallas_call(
        _moba_kernel,
        grid=(batch, n_pairs),
        in_specs=[col_spec(0), col_spec(1), col_spec(2)],
        out_specs=pl.BlockSpec((seq, LANES), lambda b, p: (b, p)),
        out_shape=jax.ShapeDtypeStruct((batch * seq, B_WIDTH), jnp.bfloat16),
        scratch_shapes=[
            pltpu.VMEM((LANES, seq), jnp.bfloat16),
            pltpu.VMEM((nb, LANES), f32),
            pltpu.VMEM((nb, nq2), f32),
            pltpu.VMEM((1, nq2), f32), pltpu.VMEM((1, nq2), f32),
            pltpu.VMEM((LANES, nq2), f32),
        ],
        compiler_params=pltpu.CompilerParams(
            dimension_semantics=("parallel", "parallel"), vmem_limit_bytes=VMEM_LIMIT),
        name="attn_moba",
    )(proj, proj, proj)


def _rope_tables(seq):
    pos = jnp.arange(seq, dtype=jnp.float32)
    inv_freq = ROPE_THETA ** (-jnp.arange(0, HEAD_DIM, 2, dtype=jnp.float32) / HEAD_DIM)
    ang = pos[:, None] * inv_freq[None, :]
    cos = jnp.concatenate([jnp.cos(ang)] * (2 * PAIR), axis=-1)
    sin = jnp.concatenate([-jnp.sin(ang), jnp.sin(ang)] * PAIR, axis=-1)
    return cos, sin


def _tile_gain(g, width):
    return jnp.tile(g, width // HEAD_DIM)


def kernel(x, norm_even, w_in_even, w_out_even, qnorm_a, knorm_a, qnorm_b, knorm_b,
           norm_odd, w_in_odd, w_out_odd, qnorm_c, knorm_c, sinks_c):
    batch, seq, d = x.shape
    bf16 = jnp.bfloat16
    scale = HEAD_DIM ** -0.5
    cos, sin = _rope_tables(seq)
    x2 = x.reshape(batch * seq, d)

    ones = jnp.ones((A_WIDTH,), jnp.float32)
    gain0 = jnp.concatenate([
        _tile_gain(qnorm_a[0], A_WIDTH) * scale, _tile_gain(knorm_a[0], A_WIDTH), ones, ones,
        _tile_gain(qnorm_b[0], B_WIDTH) * scale, _tile_gain(knorm_b[0], B_WIDTH), ones, ones])[None, :]
    tiles_per_group = A_WIDTH // LANES
    rope0 = frozenset(t for grp in (0, 1, 4, 5) for t in range(grp * tiles_per_group, (grp + 1) * tiles_per_group))
    proj0 = _inproj(x2, norm_even[0][None, :], w_in_even[0].astype(bf16), gain0, cos, sin, rope0, seq)
    oa = _attn_a(proj0, batch, seq)
    ob = _attn_b(proj0, batch, seq)
    x1 = _outproj(x2, [(oa, 0), (ob, 0)], [(proj0, 3), (proj0, 7)], w_out_even[0].astype(bf16), A_WIDTH)

    w1 = w_in_odd[0]
    kv_w = C_KV_HEADS * HEAD_DIM
    wq, wk, wv, wz = (w1[:, :C_WIDTH], w1[:, C_WIDTH:C_WIDTH + kv_w],
                      w1[:, C_WIDTH + kv_w:C_WIDTH + 2 * kv_w], w1[:, C_WIDTH + 2 * kv_w:])

    def dup_heads(w):
        return jnp.concatenate([w[:, h * HEAD_DIM:(h + 1) * HEAD_DIM]
                                for h in range(C_KV_HEADS) for _ in range(PAIR)], axis=1)

    w1r = jnp.concatenate([wq, dup_heads(wk), dup_heads(wv), wz], axis=1).astype(bf16)
    dup_w = PAIR * kv_w
    gain1 = jnp.concatenate([
        _tile_gain(qnorm_c[0], C_WIDTH) * scale, _tile_gain(knorm_c[0], dup_w),
        jnp.ones((dup_w + C_WIDTH,), jnp.float32)])[None, :]
    rope1 = frozenset(range((C_WIDTH + dup_w) // LANES))
    proj1 = _inproj(x1, norm_odd[0][None, :], w1r, gain1, cos, sin, rope1, seq)
    oc = _attn_c(proj1, sinks_c[0], batch, seq)
    half = C_WIDTH // 2
    z_blk = (C_WIDTH + 2 * dup_w) // half
    out = _outproj(x1, [(oc, 0), (oc, 1)], [(proj1, z_blk), (proj1, z_blk + 1)],
                   w_out_odd[0].astype(bf16), half)
    return out.reshape(batch, seq, d)
```

```python
import functools

import jax
import jax.numpy as jnp
from jax import lax
from jax.experimental import pallas as pl
from jax.experimental.pallas import tpu as pltpu

D_MODEL = 1024
HEAD_DIM = 64
LANES = 128
PAIR = LANES // HEAD_DIM
A_WIDTH = 512
B_WIDTH = 512
C_WIDTH = 1024
C_KV_HEADS = 2
C_GROUP = 8
DILATIONS = (1, 4, 16)
BAND = 128
MOBA_BLOCK = 256
MOBA_TOPK = 3
MOBA_CHUNK_BLOCKS = 2
BAND_UNROLL = 4
ROPE_THETA = 10000.0
NORM_EPS = 1e-6
NEG = -1e30
VMEM_LIMIT = 56 * 1024 * 1024
PROJ_ROWS = 512
PROJ_COLS = 512

_NT = (((1,), (1,)), ((), ()))


def _lane_ids(shape):
    return lax.broadcasted_iota(jnp.int32, shape, len(shape) - 1)


def _split_heads_rows(t):
    lane = _lane_ids(t.shape)
    zero = jnp.zeros_like(t)
    return jnp.concatenate([jnp.where(lane < HEAD_DIM, t, zero),
                            jnp.where(lane >= HEAD_DIM, t, zero)], axis=0)


def _merge_heads_rows(t2):
    n = t2.shape[0] // 2
    lane = _lane_ids((n, LANES))
    return jnp.where(lane < HEAD_DIM, t2[:n], t2[n:])


def _inproj_kernel(x_ref, g_ref, w_ref, gain_ref, cos_ref, sin_ref, o_ref, *, rope_tiles):
    x = x_ref[...]
    ms = jnp.mean(x * x, axis=-1, keepdims=True)
    h = (x * lax.rsqrt(ms + NORM_EPS) * g_ref[...]).astype(jnp.bfloat16)
    n_cols = o_ref.shape[1]
    lane = _lane_ids((x.shape[0], LANES))
    head0 = lane < HEAD_DIM
    first_half = (lane % HEAD_DIM) < (HEAD_DIM // 2)
    for c0 in range(0, n_cols, PROJ_COLS):
        r = jnp.dot(h, w_ref[:, c0:c0 + PROJ_COLS], preferred_element_type=jnp.float32)
        for t0 in range(0, PROJ_COLS, LANES):
            col = c0 + t0
            t = r[:, t0:t0 + LANES]
            if col // LANES in rope_tiles:
                sq = t * t
                ss0 = jnp.sum(jnp.where(head0, sq, 0.0), axis=-1, keepdims=True)
                ss1 = jnp.sum(jnp.where(head0, 0.0, sq), axis=-1, keepdims=True)
                inv = jnp.where(head0, lax.rsqrt(ss0 / HEAD_DIM + NORM_EPS),
                                lax.rsqrt(ss1 / HEAD_DIM + NORM_EPS))
                t = t * inv * gain_ref[:, col:col + LANES]
                rot = jnp.where(first_half, pltpu.roll(t, LANES - HEAD_DIM // 2, 1),
                                pltpu.roll(t, HEAD_DIM // 2, 1))
                t = t * cos_ref[...] + rot * sin_ref[...]
            o_ref[:, col:col + LANES] = t.astype(o_ref.dtype)


def _inproj(x2, g, w, gain, cos, sin, rope_tiles, seq):
    rows, d = x2.shape
    n_cols = w.shape[1]
    seq_tiles = seq // PROJ_ROWS
    return pl.pallas_call(
        functools.partial(_inproj_kernel, rope_tiles=rope_tiles),
        grid=(rows // PROJ_ROWS,),
        in_specs=[
            pl.BlockSpec((PROJ_ROWS, d), lambda i: (i, 0)),
            pl.BlockSpec((1, d), lambda i: (0, 0)),
            pl.BlockSpec((d, n_cols), lambda i: (0, 0)),
            pl.BlockSpec((1, n_cols), lambda i: (0, 0)),
            pl.BlockSpec((PROJ_ROWS, LANES), lambda i: (i % seq_tiles, 0)),
            pl.BlockSpec((PROJ_ROWS, LANES), lambda i: (i % seq_tiles, 0)),
        ],
        out_specs=pl.BlockSpec((PROJ_ROWS, n_cols), lambda i: (i, 0)),
        out_shape=jax.ShapeDtypeStruct((rows, n_cols), jnp.bfloat16),
        compiler_params=pltpu.CompilerParams(
            dimension_semantics=("parallel",), vmem_limit_bytes=VMEM_LIMIT),
        name="inproj",
    )(x2, g, w, gain, cos, sin)


def _outproj_kernel(*refs, n_parts):
    x_ref = refs[0]
    y_refs = refs[1:1 + n_parts]
    z_refs = refs[1 + n_parts:1 + 2 * n_parts]
    w_ref = refs[1 + 2 * n_parts]
    o_ref = refs[2 + 2 * n_parts]
    acc = x_ref[...]
    k0 = 0
    for y_ref, z_ref in zip(y_refs, z_refs):
        z = z_ref[...].astype(jnp.float32)
        gated = y_ref[...].astype(jnp.float32) * (z / (1.0 + jnp.exp(-z)))
        kw = y_ref.shape[1]
        acc = acc + jnp.dot(gated.astype(jnp.bfloat16), w_ref[k0:k0 + kw, :],
                            preferred_element_type=jnp.float32)
        k0 += kw
    o_ref[...] = acc


def _outproj(x2, ys, zs, w, part_width):
    rows, d = x2.shape
    n_parts = len(ys)
    in_specs = [pl.BlockSpec((PROJ_ROWS, d), lambda i: (i, 0))]
    for _, blk in list(ys) + list(zs):
        in_specs.append(pl.BlockSpec((PROJ_ROWS, part_width), lambda i, blk=blk: (i, blk)))
    in_specs.append(pl.BlockSpec(w.shape, lambda i: (0, 0)))
    return pl.pallas_call(
        functools.partial(_outproj_kernel, n_parts=n_parts),
        grid=(rows // PROJ_ROWS,),
        in_specs=in_specs,
        out_specs=pl.BlockSpec((PROJ_ROWS, d), lambda i: (i, 0)),
        out_shape=jax.ShapeDtypeStruct((rows, d), jnp.float32),
        compiler_params=pltpu.CompilerParams(
            dimension_semantics=("parallel",), vmem_limit_bytes=VMEM_LIMIT),
        name="outproj",
    )(x2, *[y for y, _ in ys], *[z for z, _ in zs], w)


def _band_block(q, kw, vw, first_key, first_block, sink_col):
    q2 = _split_heads_rows(q)
    s = lax.dot_general(q2, kw, _NT, preferred_element_type=jnp.float32)
    qi = lax.broadcasted_iota(jnp.int32, s.shape, 0) % BAND
    kj = lax.broadcasted_iota(jnp.int32, s.shape, 1)
    k_lo = jnp.where(first_block, BAND, 0)
    mask = (kj >= qi + first_key) & (kj <= qi + BAND) & (kj >= k_lo)
    s = jnp.where(mask, s, NEG)
    m = jnp.max(s, axis=-1, keepdims=True)
    if sink_col is not None:
        m = jnp.maximum(m, sink_col)
    p = jnp.exp(s - m)
    l = jnp.sum(p, axis=-1, keepdims=True)
    if sink_col is not None:
        l = l + jnp.exp(sink_col - m)
    pv = jnp.dot(p.astype(jnp.bfloat16), vw, preferred_element_type=jnp.float32)
    acc = _merge_heads_rows(pv)
    m_b = _merge_heads_rows(jnp.broadcast_to(m, pv.shape))
    l_b = _merge_heads_rows(jnp.broadcast_to(l, pv.shape))
    return acc, m_b, l_b


def _attn_a_kernel(q_ref, k_ref, v_ref, o_ref, qf, kf, vf, qd, kd, vd, m_s, l_s, acc_s):
    seq = q_ref.shape[0]
    n_blocks = seq // BAND
    qf[...] = q_ref[...].astype(jnp.float32)
    kf[...] = k_ref[...].astype(jnp.float32)
    vf[...] = v_ref[...].astype(jnp.float32)
    zeros = jnp.zeros((BAND, LANES), jnp.bfloat16)
    kd[0:BAND, :] = zeros
    vd[0:BAND, :] = zeros

    for dil in DILATIONS:
        sub = seq // dil
        blocks_per_sub = sub // BAND
        if dil == 1:
            qd[...] = q_ref[...]
            kd[BAND:, :] = k_ref[...]
            vd[BAND:, :] = v_ref[...]
        else:
            for r in range(dil):
                rows = pl.ds(r, sub, stride=dil)
                qd[r * sub:(r + 1) * sub, :] = qf[rows, :].astype(jnp.bfloat16)
                kd[BAND + r * sub:BAND + (r + 1) * sub, :] = kf[rows, :].astype(jnp.bfloat16)
                vd[BAND + r * sub:BAND + (r + 1) * sub, :] = vf[rows, :].astype(jnp.bfloat16)

        def body(it, carry, dil=dil, blocks_per_sub=blocks_per_sub):
            for u in range(BAND_UNROLL):
                one_block(it * BAND_UNROLL + u, dil, blocks_per_sub)
            return carry

        def one_block(n, dil, blocks_per_sub):
            r = n // blocks_per_sub
            c = n % blocks_per_sub
            row0 = pl.multiple_of(n * BAND, BAND)
            q = qd[pl.ds(row0, BAND), :]
            kw = kd[pl.ds(row0, 2 * BAND), :]
            vw = vd[pl.ds(row0, 2 * BAND), :]
            acc, m_b, l_b = _band_block(q, kw, vw, 0, c == 0, None)
            if dil == 1:
                rows = pl.ds(row0, BAND)
                m_s[rows, :] = m_b
                l_s[rows, :] = l_b
                acc_s[rows, :] = acc
            else:
                rows = pl.ds(c * (BAND * dil) + r, BAND, stride=dil)
                m_old = m_s[rows, :]
                m_new = jnp.maximum(m_old, m_b)
                a_old = jnp.exp(m_old - m_new)
                a_new = jnp.exp(m_b - m_new)
                m_s[rows, :] = m_new
                l_s[rows, :] = a_old * l_s[rows, :] + a_new * l_b
                acc_s[rows, :] = a_old * acc_s[rows, :] + a_new * acc

        lax.fori_loop(0, n_blocks // BAND_UNROLL, body, 0)

    o_ref[...] = (acc_s[...] / l_s[...]).astype(o_ref.dtype)


def _attn_a(proj, batch, seq):
    n_pairs = A_WIDTH // LANES
    blocks_per_batch = 1

    def col_spec(group):
        return pl.BlockSpec((seq, LANES), lambda b, p, group=group: (b, group * n_pairs + p))

    f32 = jnp.float32
    bf16 = jnp.bfloat16
    del blocks_per_batch
    return pl.pallas_call(
        _attn_a_kernel,
        grid=(batch, n_pairs),
        in_specs=[col_spec(0), col_spec(1), col_spec(2)],
        out_specs=pl.BlockSpec((seq, LANES), lambda b, p: (b, p)),
        out_shape=jax.ShapeDtypeStruct((batch * seq, A_WIDTH), bf16),
        scratch_shapes=[
            pltpu.VMEM((seq, LANES), f32), pltpu.VMEM((seq, LANES), f32), pltpu.VMEM((seq, LANES), f32),
            pltpu.VMEM((seq, LANES), bf16),
            pltpu.VMEM((seq + BAND, LANES), bf16), pltpu.VMEM((seq + BAND, LANES), bf16),
            pltpu.VMEM((seq, LANES), f32), pltpu.VMEM((seq, LANES), f32), pltpu.VMEM((seq, LANES), f32),
        ],
        compiler_params=pltpu.CompilerParams(
            dimension_semantics=("parallel", "parallel"), vmem_limit_bytes=VMEM_LIMIT),
        name="attn_dilated",
    )(proj, proj, proj)


def _attn_c_kernel(sinks_ref, q_ref, k_ref, v_ref, o_ref, kd, vd):
    seq = q_ref.shape[0]
    pair = pl.program_id(1)
    zeros = jnp.zeros((BAND, LANES), jnp.bfloat16)
    kd[0:BAND, :] = zeros
    vd[0:BAND, :] = zeros
    kd[BAND:, :] = k_ref[...]
    vd[BAND:, :] = v_ref[...]
    row = lax.broadcasted_iota(jnp.int32, (2 * BAND, 1), 0)
    sink_col = jnp.where(row < BAND, sinks_ref[PAIR * pair], sinks_ref[PAIR * pair + 1])

    def body(it, carry):
        for u in range(BAND_UNROLL):
            n = it * BAND_UNROLL + u
            row0 = pl.multiple_of(n * BAND, BAND)
            q = q_ref[pl.ds(row0, BAND), :]
            kw = kd[pl.ds(row0, 2 * BAND), :]
            vw = vd[pl.ds(row0, 2 * BAND), :]
            acc, _, l_b = _band_block(q, kw, vw, 1, n == 0, sink_col)
            o_ref[pl.ds(row0, BAND), :] = (acc / l_b).astype(o_ref.dtype)
        return carry

    lax.fori_loop(0, seq // (BAND * BAND_UNROLL), body, 0)


def _attn_c(proj, sinks, batch, seq):
    n_pairs = C_WIDTH // LANES
    pairs_per_kv = C_GROUP // PAIR
    k_tile0 = C_WIDTH // LANES
    v_tile0 = k_tile0 + C_KV_HEADS
    bf16 = jnp.bfloat16
    return pl.pallas_call(
        _attn_c_kernel,
        grid=(batch, n_pairs),
        in_specs=[
            pl.BlockSpec(memory_space=pltpu.SMEM),
            pl.BlockSpec((seq, LANES), lambda b, p: (b, p)),
            pl.BlockSpec((seq, LANES), lambda b, p: (b, k_tile0 + p // pairs_per_kv)),
            pl.BlockSpec((seq, LANES), lambda b, p: (b, v_tile0 + p // pairs_per_kv)),
        ],
        out_specs=pl.BlockSpec((seq, LANES), lambda b, p: (b, p)),
        out_shape=jax.ShapeDtypeStruct((batch * seq, C_WIDTH), bf16),
        scratch_shapes=[pltpu.VMEM((seq + BAND, LANES), bf16), pltpu.VMEM((seq + BAND, LANES), bf16)],
        compiler_params=pltpu.CompilerParams(
            dimension_semantics=("parallel", "parallel"), vmem_limit_bytes=VMEM_LIMIT),
        name="attn_swa",
    )(sinks, proj, proj, proj)


def _moba_kernel(q_ref, k_ref, v_ref, o_ref, vt, kmean, bias_s, m_s, l_s, acc_s):
    seq = q_ref.shape[0]
    nb = seq // MOBA_BLOCK
    nq2 = PAIR * MOBA_BLOCK

    def prep(j, carry):
        rows = pl.ds(pl.multiple_of(j * MOBA_BLOCK, MOBA_BLOCK), MOBA_BLOCK)
        vt[:, rows] = v_ref[rows, :].astype(jnp.float32).T.astype(jnp.bfloat16)
        kmean[pl.ds(j, 1), :] = jnp.mean(k_ref[rows, :].astype(jnp.float32), axis=0, keepdims=True)
        return carry

    lax.fori_loop(0, nb, prep, 0)
    km = kmean[...]
    km_hi = km.astype(jnp.bfloat16)
    km_lo = (km - km_hi.astype(jnp.float32)).astype(jnp.bfloat16)
    blk_id = lax.broadcasted_iota(jnp.int32, (nb, nq2), 0)
    chunk = MOBA_CHUNK_BLOCKS * MOBA_BLOCK
    key_id = lax.broadcasted_iota(jnp.int32, (chunk, nq2), 0)
    qry_id = lax.broadcasted_iota(jnp.int32, (chunk, nq2), 1) % MOBA_BLOCK

    def scores_t(j, q2):
        rows = pl.ds(pl.multiple_of(j * chunk, chunk), chunk)
        return lax.dot_general(k_ref[rows, :], q2, _NT, preferred_element_type=jnp.float32)

    def pv_t(j, p):
        rows = pl.ds(pl.multiple_of(j * chunk, chunk), chunk)
        return jnp.dot(vt[:, rows], p.astype(jnp.bfloat16), preferred_element_type=jnp.float32)

    def block_bias(j):
        return [bias_s[pl.ds(j * MOBA_CHUNK_BLOCKS + i, 1), :] for i in range(MOBA_CHUNK_BLOCKS)]

    def per_block(x, fn):
        return [fn(x[i * MOBA_BLOCK:(i + 1) * MOBA_BLOCK], i) for i in range(MOBA_CHUNK_BLOCKS)]

    def per_query_block(qb, carry):
        qrows = pl.ds(pl.multiple_of(qb * MOBA_BLOCK, MOBA_BLOCK), MOBA_BLOCK)
        q2 = _split_heads_rows(q_ref[qrows, :])
        gate = (lax.dot_general(km_hi, q2, _NT, preferred_element_type=jnp.float32)
                + lax.dot_general(km_lo, q2, _NT, preferred_element_type=jnp.float32))
        past = blk_id < qb
        gate = jnp.where(past, gate, NEG)
        rank = jnp.zeros((nb, nq2), jnp.int32)
        for j in range(nb - 1):
            gj = gate[j:j + 1, :]
            beats = (gj > gate) | ((gj == gate) & (j < blk_id))
            rank = rank + jnp.where(beats & (j < qb), 1, 0)
        allowed = (past & (rank < MOBA_TOPK)) | (blk_id == qb)
        bias_s[...] = jnp.where(allowed, 0.0, NEG)

        j_own = qb // MOBA_CHUNK_BLOCKS
        s = scores_t(j_own, q2)
        causal = (j_own * chunk + key_id) <= (qb * MOBA_BLOCK + qry_id)
        bias = block_bias(j_own)
        s = jnp.where(causal, jnp.concatenate(per_block(s, lambda x, i: x + bias[i]), axis=0), NEG)
        m0 = jnp.max(s, axis=0, keepdims=True)
        p = jnp.exp(s - m0)
        m_s[...] = m0
        l_s[...] = jnp.sum(p, axis=0, keepdims=True)
        acc_s[...] = pv_t(j_own, p)

        def past_chunk(j, c):
            s = scores_t(j, q2)
            bias = block_bias(j)
            m_old = m_s[...]
            m_blk = per_block(s, lambda x, i: jnp.max(x, axis=0, keepdims=True) + bias[i])
            m_new = functools.reduce(jnp.maximum, m_blk, m_old)
            alpha = jnp.exp(m_old - m_new)
            p = jnp.concatenate(per_block(s, lambda x, i: jnp.exp(x - (m_new - bias[i]))), axis=0)
            m_s[...] = m_new
            l_s[...] = alpha * l_s[...] + jnp.sum(p, axis=0, keepdims=True)
            acc_s[...] = alpha * acc_s[...] + pv_t(j, p)
            return c

        lax.fori_loop(0, j_own, past_chunk, 0)

        inv_l = 1.0 / l_s[...]
        o_t = jnp.concatenate(
            [acc_s[0:HEAD_DIM, 0:MOBA_BLOCK] * inv_l[:, 0:MOBA_BLOCK],
             acc_s[HEAD_DIM:LANES, MOBA_BLOCK:nq2] * inv_l[:, MOBA_BLOCK:nq2]], axis=0)
        o_ref[qrows, :] = o_t.T.astype(o_ref.dtype)
        return carry

    lax.fori_loop(0, nb, per_query_block, 0)


def _attn_b(proj, batch, seq):
    n_pairs = B_WIDTH // LANES
    tile0 = 4 * (A_WIDTH // LANES)

    def col_spec(group):
        return pl.BlockSpec((seq, LANES), lambda b, p, group=group: (b, tile0 + group * n_pairs + p))

    nb = seq // MOBA_BLOCK
    nq2 = PAIR * MOBA_BLOCK
    f32 = jnp.float32
    return pl.pallas_call(
        _moba_kernel,
        grid=(batch, n_pairs),
        in_specs=[col_spec(0), col_spec(1), col_spec(2)],
        out_specs=pl.BlockSpec((seq, LANES), lambda b, p: (b, p)),
        out_shape=jax.ShapeDtypeStruct((batch * seq, B_WIDTH), jnp.bfloat16),
        scratch_shapes=[
            pltpu.VMEM((LANES, seq), jnp.bfloat16),
            pltpu.VMEM((nb, LANES), f32),
            pltpu.VMEM((nb, nq2), f32),
            pltpu.VMEM((1, nq2), f32), pltpu.VMEM((1, nq2), f32),
            pltpu.VMEM((LANES, nq2), f32),
        ],
        compiler_params=pltpu.CompilerParams(
            dimension_semantics=("parallel", "parallel"), vmem_limit_bytes=VMEM_LIMIT),
        name="attn_moba",
    )(proj, proj, proj)


def _rope_tables(seq):
    pos = jnp.arange(seq, dtype=jnp.float32)
    inv_freq = ROPE_THETA ** (-jnp.arange(0, HEAD_DIM, 2, dtype=jnp.float32) / HEAD_DIM)
    ang = pos[:, None] * inv_freq[None, :]
    cos = jnp.concatenate([jnp.cos(ang)] * (2 * PAIR), axis=-1)
    sin = jnp.concatenate([-jnp.sin(ang), jnp.sin(ang)] * PAIR, axis=-1)
    return cos, sin


def _tile_gain(g, width):
    return jnp.tile(g, width // HEAD_DIM)


def kernel(x, norm_even, w_in_even, w_out_even, qnorm_a, knorm_a, qnorm_b, knorm_b,
           norm_odd, w_in_odd, w_out_odd, qnorm_c, knorm_c, sinks_c):
    batch, seq, d = x.shape
    bf16 = jnp.bfloat16
    scale = HEAD_DIM ** -0.5
    cos, sin = _rope_tables(seq)
    x2 = x.reshape(batch * seq, d)

    ones = jnp.ones((A_WIDTH,), jnp.float32)
    gain0 = jnp.concatenate([
        _tile_gain(qnorm_a[0], A_WIDTH) * scale, _tile_gain(knorm_a[0], A_WIDTH), ones, ones,
        _tile_gain(qnorm_b[0], B_WIDTH) * scale, _tile_gain(knorm_b[0], B_WIDTH), ones, ones])[None, :]
    tiles_per_group = A_WIDTH // LANES
    rope0 = frozenset(t for grp in (0, 1, 4, 5) for t in range(grp * tiles_per_group, (grp + 1) * tiles_per_group))
    proj0 = _inproj(x2, norm_even[0][None, :], w_in_even[0].astype(bf16), gain0, cos, sin, rope0, seq)
    oa = _attn_a(proj0, batch, seq)
    ob = _attn_b(proj0, batch, seq)
    x1 = _outproj(x2, [(oa, 0), (ob, 0)], [(proj0, 3), (proj0, 7)], w_out_even[0].astype(bf16), A_WIDTH)

    w1 = w_in_odd[0]
    kv_w = C_KV_HEADS * HEAD_DIM
    wq, wk, wv, wz = (w1[:, :C_WIDTH], w1[:, C_WIDTH:C_WIDTH + kv_w],
                      w1[:, C_WIDTH + kv_w:C_WIDTH + 2 * kv_w], w1[:, C_WIDTH + 2 * kv_w:])

    def dup_heads(w):
        return jnp.concatenate([w[:, h * HEAD_DIM:(h + 1) * HEAD_DIM]
                                for h in range(C_KV_HEADS) for _ in range(PAIR)], axis=1)

    w1r = jnp.concatenate([wq, dup_heads(wk), dup_heads(wv), wz], axis=1).astype(bf16)
    dup_w = PAIR * kv_w
    gain1 = jnp.concatenate([
        _tile_gain(qnorm_c[0], C_WIDTH) * scale, _tile_gain(knorm_c[0], dup_w),
        jnp.ones((dup_w + C_WIDTH,), jnp.float32)])[None, :]
    rope1 = frozenset(range((C_WIDTH + dup_w) // LANES))
    proj1 = _inproj(x1, norm_odd[0][None, :], w1r, gain1, cos, sin, rope1, seq)
    oc = _attn_c(proj1, sinks_c[0], batch, seq)
    half = C_WIDTH // 2
    z_blk = (C_WIDTH + 2 * dup_w) // half
    out = _outproj(x1, [(oc, 0), (oc, 1)], [(proj1, z_blk), (proj1, z_blk + 1)],
                   w_out_odd[0].astype(bf16), half)
    return out.reshape(batch, seq, d)
```

```python
import functools

import jax
import jax.numpy as jnp
from jax import lax
from jax.experimental import pallas as pl
from jax.experimental.pallas import tpu as pltpu

D_MODEL = 1024
HEAD_DIM = 64
LANES = 128
PAIR = LANES // HEAD_DIM
A_WIDTH = 512
B_WIDTH = 512
C_WIDTH = 1024
C_KV_HEADS = 2
C_GROUP = 8
DILATIONS = (1, 4, 16)
BAND = 128
MOBA_BLOCK = 256
MOBA_TOPK = 3
MOBA_CHUNK_BLOCKS = 2
BAND_UNROLL = 8
PREP_UNROLL = 4
SUM_ROWS = 16
LOG2E = 1.4426950408889634
ROPE_THETA = 10000.0
NORM_EPS = 1e-6
NEG = -1e30
VMEM_LIMIT = 56 * 1024 * 1024
PROJ_ROWS = 512
PROJ_COLS = 512

_NT = (((1,), (1,)), ((), ()))


def _lane_ids(shape):
    return lax.broadcasted_iota(jnp.int32, shape, len(shape) - 1)


def _split_heads_rows(t):
    lane = _lane_ids(t.shape)
    zero = jnp.zeros_like(t)
    return jnp.concatenate([jnp.where(lane < HEAD_DIM, t, zero),
                            jnp.where(lane >= HEAD_DIM, t, zero)], axis=0)


def _inproj_kernel(x_ref, g_ref, w_ref, gain_ref, cos_ref, sin_ref, o_ref, *, rope_tiles):
    x = x_ref[...]
    ms = jnp.mean(x * x, axis=-1, keepdims=True)
    h = (x * lax.rsqrt(ms + NORM_EPS) * g_ref[...]).astype(jnp.bfloat16)
    n_cols = o_ref.shape[1]
    lane = _lane_ids((x.shape[0], LANES))
    head0 = lane < HEAD_DIM
    first_half = (lane % HEAD_DIM) < (HEAD_DIM // 2)
    for c0 in range(0, n_cols, PROJ_COLS):
        r = jnp.dot(h, w_ref[:, c0:c0 + PROJ_COLS], preferred_element_type=jnp.float32)
        for t0 in range(0, PROJ_COLS, LANES):
            col = c0 + t0
            t = r[:, t0:t0 + LANES]
            if col // LANES in rope_tiles:
                sq = t * t
                ss0 = jnp.sum(jnp.where(head0, sq, 0.0), axis=-1, keepdims=True)
                ss1 = jnp.sum(jnp.where(head0, 0.0, sq), axis=-1, keepdims=True)
                inv = jnp.where(head0, lax.rsqrt(ss0 / HEAD_DIM + NORM_EPS),
                                lax.rsqrt(ss1 / HEAD_DIM + NORM_EPS))
                t = t * inv * gain_ref[:, col:col + LANES]
                rot = jnp.where(first_half, pltpu.roll(t, LANES - HEAD_DIM // 2, 1),
                                pltpu.roll(t, HEAD_DIM // 2, 1))
                t = t * cos_ref[...] + rot * sin_ref[...]
            o_ref[:, col:col + LANES] = t.astype(o_ref.dtype)


def _inproj(x2, g, w, gain, cos, sin, rope_tiles, seq):
    rows, d = x2.shape
    n_cols = w.shape[1]
    seq_tiles = seq // PROJ_ROWS
    return pl.pallas_call(
        functools.partial(_inproj_kernel, rope_tiles=rope_tiles),
        grid=(rows // PROJ_ROWS,),
        in_specs=[
            pl.BlockSpec((PROJ_ROWS, d), lambda i: (i, 0)),
            pl.BlockSpec((1, d), lambda i: (0, 0)),
            pl.BlockSpec((d, n_cols), lambda i: (0, 0)),
            pl.BlockSpec((1, n_cols), lambda i: (0, 0)),
            pl.BlockSpec((PROJ_ROWS, LANES), lambda i: (i % seq_tiles, 0)),
            pl.BlockSpec((PROJ_ROWS, LANES), lambda i: (i % seq_tiles, 0)),
        ],
        out_specs=pl.BlockSpec((PROJ_ROWS, n_cols), lambda i: (i, 0)),
        out_shape=jax.ShapeDtypeStruct((rows, n_cols), jnp.bfloat16),
        compiler_params=pltpu.CompilerParams(
            dimension_semantics=("parallel",), vmem_limit_bytes=VMEM_LIMIT),
        name="inproj",
    )(x2, g, w, gain, cos, sin)


def _outproj_kernel(*refs, n_parts):
    x_ref = refs[0]
    y_refs = refs[1:1 + n_parts]
    z_refs = refs[1 + n_parts:1 + 2 * n_parts]
    w_ref = refs[1 + 2 * n_parts]
    o_ref = refs[2 + 2 * n_parts]
    acc = x_ref[...]
    k0 = 0
    for y_ref, z_ref in zip(y_refs, z_refs):
        z = z_ref[...].astype(jnp.float32)
        gated = y_ref[...].astype(jnp.float32) * (z / (1.0 + jnp.exp(-z)))
        kw = y_ref.shape[1]
        acc = acc + jnp.dot(gated.astype(jnp.bfloat16), w_ref[k0:k0 + kw, :],
                            preferred_element_type=jnp.float32)
        k0 += kw
    o_ref[...] = acc


def _outproj(x2, ys, zs, w, part_width):
    rows, d = x2.shape
    n_parts = len(ys)
    in_specs = [pl.BlockSpec((PROJ_ROWS, d), lambda i: (i, 0))]
    for _, blk in list(ys) + list(zs):
        in_specs.append(pl.BlockSpec((PROJ_ROWS, part_width), lambda i, blk=blk: (i, blk)))
    in_specs.append(pl.BlockSpec(w.shape, lambda i: (0, 0)))
    return pl.pallas_call(
        functools.partial(_outproj_kernel, n_parts=n_parts),
        grid=(rows // PROJ_ROWS,),
        in_specs=in_specs,
        out_specs=pl.BlockSpec((PROJ_ROWS, d), lambda i: (i, 0)),
        out_shape=jax.ShapeDtypeStruct((rows, d), jnp.float32),
        compiler_params=pltpu.CompilerParams(
            dimension_semantics=("parallel",), vmem_limit_bytes=VMEM_LIMIT),
        name="outproj",
    )(x2, *[y for y, _ in ys], *[z for z, _ in zs], w)


def _fill_band_bias(bias_ref, first_key):
    kj = lax.broadcasted_iota(jnp.int32, (2 * BAND, PAIR * BAND), 0)
    qi = lax.broadcasted_iota(jnp.int32, (2 * BAND, PAIR * BAND), 1) % BAND
    band = (kj >= qi + first_key) & (kj <= qi + BAND)
    bias_ref[0] = jnp.where(band, 0.0, NEG)
    bias_ref[1] = jnp.where(band & (kj >= BAND), 0.0, NEG)


def _band_blocks(operands, sink_row):
    scores = [lax.dot_general(kw, _split_heads_rows(q), _NT, preferred_element_type=jnp.float32) + bias
              for q, kw, _, bias in operands]
    probs = []
    for s in scores:
        m = jnp.max(s, axis=0, keepdims=True)
        if sink_row is not None:
            m = jnp.maximum(m, sink_row)
        probs.append((jnp.exp2(s - m).astype(jnp.bfloat16), m))
    return [(jnp.dot(vtw, p, preferred_element_type=jnp.float32), m)
            for (_, _, vtw, _), (p, m) in zip(operands, probs)]


def _pair_tile(x):
    return jnp.concatenate([x[0:HEAD_DIM, 0:BAND], x[HEAD_DIM:LANES, BAND:PAIR * BAND]], axis=0)


def _pair_stat_tile(v):
    return jnp.concatenate([jnp.broadcast_to(v[:, 0:BAND], (HEAD_DIM, BAND)),
                            jnp.broadcast_to(v[:, BAND:PAIR * BAND], (HEAD_DIM, BAND))], axis=0)


def _attn_a_kernel(q_ref, k_ref, v_ref, o_ref, qf, kf, vf, qd, kd, vdt, bias_s, m_s, l_s, acc_s):
    seq = q_ref.shape[0]
    n_blocks = seq // BAND
    piece = 2 * BAND
    qf[...] = q_ref[...].astype(jnp.float32)
    kf[...] = k_ref[...].astype(jnp.float32)
    vf[...] = v_ref[...].astype(jnp.float32)
    kd[0:BAND, :] = jnp.zeros((BAND, LANES), kd.dtype)
    vdt[:, 0:BAND] = jnp.zeros((LANES + SUM_ROWS, BAND), vdt.dtype)
    vdt[LANES:, BAND:] = jnp.ones((SUM_ROWS, seq), vdt.dtype)
    _fill_band_bias(bias_s, 0)

    for dil in DILATIONS:
        sub = seq // dil
        blocks_per_sub = sub // BAND
        pieces_per_sub = sub // piece

        def deinterleave(i, carry, dil=dil, pieces_per_sub=pieces_per_sub):
            dst = pl.multiple_of(i * piece, piece)
            if dil == 1:
                src = pl.ds(dst, piece)
                qd[pl.ds(dst, piece), :] = q_ref[src, :]
                kd[pl.ds(BAND + dst, piece), :] = k_ref[src, :]
            else:
                r = i // pieces_per_sub
                j = i % pieces_per_sub
                src = pl.ds(r + j * (piece * dil), piece, stride=dil)
                qd[pl.ds(dst, piece), :] = qf[src, :].astype(qd.dtype)
                kd[pl.ds(BAND + dst, piece), :] = kf[src, :].astype(kd.dtype)
            vdt[0:LANES, pl.ds(BAND + dst, piece)] = vf[src, :].T.astype(vdt.dtype)
            return carry

        lax.fori_loop(0, seq // piece, deinterleave, 0, unroll=PREP_UNROLL)

        def body(it, carry, dil=dil, blocks_per_sub=blocks_per_sub):
            blocks = [it * BAND_UNROLL + u for u in range(BAND_UNROLL)]
            operands = []
            for n in blocks:
                row0 = pl.multiple_of(n * BAND, BAND)
                first = (n % blocks_per_sub) == 0
                operands.append((qd[pl.ds(row0, BAND), :], kd[pl.ds(row0, 2 * BAND), :],
                                 vdt[:, pl.ds(row0, 2 * BAND)], bias_s[jnp.where(first, 1, 0)]))
            for n, (o_t, m) in zip(blocks, _band_blocks(operands, None)):
                merge_block(n, o_t, m, dil, blocks_per_sub)
            return carry

        def merge_block(n, o_t, m, dil, blocks_per_sub):
            r = n // blocks_per_sub
            c = n % blocks_per_sub
            row0 = pl.multiple_of(n * BAND, BAND)
            acc = _pair_tile(o_t).T
            m_b = _pair_stat_tile(m).T
            l_b = _pair_stat_tile(o_t[LANES:LANES + 1, :]).T
            if dil == 1:
                rows = pl.ds(row0, BAND)
                m_s[rows, :] = m_b
                l_s[rows, :] = l_b
                acc_s[rows, :] = acc
            else:
                rows = pl.ds(c * (BAND * dil) + r, BAND, stride=dil)
                m_old = m_s[rows, :]
                m_new = jnp.maximum(m_old, m_b)
                a_old = jnp.exp2(m_old - m_new)
                a_new = jnp.exp2(m_b - m_new)
                m_s[rows, :] = m_new
                l_s[rows, :] = a_old * l_s[rows, :] + a_new * l_b
                acc_s[rows, :] = a_old * acc_s[rows, :] + a_new * acc

        lax.fori_loop(0, n_blocks // BAND_UNROLL, body, 0)

    o_ref[...] = (acc_s[...] / l_s[...]).astype(o_ref.dtype)


def _attn_a(proj, batch, seq):
    n_pairs = A_WIDTH // LANES
    blocks_per_batch = 1

    def col_spec(group):
        return pl.BlockSpec((seq, LANES), lambda b, p, group=group: (b, group * n_pairs + p))

    f32 = jnp.float32
    bf16 = jnp.bfloat16
    del blocks_per_batch
    return pl.pallas_call(
        _attn_a_kernel,
        grid=(batch, n_pairs),
        in_specs=[col_spec(0), col_spec(1), col_spec(2)],
        out_specs=pl.BlockSpec((seq, LANES), lambda b, p: (b, p)),
        out_shape=jax.ShapeDtypeStruct((batch * seq, A_WIDTH), bf16),
        scratch_shapes=[
            pltpu.VMEM((seq, LANES), f32), pltpu.VMEM((seq, LANES), f32), pltpu.VMEM((seq, LANES), f32),
            pltpu.VMEM((seq, LANES), bf16),
            pltpu.VMEM((seq + BAND, LANES), bf16),
            pltpu.VMEM((LANES + SUM_ROWS, seq + BAND), bf16),
            pltpu.VMEM((2, 2 * BAND, PAIR * BAND), f32),
            pltpu.VMEM((seq, LANES), f32), pltpu.VMEM((seq, LANES), f32), pltpu.VMEM((seq, LANES), f32),
        ],
        compiler_params=pltpu.CompilerParams(
            dimension_semantics=("parallel", "parallel"), vmem_limit_bytes=VMEM_LIMIT),
        name="attn_dilated",
    )(proj, proj, proj)


def _attn_c_kernel(sinks_ref, q_ref, k_ref, v_ref, o_ref, kd, vdt, bias_s):
    seq = q_ref.shape[0]
    pair = pl.program_id(1)
    piece = 2 * BAND
    kd[0:BAND, :] = jnp.zeros((BAND, LANES), kd.dtype)
    kd[BAND:, :] = k_ref[...]
    vdt[:, 0:BAND] = jnp.zeros((LANES + SUM_ROWS, BAND), vdt.dtype)
    vdt[LANES:, BAND:] = jnp.ones((SUM_ROWS, seq), vdt.dtype)

    def transpose_v(i, carry):
        rows = pl.ds(pl.multiple_of(i * piece, piece), piece)
        vdt[0:LANES, pl.ds(pl.multiple_of(BAND + i * piece, BAND), piece)] = (
            v_ref[rows, :].astype(jnp.float32).T.astype(vdt.dtype))
        return carry

    lax.fori_loop(0, seq // piece, transpose_v, 0, unroll=PREP_UNROLL)
    _fill_band_bias(bias_s, 1)
    col = lax.broadcasted_iota(jnp.int32, (1, PAIR * BAND), 1)
    sink_row = jnp.where(col < BAND, sinks_ref[PAIR * pair], sinks_ref[PAIR * pair + 1]) * LOG2E

    def body(it, carry):
        blocks = [it * BAND_UNROLL + u for u in range(BAND_UNROLL)]
        operands = []
        for n in blocks:
            row0 = pl.multiple_of(n * BAND, BAND)
            operands.append((q_ref[pl.ds(row0, BAND), :], kd[pl.ds(row0, 2 * BAND), :],
                             vdt[:, pl.ds(row0, 2 * BAND)], bias_s[jnp.where(n == 0, 1, 0)]))
        for n, (o_t, m) in zip(blocks, _band_blocks(operands, sink_row)):
            l = o_t[LANES:LANES + 1, :] + jnp.exp2(sink_row - m)
            out_t = _pair_tile(o_t) * _pair_stat_tile(1.0 / l)
            o_ref[pl.ds(pl.multiple_of(n * BAND, BAND), BAND), :] = out_t.T.astype(o_ref.dtype)
        return carry

    lax.fori_loop(0, seq // (BAND * BAND_UNROLL), body, 0)


def _attn_c(proj, sinks, batch, seq):
    n_pairs = C_WIDTH // LANES
    pairs_per_kv = C_GROUP // PAIR
    k_tile0 = C_WIDTH // LANES
    v_tile0 = k_tile0 + C_KV_HEADS
    bf16 = jnp.bfloat16
    return pl.pallas_call(
        _attn_c_kernel,
        grid=(batch, n_pairs),
        in_specs=[
            pl.BlockSpec(memory_space=pltpu.SMEM),
            pl.BlockSpec((seq, LANES), lambda b, p: (b, p)),
            pl.BlockSpec((seq, LANES), lambda b, p: (b, k_tile0 + p // pairs_per_kv)),
            pl.BlockSpec((seq, LANES), lambda b, p: (b, v_tile0 + p // pairs_per_kv)),
        ],
        out_specs=pl.BlockSpec((seq, LANES), lambda b, p: (b, p)),
        out_shape=jax.ShapeDtypeStruct((batch * seq, C_WIDTH), bf16),
        scratch_shapes=[
            pltpu.VMEM((seq + BAND, LANES), bf16),
            pltpu.VMEM((LANES + SUM_ROWS, seq + BAND), bf16),
            pltpu.VMEM((2, 2 * BAND, PAIR * BAND), jnp.float32),
        ],
        compiler_params=pltpu.CompilerParams(
            dimension_semantics=("parallel", "parallel"), vmem_limit_bytes=VMEM_LIMIT),
        name="attn_swa",
    )(sinks, proj, proj, proj)


_T_QB, _T_CH, _T_FIRST, _T_PREV_CH, _T_PREV_QB, _T_PREV_LAST, _T_ROWS = range(7)


def _moba_step_table(nb):
    qbs, chs, firsts, lasts = [], [], [], []
    for qb in range(nb):
        own = qb // MOBA_CHUNK_BLOCKS
        order = [own] + list(range(own))
        for i, ch in enumerate(order):
            qbs.append(qb)
            chs.append(ch)
            firsts.append(int(i == 0))
            lasts.append(int(i == len(order) - 1))
    n_steps = len(qbs)
    rows = [None] * _T_ROWS
    rows[_T_QB] = qbs + [qbs[-1]]
    rows[_T_CH] = chs + [chs[-1]]
    rows[_T_FIRST] = firsts + [0]
    rows[_T_PREV_CH] = [chs[0]] + chs
    rows[_T_PREV_QB] = [qbs[0]] + qbs
    rows[_T_PREV_LAST] = [0] + lasts
    return n_steps, [v for row in rows for v in row]


def _moba_kernel(tbl_ref, q_ref, k_ref, v_ref, o_ref, vt, q2_all, kmean, bias_s, m_s, alpha_s, acc_s,
                 s_a, s_b, p_a, p_b, *, n_steps):
    seq = q_ref.shape[0]
    nb = seq // MOBA_BLOCK
    nq2 = PAIR * MOBA_BLOCK

    def tbl(row, t):
        return tbl_ref[row * (n_steps + 1) + t]

    def prep(j, carry):
        rows = pl.ds(pl.multiple_of(j * MOBA_BLOCK, MOBA_BLOCK), MOBA_BLOCK)
        vt[0:LANES, rows] = v_ref[rows, :].astype(jnp.float32).T.astype(jnp.bfloat16)
        kmean[pl.ds(j, 1), :] = jnp.mean(k_ref[rows, :].astype(jnp.float32), axis=0, keepdims=True)
        q2_all[pl.ds(pl.multiple_of(j * nq2, nq2), nq2), :] = _split_heads_rows(q_ref[rows, :])
        return carry

    lax.fori_loop(0, nb, prep, 0, unroll=PREP_UNROLL)
    vt[LANES:, :] = jnp.ones((SUM_ROWS, seq), vt.dtype)
    km = kmean[...]
    km_hi = km.astype(jnp.bfloat16)
    km_lo = (km - km_hi.astype(jnp.float32)).astype(jnp.bfloat16)
    blk_id = lax.broadcasted_iota(jnp.int32, (nb, nq2), 0)
    chunk = MOBA_CHUNK_BLOCKS * MOBA_BLOCK
    key_id = lax.broadcasted_iota(jnp.int32, (chunk, nq2), 0)
    qry_id = lax.broadcasted_iota(jnp.int32, (chunk, nq2), 1) % MOBA_BLOCK

    def chunk_rows(ch):
        return pl.ds(pl.multiple_of(ch * chunk, chunk), chunk)

    def query_rows(qb):
        return pl.ds(pl.multiple_of(qb * MOBA_BLOCK, MOBA_BLOCK), MOBA_BLOCK)

    def split_query_rows(qb):
        return pl.ds(pl.multiple_of(qb * nq2, nq2), nq2)

    def scores_t(t):
        q2 = q2_all[split_query_rows(tbl(_T_QB, t)), :]
        return lax.dot_general(k_ref[chunk_rows(tbl(_T_CH, t)), :], q2, _NT,
                               preferred_element_type=jnp.float32)

    def accumulate_prev(t, p_prev, alpha_prev):
        pv = jnp.dot(vt[:, chunk_rows(tbl(_T_PREV_CH, t))], p_prev[...], preferred_element_type=jnp.float32)
        acc_s[...] = alpha_prev * acc_s[...] + pv

    def block_bias(ch):
        return [bias_s[pl.ds(ch * MOBA_CHUNK_BLOCKS + i, 1), :] for i in range(MOBA_CHUNK_BLOCKS)]

    def per_block(x, fn):
        return [fn(x[i * MOBA_BLOCK:(i + 1) * MOBA_BLOCK], i) for i in range(MOBA_CHUNK_BLOCKS)]

    def select_blocks(qb):
        q2 = q2_all[split_query_rows(qb), :]
        gate = (lax.dot_general(km_hi, q2, _NT, preferred_element_type=jnp.float32)
                + lax.dot_general(km_lo, q2, _NT, preferred_element_type=jnp.float32))
        past = blk_id < qb
        gate = jnp.where(past, gate, NEG)
        rank = jnp.zeros((nb, nq2), jnp.int32)
        for j in range(nb - 1):
            gj = gate[j:j + 1, :]
            beats = (gj > gate) | ((gj == gate) & (j < blk_id))
            rank = rank + jnp.where(beats & (j < qb), 1, 0)
        allowed = (past & (rank < MOBA_TOPK)) | (blk_id == qb)
        bias_s[...] = jnp.where(allowed, 0.0, NEG)

    def softmax_first(t, s_cur, p_cur):
        qb = tbl(_T_QB, t)
        ch = tbl(_T_CH, t)
        select_blocks(qb)
        causal = (ch * chunk + key_id) <= (qb * MOBA_BLOCK + qry_id)
        bias = block_bias(ch)
        s = jnp.where(causal, jnp.concatenate(per_block(s_cur[...], lambda x, i: x + bias[i]), axis=0), NEG)
        m0 = jnp.max(s, axis=0, keepdims=True)
        p = jnp.exp2(s - m0)
        m_s[...] = m0
        alpha_s[...] = jnp.zeros_like(m0)
        p_cur[...] = p.astype(p_cur.dtype)

    def softmax_next(t, s_cur, p_cur):
        s = s_cur[...]
        bias = block_bias(tbl(_T_CH, t))
        m_old = m_s[...]
        m_blk = per_block(s, lambda x, i: jnp.max(x, axis=0, keepdims=True) + bias[i])
        m_new = functools.reduce(jnp.maximum, m_blk, m_old)
        alpha = jnp.exp2(m_old - m_new)
        p = jnp.concatenate(per_block(s, lambda x, i: jnp.exp2(x - (m_new - bias[i]))), axis=0)
        m_s[...] = m_new
        alpha_s[...] = alpha
        p_cur[...] = p.astype(p_cur.dtype)

    def finalize(qb):
        inv_l = 1.0 / acc_s[LANES:LANES + 1, :]
        o_t = jnp.concatenate(
            [acc_s[0:HEAD_DIM, 0:MOBA_BLOCK] * inv_l[:, 0:MOBA_BLOCK],
             acc_s[HEAD_DIM:LANES, MOBA_BLOCK:nq2] * inv_l[:, MOBA_BLOCK:nq2]], axis=0)
        o_ref[query_rows(qb), :] = o_t.T.astype(o_ref.dtype)

    def step(t, s_cur, s_nxt, p_cur, p_prev):
        def first():
            alpha_prev = alpha_s[...]
            softmax_first(t, s_cur, p_cur)
            s_nxt[...] = scores_t(t + 1)
            accumulate_prev(t, p_prev, alpha_prev)

        def later():
            alpha_prev = alpha_s[...]
            softmax_next(t, s_cur, p_cur)
            s_nxt[...] = scores_t(t + 1)
            accumulate_prev(t, p_prev, alpha_prev)

        lax.cond(tbl(_T_FIRST, t) == 1, first, later)

        @pl.when(tbl(_T_PREV_LAST, t) == 1)
        def _():
            finalize(tbl(_T_PREV_QB, t))

    s_a[...] = scores_t(0)
    p_b[...] = jnp.zeros_like(p_b)
    acc_s[...] = jnp.zeros_like(acc_s)
    alpha_s[...] = jnp.zeros_like(alpha_s)

    def two_steps(i, carry):
        step(2 * i, s_a, s_b, p_a, p_b)
        step(2 * i + 1, s_b, s_a, p_b, p_a)
        return carry

    lax.fori_loop(0, n_steps // 2, two_steps, 0)
    accumulate_prev(n_steps, p_b, alpha_s[...])
    finalize(nb - 1)


def _attn_b(proj, batch, seq):
    n_pairs = B_WIDTH // LANES
    tile0 = 4 * (A_WIDTH // LANES)

    def col_spec(group):
        return pl.BlockSpec((seq, LANES), lambda b, p, group=group: (b, tile0 + group * n_pairs + p))

    nb = seq // MOBA_BLOCK
    nq2 = PAIR * MOBA_BLOCK
    chunk = MOBA_CHUNK_BLOCKS * MOBA_BLOCK
    n_steps, table = _moba_step_table(nb)
    assert n_steps % 2 == 0 and nb % MOBA_CHUNK_BLOCKS == 0
    f32 = jnp.float32
    bf16 = jnp.bfloat16
    return pl.pallas_call(
        functools.partial(_moba_kernel, n_steps=n_steps),
        grid=(batch, n_pairs),
        in_specs=[pl.BlockSpec(memory_space=pltpu.SMEM), col_spec(0), col_spec(1), col_spec(2)],
        out_specs=pl.BlockSpec((seq, LANES), lambda b, p: (b, p)),
        out_shape=jax.ShapeDtypeStruct((batch * seq, B_WIDTH), bf16),
        scratch_shapes=[
            pltpu.VMEM((LANES + SUM_ROWS, seq), bf16),
            pltpu.VMEM((nb * nq2, LANES), bf16),
            pltpu.VMEM((nb, LANES), f32),
            pltpu.VMEM((nb, nq2), f32),
            pltpu.VMEM((1, nq2), f32),
            pltpu.VMEM((1, nq2), f32),
            pltpu.VMEM((LANES + SUM_ROWS, nq2), f32),
            pltpu.VMEM((chunk, nq2), f32), pltpu.VMEM((chunk, nq2), f32),
            pltpu.VMEM((chunk, nq2), bf16), pltpu.VMEM((chunk, nq2), bf16),
        ],
        compiler_params=pltpu.CompilerParams(
            dimension_semantics=("parallel", "parallel"), vmem_limit_bytes=VMEM_LIMIT),
        name="attn_moba",
    )(jnp.asarray(table, jnp.int32), proj, proj, proj)


def _rope_tables(seq):
    pos = jnp.arange(seq, dtype=jnp.float32)
    inv_freq = ROPE_THETA ** (-jnp.arange(0, HEAD_DIM, 2, dtype=jnp.float32) / HEAD_DIM)
    ang = pos[:, None] * inv_freq[None, :]
    cos = jnp.concatenate([jnp.cos(ang)] * (2 * PAIR), axis=-1)
    sin = jnp.concatenate([-jnp.sin(ang), jnp.sin(ang)] * PAIR, axis=-1)
    return cos, sin


def _tile_gain(g, width):
    return jnp.tile(g, width // HEAD_DIM)


def kernel(x, norm_even, w_in_even, w_out_even, qnorm_a, knorm_a, qnorm_b, knorm_b,
           norm_odd, w_in_odd, w_out_odd, qnorm_c, knorm_c, sinks_c):
    batch, seq, d = x.shape
    bf16 = jnp.bfloat16
    scale = HEAD_DIM ** -0.5 * LOG2E
    cos, sin = _rope_tables(seq)
    x2 = x.reshape(batch * seq, d)

    ones = jnp.ones((A_WIDTH,), jnp.float32)
    gain0 = jnp.concatenate([
        _tile_gain(qnorm_a[0], A_WIDTH) * scale, _tile_gain(knorm_a[0], A_WIDTH), ones, ones,
        _tile_gain(qnorm_b[0], B_WIDTH) * scale, _tile_gain(knorm_b[0], B_WIDTH), ones, ones])[None, :]
    tiles_per_group = A_WIDTH // LANES
    rope0 = frozenset(t for grp in (0, 1, 4, 5) for t in range(grp * tiles_per_group, (grp + 1) * tiles_per_group))
    proj0 = _inproj(x2, norm_even[0][None, :], w_in_even[0].astype(bf16), gain0, cos, sin, rope0, seq)
    oa = _attn_a(proj0, batch, seq)
    ob = _attn_b(proj0, batch, seq)
    x1 = _outproj(x2, [(oa, 0), (ob, 0)], [(proj0, 3), (proj0, 7)], w_out_even[0].astype(bf16), A_WIDTH)

    w1 = w_in_odd[0]
    kv_w = C_KV_HEADS * HEAD_DIM
    wq, wk, wv, wz = (w1[:, :C_WIDTH], w1[:, C_WIDTH:C_WIDTH + kv_w],
                      w1[:, C_WIDTH + kv_w:C_WIDTH + 2 * kv_w], w1[:, C_WIDTH + 2 * kv_w:])

    def dup_heads(w):
        return jnp.concatenate([w[:, h * HEAD_DIM:(h + 1) * HEAD_DIM]
                                for h in range(C_KV_HEADS) for _ in range(PAIR)], axis=1)

    w1r = jnp.concatenate([wq, dup_heads(wk), dup_heads(wv), wz], axis=1).astype(bf16)
    dup_w = PAIR * kv_w
    gain1 = jnp.concatenate([
        _tile_gain(qnorm_c[0], C_WIDTH) * scale, _tile_gain(knorm_c[0], dup_w),
        jnp.ones((dup_w + C_WIDTH,), jnp.float32)])[None, :]
    rope1 = frozenset(range((C_WIDTH + dup_w) // LANES))
    proj1 = _inproj(x1, norm_odd[0][None, :], w1r, gain1, cos, sin, rope1, seq)
    oc = _attn_c(proj1, sinks_c[0], batch, seq)
    half = C_WIDTH // 2
    z_blk = (C_WIDTH + 2 * dup_w) // half
    out = _outproj(x1, [(oc, 0), (oc, 1)], [(proj1, z_blk), (proj1, z_blk + 1)],
                   w_out_odd[0].astype(bf16), half)
    return out.reshape(batch, seq, d)
```

```python
import functools

import jax
import jax.numpy as jnp
from jax import lax
from jax.experimental import pallas as pl
from jax.experimental.pallas import tpu as pltpu

D_MODEL = 1024
HEAD_DIM = 64
LANES = 128
PAIR = LANES // HEAD_DIM
A_WIDTH = 512
B_WIDTH = 512
C_WIDTH = 1024
C_KV_HEADS = 2
C_GROUP = 8
DILATIONS = (1, 4, 16)
BAND = 128
MOBA_BLOCK = 256
MOBA_TOPK = 3
MOBA_CHUNK_BLOCKS = 2
BAND_UNROLL = 8
PREP_UNROLL = 4
SUM_ROWS = 16
LOG2E = 1.4426950408889634
ROPE_THETA = 10000.0
NORM_EPS = 1e-6
NEG = -1e30
VMEM_LIMIT = 56 * 1024 * 1024
PROJ_ROWS = 512
PROJ_COLS = 512

_NT = (((1,), (1,)), ((), ()))


def _lane_ids(shape):
    return lax.broadcasted_iota(jnp.int32, shape, len(shape) - 1)


def _split_heads_rows(t):
    lane = _lane_ids(t.shape)
    zero = jnp.zeros_like(t)
    return jnp.concatenate([jnp.where(lane < HEAD_DIM, t, zero),
                            jnp.where(lane >= HEAD_DIM, t, zero)], axis=0)


def _inproj_kernel(x_ref, g_ref, w_ref, gain_ref, cos_ref, sin_ref, o_ref, *, rope_tiles):
    x = x_ref[...]
    ms = jnp.mean(x * x, axis=-1, keepdims=True)
    h = (x * lax.rsqrt(ms + NORM_EPS) * g_ref[...]).astype(jnp.bfloat16)
    n_cols = o_ref.shape[1]
    lane = _lane_ids((x.shape[0], LANES))
    head0 = lane < HEAD_DIM
    first_half = (lane % HEAD_DIM) < (HEAD_DIM // 2)
    for c0 in range(0, n_cols, PROJ_COLS):
        r = jnp.dot(h, w_ref[:, c0:c0 + PROJ_COLS], preferred_element_type=jnp.float32)
        for t0 in range(0, PROJ_COLS, LANES):
            col = c0 + t0
            t = r[:, t0:t0 + LANES]
            if col // LANES in rope_tiles:
                sq = t * t
                ss0 = jnp.sum(jnp.where(head0, sq, 0.0), axis=-1, keepdims=True)
                ss1 = jnp.sum(jnp.where(head0, 0.0, sq), axis=-1, keepdims=True)
                inv = jnp.where(head0, lax.rsqrt(ss0 / HEAD_DIM + NORM_EPS),
                                lax.rsqrt(ss1 / HEAD_DIM + NORM_EPS))
                t = t * inv * gain_ref[:, col:col + LANES]
                rot = jnp.where(first_half, pltpu.roll(t, LANES - HEAD_DIM // 2, 1),
                                pltpu.roll(t, HEAD_DIM // 2, 1))
                t = t * cos_ref[...] + rot * sin_ref[...]
            o_ref[:, col:col + LANES] = t.astype(o_ref.dtype)


def _inproj(x2, g, w, gain, cos, sin, rope_tiles, seq):
    rows, d = x2.shape
    n_cols = w.shape[1]
    seq_tiles = seq // PROJ_ROWS
    return pl.pallas_call(
        functools.partial(_inproj_kernel, rope_tiles=rope_tiles),
        grid=(rows // PROJ_ROWS,),
        in_specs=[
            pl.BlockSpec((PROJ_ROWS, d), lambda i: (i, 0)),
            pl.BlockSpec((1, d), lambda i: (0, 0)),
            pl.BlockSpec((d, n_cols), lambda i: (0, 0)),
            pl.BlockSpec((1, n_cols), lambda i: (0, 0)),
            pl.BlockSpec((PROJ_ROWS, LANES), lambda i: (i % seq_tiles, 0)),
            pl.BlockSpec((PROJ_ROWS, LANES), lambda i: (i % seq_tiles, 0)),
        ],
        out_specs=pl.BlockSpec((PROJ_ROWS, n_cols), lambda i: (i, 0)),
        out_shape=jax.ShapeDtypeStruct((rows, n_cols), jnp.bfloat16),
        compiler_params=pltpu.CompilerParams(
            dimension_semantics=("parallel",), vmem_limit_bytes=VMEM_LIMIT),
        name="inproj",
    )(x2, g, w, gain, cos, sin)


def _outproj_kernel(*refs, n_parts):
    x_ref = refs[0]
    y_refs = refs[1:1 + n_parts]
    z_refs = refs[1 + n_parts:1 + 2 * n_parts]
    w_ref = refs[1 + 2 * n_parts]
    o_ref = refs[2 + 2 * n_parts]
    acc = x_ref[...]
    k0 = 0
    for y_ref, z_ref in zip(y_refs, z_refs):
        z = z_ref[...].astype(jnp.float32)
        gated = y_ref[...].astype(jnp.float32) * (z / (1.0 + jnp.exp(-z)))
        kw = y_ref.shape[1]
        acc = acc + jnp.dot(gated.astype(jnp.bfloat16), w_ref[k0:k0 + kw, :],
                            preferred_element_type=jnp.float32)
        k0 += kw
    o_ref[...] = acc


def _outproj(x2, ys, zs, w, part_width):
    rows, d = x2.shape
    n_parts = len(ys)
    in_specs = [pl.BlockSpec((PROJ_ROWS, d), lambda i: (i, 0))]
    for _, blk in list(ys) + list(zs):
        in_specs.append(pl.BlockSpec((PROJ_ROWS, part_width), lambda i, blk=blk: (i, blk)))
    in_specs.append(pl.BlockSpec(w.shape, lambda i: (0, 0)))
    return pl.pallas_call(
        functools.partial(_outproj_kernel, n_parts=n_parts),
        grid=(rows // PROJ_ROWS,),
        in_specs=in_specs,
        out_specs=pl.BlockSpec((PROJ_ROWS, d), lambda i: (i, 0)),
        out_shape=jax.ShapeDtypeStruct((rows, d), jnp.float32),
        compiler_params=pltpu.CompilerParams(
            dimension_semantics=("parallel",), vmem_limit_bytes=VMEM_LIMIT),
        name="outproj",
    )(x2, *[y for y, _ in ys], *[z for z, _ in zs], w)


def _fill_band_bias(bias_ref, first_key):
    kj = lax.broadcasted_iota(jnp.int32, (2 * BAND, PAIR * BAND), 0)
    qi = lax.broadcasted_iota(jnp.int32, (2 * BAND, PAIR * BAND), 1) % BAND
    band = (kj >= qi + first_key) & (kj <= qi + BAND)
    bias_ref[0] = jnp.where(band, 0.0, NEG)
    bias_ref[1] = jnp.where(band & (kj >= BAND), 0.0, NEG)


def _band_blocks(operands, sink_row):
    scores = [lax.dot_general(kw, _split_heads_rows(q), _NT, preferred_element_type=jnp.float32) + bias
              for q, kw, _, bias in operands]
    probs = []
    for s in scores:
        m = jnp.max(s, axis=0, keepdims=True)
        if sink_row is not None:
            m = jnp.maximum(m, sink_row)
        probs.append((jnp.exp2(s - m).astype(jnp.bfloat16), m))
    return [(jnp.dot(vtw, p, preferred_element_type=jnp.float32), m)
            for (_, _, vtw, _), (p, m) in zip(operands, probs)]


def _pair_tile(x):
    return jnp.concatenate([x[0:HEAD_DIM, 0:BAND], x[HEAD_DIM:LANES, BAND:PAIR * BAND]], axis=0)


def _pair_stat_tile(v):
    return jnp.concatenate([jnp.broadcast_to(v[:, 0:BAND], (HEAD_DIM, BAND)),
                            jnp.broadcast_to(v[:, BAND:PAIR * BAND], (HEAD_DIM, BAND))], axis=0)


def _attn_a_kernel(q_ref, k_ref, v_ref, o_ref, qf, kf, vf, qd, kd, vdt, bias_s, m_s, l_s, acc_s):
    seq = q_ref.shape[0]
    n_blocks = seq // BAND
    piece = 2 * BAND
    qf[...] = q_ref[...].astype(jnp.float32)
    kf[...] = k_ref[...].astype(jnp.float32)
    vf[...] = v_ref[...].astype(jnp.float32)
    kd[0:BAND, :] = jnp.zeros((BAND, LANES), kd.dtype)
    vdt[:, 0:BAND] = jnp.zeros((LANES + SUM_ROWS, BAND), vdt.dtype)
    vdt[LANES:, BAND:] = jnp.ones((SUM_ROWS, seq), vdt.dtype)
    _fill_band_bias(bias_s, 0)

    for dil in DILATIONS:
        sub = seq // dil
        blocks_per_sub = sub // BAND
        pieces_per_sub = sub // piece

        def deinterleave(i, carry, dil=dil, pieces_per_sub=pieces_per_sub):
            dst = pl.multiple_of(i * piece, piece)
            if dil == 1:
                src = pl.ds(dst, piece)
                qd[pl.ds(dst, piece), :] = q_ref[src, :]
                kd[pl.ds(BAND + dst, piece), :] = k_ref[src, :]
            else:
                r = i // pieces_per_sub
                j = i % pieces_per_sub
                src = pl.ds(r + j * (piece * dil), piece, stride=dil)
                qd[pl.ds(dst, piece), :] = qf[src, :].astype(qd.dtype)
                kd[pl.ds(BAND + dst, piece), :] = kf[src, :].astype(kd.dtype)
            vdt[0:LANES, pl.ds(BAND + dst, piece)] = vf[src, :].T.astype(vdt.dtype)
            return carry

        lax.fori_loop(0, seq // piece, deinterleave, 0, unroll=PREP_UNROLL)

        def body(it, carry, dil=dil, blocks_per_sub=blocks_per_sub):
            blocks = [it * BAND_UNROLL + u for u in range(BAND_UNROLL)]
            operands = []
            for n in blocks:
                row0 = pl.multiple_of(n * BAND, BAND)
                first = (n % blocks_per_sub) == 0
                operands.append((qd[pl.ds(row0, BAND), :], kd[pl.ds(row0, 2 * BAND), :],
                                 vdt[:, pl.ds(row0, 2 * BAND)], bias_s[jnp.where(first, 1, 0)]))
            for n, (o_t, m) in zip(blocks, _band_blocks(operands, None)):
                merge_block(n, o_t, m, dil, blocks_per_sub)
            return carry

        def merge_block(n, o_t, m, dil, blocks_per_sub):
            r = n // blocks_per_sub
            c = n % blocks_per_sub
            row0 = pl.multiple_of(n * BAND, BAND)
            acc = _pair_tile(o_t).T
            m_b = _pair_stat_tile(m).T
            l_b = _pair_stat_tile(o_t[LANES:LANES + 1, :]).T
            if dil == 1:
                rows = pl.ds(row0, BAND)
                m_s[rows, :] = m_b
                l_s[rows, :] = l_b
                acc_s[rows, :] = acc
            else:
                rows = pl.ds(c * (BAND * dil) + r, BAND, stride=dil)
                m_old = m_s[rows, :]
                m_new = jnp.maximum(m_old, m_b)
                a_old = jnp.exp2(m_old - m_new)
                a_new = jnp.exp2(m_b - m_new)
                m_s[rows, :] = m_new
                l_s[rows, :] = a_old * l_s[rows, :] + a_new * l_b
                acc_s[rows, :] = a_old * acc_s[rows, :] + a_new * acc

        lax.fori_loop(0, n_blocks // BAND_UNROLL, body, 0)

    o_ref[...] = (acc_s[...] / l_s[...]).astype(o_ref.dtype)


def _attn_a(proj, batch, seq):
    n_pairs = A_WIDTH // LANES
    blocks_per_batch = 1

    def col_spec(group):
        return pl.BlockSpec((seq, LANES), lambda b, p, group=group: (b, group * n_pairs + p))

    f32 = jnp.float32
    bf16 = jnp.bfloat16
    del blocks_per_batch
    return pl.pallas_call(
        _attn_a_kernel,
        grid=(batch, n_pairs),
        in_specs=[col_spec(0), col_spec(1), col_spec(2)],
        out_specs=pl.BlockSpec((seq, LANES), lambda b, p: (b, p)),
        out_shape=jax.ShapeDtypeStruct((batch * seq, A_WIDTH), bf16),
        scratch_shapes=[
            pltpu.VMEM((seq, LANES), f32), pltpu.VMEM((seq, LANES), f32), pltpu.VMEM((seq, LANES), f32),
            pltpu.VMEM((seq, LANES), bf16),
            pltpu.VMEM((seq + BAND, LANES), bf16),
            pltpu.VMEM((LANES + SUM_ROWS, seq + BAND), bf16),
            pltpu.VMEM((2, 2 * BAND, PAIR * BAND), f32),
            pltpu.VMEM((seq, LANES), f32), pltpu.VMEM((seq, LANES), f32), pltpu.VMEM((seq, LANES), f32),
        ],
        compiler_params=pltpu.CompilerParams(
            dimension_semantics=("parallel", "parallel"), vmem_limit_bytes=VMEM_LIMIT),
        name="attn_dilated",
    )(proj, proj, proj)


def _attn_c_kernel(sinks_ref, q_ref, k_ref, v_ref, o_ref, kd, vdt, bias_s):
    seq = q_ref.shape[0]
    pair = pl.program_id(1)
    piece = 2 * BAND
    kd[0:BAND, :] = jnp.zeros((BAND, LANES), kd.dtype)
    kd[BAND:, :] = k_ref[...]
    vdt[:, 0:BAND] = jnp.zeros((LANES + SUM_ROWS, BAND), vdt.dtype)
    vdt[LANES:, BAND:] = jnp.ones((SUM_ROWS, seq), vdt.dtype)

    def transpose_v(i, carry):
        rows = pl.ds(pl.multiple_of(i * piece, piece), piece)
        vdt[0:LANES, pl.ds(pl.multiple_of(BAND + i * piece, BAND), piece)] = (
            v_ref[rows, :].astype(jnp.float32).T.astype(vdt.dtype))
        return carry

    lax.fori_loop(0, seq // piece, transpose_v, 0, unroll=PREP_UNROLL)
    _fill_band_bias(bias_s, 1)
    col = lax.broadcasted_iota(jnp.int32, (1, PAIR * BAND), 1)
    sink_row = jnp.where(col < BAND, sinks_ref[PAIR * pair], sinks_ref[PAIR * pair + 1]) * LOG2E

    def body(it, carry):
        blocks = [it * BAND_UNROLL + u for u in range(BAND_UNROLL)]
        operands = []
        for n in blocks:
            row0 = pl.multiple_of(n * BAND, BAND)
            operands.append((q_ref[pl.ds(row0, BAND), :], kd[pl.ds(row0, 2 * BAND), :],
                             vdt[:, pl.ds(row0, 2 * BAND)], bias_s[jnp.where(n == 0, 1, 0)]))
        for n, (o_t, m) in zip(blocks, _band_blocks(operands, sink_row)):
            l = o_t[LANES:LANES + 1, :] + jnp.exp2(sink_row - m)
            out_t = _pair_tile(o_t) * _pair_stat_tile(1.0 / l)
            o_ref[pl.ds(pl.multiple_of(n * BAND, BAND), BAND), :] = out_t.T.astype(o_ref.dtype)
        return carry

    lax.fori_loop(0, seq // (BAND * BAND_UNROLL), body, 0)


def _attn_c(proj, sinks, batch, seq):
    n_pairs = C_WIDTH // LANES
    pairs_per_kv = C_GROUP // PAIR
    k_tile0 = C_WIDTH // LANES
    v_tile0 = k_tile0 + C_KV_HEADS
    bf16 = jnp.bfloat16
    return pl.pallas_call(
        _attn_c_kernel,
        grid=(batch, n_pairs),
        in_specs=[
            pl.BlockSpec(memory_space=pltpu.SMEM),
            pl.BlockSpec((seq, LANES), lambda b, p: (b, p)),
            pl.BlockSpec((seq, LANES), lambda b, p: (b, k_tile0 + p // pairs_per_kv)),
            pl.BlockSpec((seq, LANES), lambda b, p: (b, v_tile0 + p // pairs_per_kv)),
        ],
        out_specs=pl.BlockSpec((seq, LANES), lambda b, p: (b, p)),
        out_shape=jax.ShapeDtypeStruct((batch * seq, C_WIDTH), bf16),
        scratch_shapes=[
            pltpu.VMEM((seq + BAND, LANES), bf16),
            pltpu.VMEM((LANES + SUM_ROWS, seq + BAND), bf16),
            pltpu.VMEM((2, 2 * BAND, PAIR * BAND), jnp.float32),
        ],
        compiler_params=pltpu.CompilerParams(
            dimension_semantics=("parallel", "parallel"), vmem_limit_bytes=VMEM_LIMIT),
        name="attn_swa",
    )(sinks, proj, proj, proj)


_T_TILE, _T_CH, _T_FIRST, _T_LAST, _T_ROWS = range(5)


def _moba_step_table(n_tiles):
    tiles, chs, firsts, lasts = [], [], [], []
    for tile in range(n_tiles):
        order = [tile] + list(range(tile))
        for i, ch in enumerate(order):
            tiles.append(tile)
            chs.append(ch)
            firsts.append(int(i == 0))
            lasts.append(int(i == len(order) - 1))
    n_steps = len(tiles)
    rows = [None] * _T_ROWS
    rows[_T_TILE] = tiles + [tiles[-1]]
    rows[_T_CH] = chs + [chs[-1]]
    rows[_T_FIRST] = firsts + [0]
    rows[_T_LAST] = lasts + [0]
    return n_steps, [v for row in rows for v in row]


def _moba_kernel(tbl_ref, q_ref, k_ref, v_ref, o_ref, vt, q2_all, kmean, bias_s, causal_s, m_s, acc_s,
                 s_a, s_b, *, n_steps):
    seq = q_ref.shape[0]
    nb = seq // MOBA_BLOCK
    nq2 = PAIR * MOBA_BLOCK
    nq = MOBA_CHUNK_BLOCKS * nq2
    n_tiles = nb // MOBA_CHUNK_BLOCKS
    chunk = MOBA_CHUNK_BLOCKS * MOBA_BLOCK

    def tbl(row, t):
        return tbl_ref[row * (n_steps + 1) + t]

    def prep(j, carry):
        rows = pl.ds(pl.multiple_of(j * MOBA_BLOCK, MOBA_BLOCK), MOBA_BLOCK)
        vt[0:LANES, rows] = v_ref[rows, :].astype(jnp.float32).T.astype(jnp.bfloat16)
        kmean[pl.ds(j, 1), :] = jnp.mean(k_ref[rows, :].astype(jnp.float32), axis=0, keepdims=True)
        q2_all[pl.ds(pl.multiple_of(j * nq2, nq2), nq2), :] = _split_heads_rows(q_ref[rows, :])
        return carry

    lax.fori_loop(0, nb, prep, 0, unroll=PREP_UNROLL)
    vt[LANES:, :] = jnp.ones((SUM_ROWS, seq), vt.dtype)
    km = kmean[...]
    km_hi = km.astype(jnp.bfloat16)
    km_lo = (km - km_hi.astype(jnp.float32)).astype(jnp.bfloat16)
    blk_id = lax.broadcasted_iota(jnp.int32, (nb, nq2), 0)
    key_id = lax.broadcasted_iota(jnp.int32, (chunk, nq), 0)
    col_id = lax.broadcasted_iota(jnp.int32, (chunk, nq), 1)
    causal_s[...] = jnp.where(key_id <= col_id % MOBA_BLOCK + (col_id // nq2) * MOBA_BLOCK, 0.0, NEG)

    def chunk_rows(ch):
        return pl.ds(pl.multiple_of(ch * chunk, chunk), chunk)

    def scores_t(t):
        q2 = q2_all[pl.ds(pl.multiple_of(tbl(_T_TILE, t) * nq, nq), nq), :]
        return lax.dot_general(k_ref[chunk_rows(tbl(_T_CH, t)), :], q2, _NT,
                               preferred_element_type=jnp.float32)

    def pv_t(t, p):
        return jnp.dot(vt[:, chunk_rows(tbl(_T_CH, t))], p.astype(vt.dtype), preferred_element_type=jnp.float32)

    def block_bias(t):
        row0 = tbl(_T_TILE, t) * nb + tbl(_T_CH, t) * MOBA_CHUNK_BLOCKS
        return [bias_s[pl.ds(row0 + i, 1), :] for i in range(MOBA_CHUNK_BLOCKS)]

    def per_block(x, fn):
        return [fn(x[i * MOBA_BLOCK:(i + 1) * MOBA_BLOCK], i) for i in range(MOBA_CHUNK_BLOCKS)]

    def select_blocks(tile, carry):
        for i in range(MOBA_CHUNK_BLOCKS):
            qb = tile * MOBA_CHUNK_BLOCKS + i
            q2 = q2_all[pl.ds(pl.multiple_of(qb * nq2, nq2), nq2), :]
            gate = (lax.dot_general(km_hi, q2, _NT, preferred_element_type=jnp.float32)
                    + lax.dot_general(km_lo, q2, _NT, preferred_element_type=jnp.float32))
            past = blk_id < qb
            gate = jnp.where(past, gate, NEG)
            rank = jnp.zeros((nb, nq2), jnp.int32)
            for j in range(nb - 1):
                gj = gate[j:j + 1, :]
                beats = (gj > gate) | ((gj == gate) & (j < blk_id))
                rank = rank + jnp.where(beats & (j < qb), 1, 0)
            allowed = (past & (rank < MOBA_TOPK)) | (blk_id == qb)
            bias_s[pl.ds(pl.multiple_of(tile * nb, nb), nb), i * nq2:(i + 1) * nq2] = jnp.where(allowed, 0.0, NEG)
        return carry

    lax.fori_loop(0, n_tiles, select_blocks, 0, unroll=2)

    def attend_first(t, s_cur):
        bias = block_bias(t)
        s = s_cur[...] + causal_s[...]
        s = jnp.concatenate(per_block(s, lambda x, i: x + bias[i]), axis=0)
        m0 = jnp.max(s, axis=0, keepdims=True)
        p = jnp.exp2(s - m0)
        m_s[...] = m0
        acc_s[...] = pv_t(t, p)

    def attend_next(t, s_cur):
        s = s_cur[...]
        bias = block_bias(t)
        m_old = m_s[...]
        m_blk = per_block(s, lambda x, i: jnp.max(x, axis=0, keepdims=True) + bias[i])
        m_new = functools.reduce(jnp.maximum, m_blk, m_old)
        alpha = jnp.exp2(m_old - m_new)
        p = jnp.concatenate(per_block(s, lambda x, i: jnp.exp2(x - (m_new - bias[i]))), axis=0)
        m_s[...] = m_new
        acc_s[...] = alpha * acc_s[...] + pv_t(t, p)

    def finalize(tile):
        inv_l = 1.0 / acc_s[LANES:LANES + 1, :]
        for i in range(MOBA_CHUNK_BLOCKS):
            c0 = i * nq2
            o_t = jnp.concatenate(
                [acc_s[0:HEAD_DIM, c0:c0 + MOBA_BLOCK] * inv_l[:, c0:c0 + MOBA_BLOCK],
                 acc_s[HEAD_DIM:LANES, c0 + MOBA_BLOCK:c0 + nq2] * inv_l[:, c0 + MOBA_BLOCK:c0 + nq2]],
                axis=0)
            rows = pl.ds(pl.multiple_of((tile * MOBA_CHUNK_BLOCKS + i) * MOBA_BLOCK, MOBA_BLOCK), MOBA_BLOCK)
            o_ref[rows, :] = o_t.T.astype(o_ref.dtype)

    def step(t, s_cur, s_nxt):
        def first():
            s_nxt[...] = scores_t(t + 1)
            attend_first(t, s_cur)

        def later():
            s_nxt[...] = scores_t(t + 1)
            attend_next(t, s_cur)

        lax.cond(tbl(_T_FIRST, t) == 1, first, later)

        @pl.when(tbl(_T_LAST, t) == 1)
        def _():
            finalize(tbl(_T_TILE, t))

    s_a[...] = scores_t(0)

    def two_steps(i, carry):
        step(2 * i, s_a, s_b)
        step(2 * i + 1, s_b, s_a)
        return carry

    lax.fori_loop(0, n_steps // 2, two_steps, 0)


def _attn_b(proj, batch, seq):
    n_pairs = B_WIDTH // LANES
    tile0 = 4 * (A_WIDTH // LANES)

    def col_spec(group):
        return pl.BlockSpec((seq, LANES), lambda b, p, group=group: (b, tile0 + group * n_pairs + p))

    nb = seq // MOBA_BLOCK
    nq2 = PAIR * MOBA_BLOCK
    chunk = MOBA_CHUNK_BLOCKS * MOBA_BLOCK
    assert nb % MOBA_CHUNK_BLOCKS == 0
    n_tiles = nb // MOBA_CHUNK_BLOCKS
    nq = MOBA_CHUNK_BLOCKS * nq2
    n_steps, table = _moba_step_table(n_tiles)
    assert n_steps % 2 == 0
    f32 = jnp.float32
    bf16 = jnp.bfloat16
    return pl.pallas_call(
        functools.partial(_moba_kernel, n_steps=n_steps),
        grid=(batch, n_pairs),
        in_specs=[pl.BlockSpec(memory_space=pltpu.SMEM), col_spec(0), col_spec(1), col_spec(2)],
        out_specs=pl.BlockSpec((seq, LANES), lambda b, p: (b, p)),
        out_shape=jax.ShapeDtypeStruct((batch * seq, B_WIDTH), bf16),
        scratch_shapes=[
            pltpu.VMEM((LANES + SUM_ROWS, seq), bf16),
            pltpu.VMEM((nb * nq2, LANES), bf16),
            pltpu.VMEM((nb, LANES), f32),
            pltpu.VMEM((n_tiles * nb, nq), f32),
            pltpu.VMEM((chunk, nq), f32),
            pltpu.VMEM((1, nq), f32),
            pltpu.VMEM((LANES + SUM_ROWS, nq), f32),
            pltpu.VMEM((chunk, nq), f32), pltpu.VMEM((chunk, nq), f32),
        ],
        compiler_params=pltpu.CompilerParams(
            dimension_semantics=("parallel", "parallel"), vmem_limit_bytes=VMEM_LIMIT),
        name="attn_moba",
    )(jnp.asarray(table, jnp.int32), proj, proj, proj)


def _rope_tables(seq):
    pos = jnp.arange(seq, dtype=jnp.float32)
    inv_freq = ROPE_THETA ** (-jnp.arange(0, HEAD_DIM, 2, dtype=jnp.float32) / HEAD_DIM)
    ang = pos[:, None] * inv_freq[None, :]
    cos = jnp.concatenate([jnp.cos(ang)] * (2 * PAIR), axis=-1)
    sin = jnp.concatenate([-jnp.sin(ang), jnp.sin(ang)] * PAIR, axis=-1)
    return cos, sin


def _tile_gain(g, width):
    return jnp.tile(g, width // HEAD_DIM)


def kernel(x, norm_even, w_in_even, w_out_even, qnorm_a, knorm_a, qnorm_b, knorm_b,
           norm_odd, w_in_odd, w_out_odd, qnorm_c, knorm_c, sinks_c):
    batch, seq, d = x.shape
    bf16 = jnp.bfloat16
    scale = HEAD_DIM ** -0.5 * LOG2E
    cos, sin = _rope_tables(seq)
    x2 = x.reshape(batch * seq, d)

    ones = jnp.ones((A_WIDTH,), jnp.float32)
    gain0 = jnp.concatenate([
        _tile_gain(qnorm_a[0], A_WIDTH) * scale, _tile_gain(knorm_a[0], A_WIDTH), ones, ones,
        _tile_gain(qnorm_b[0], B_WIDTH) * scale, _tile_gain(knorm_b[0], B_WIDTH), ones, ones])[None, :]
    tiles_per_group = A_WIDTH // LANES
    rope0 = frozenset(t for grp in (0, 1, 4, 5) for t in range(grp * tiles_per_group, (grp + 1) * tiles_per_group))
    proj0 = _inproj(x2, norm_even[0][None, :], w_in_even[0].astype(bf16), gain0, cos, sin, rope0, seq)
    oa = _attn_a(proj0, batch, seq)
    ob = _attn_b(proj0, batch, seq)
    x1 = _outproj(x2, [(oa, 0), (ob, 0)], [(proj0, 3), (proj0, 7)], w_out_even[0].astype(bf16), A_WIDTH)

    w1 = w_in_odd[0]
    kv_w = C_KV_HEADS * HEAD_DIM
    wq, wk, wv, wz = (w1[:, :C_WIDTH], w1[:, C_WIDTH:C_WIDTH + kv_w],
                      w1[:, C_WIDTH + kv_w:C_WIDTH + 2 * kv_w], w1[:, C_WIDTH + 2 * kv_w:])

    def dup_heads(w):
        return jnp.concatenate([w[:, h * HEAD_DIM:(h + 1) * HEAD_DIM]
                                for h in range(C_KV_HEADS) for _ in range(PAIR)], axis=1)

    w1r = jnp.concatenate([wq, dup_heads(wk), dup_heads(wv), wz], axis=1).astype(bf16)
    dup_w = PAIR * kv_w
    gain1 = jnp.concatenate([
        _tile_gain(qnorm_c[0], C_WIDTH) * scale, _tile_gain(knorm_c[0], dup_w),
        jnp.ones((dup_w + C_WIDTH,), jnp.float32)])[None, :]
    rope1 = frozenset(range((C_WIDTH + dup_w) // LANES))
    proj1 = _inproj(x1, norm_odd[0][None, :], w1r, gain1, cos, sin, rope1, seq)
    oc = _attn_c(proj1, sinks_c[0], batch, seq)
    half = C_WIDTH // 2
    z_blk = (C_WIDTH + 2 * dup_w) // half
    out = _outproj(x1, [(oc, 0), (oc, 1)], [(proj1, z_blk), (proj1, z_blk + 1)],
                   w_out_odd[0].astype(bf16), half)
    return out.reshape(batch, seq, d)
```

```python
import functools

import jax
import jax.numpy as jnp
from jax import lax
from jax.experimental import pallas as pl
from jax.experimental.pallas import tpu as pltpu

D_MODEL = 1024
HEAD_DIM = 64
LANES = 128
PAIR = LANES // HEAD_DIM
HALF_DIM = HEAD_DIM // 2
NORM_COLS = 2 * LANES
A_WIDTH = 512
B_WIDTH = 512
C_WIDTH = 1024
C_KV_HEADS = 2
C_GROUP = 8
DILATIONS = (1, 4, 16)
BAND = 128
MOBA_BLOCK = 256
MOBA_TOPK = 3
MOBA_CHUNK_BLOCKS = 2
DILATED_UNROLL = 8
SWA_UNROLL = 16
PREP_UNROLL = 4
SUM_ROWS = 16
LOG2E = 1.4426950408889634
ROPE_THETA = 10000.0
NORM_EPS = 1e-6
NEG = -1e30
VMEM_LIMIT = 56 * 1024 * 1024
PROJ_ROWS = 512
PROJ_COLS = 512

_NT = (((1,), (1,)), ((), ()))


def _lane_ids(shape):
    return lax.broadcasted_iota(jnp.int32, shape, len(shape) - 1)


def _qk_head_of_lane(lane):
    return (lane // HALF_DIM) % PAIR


def _qk_lane_order():
    return [h * HEAD_DIM + half * HALF_DIM + d for half in range(2) for h in range(PAIR) for d in range(HALF_DIM)]


def _split_heads_rows(t):
    head = _qk_head_of_lane(_lane_ids(t.shape))
    zero = jnp.zeros_like(t)
    return jnp.concatenate([jnp.where(head == 0, t, zero), jnp.where(head == 1, t, zero)], axis=0)


def _inproj_kernel(x_ref, g_ref, w_ref, gain_ref, cos_ref, sin_ref, o_ref, *, rope_tiles):
    x = x_ref[...]
    ms = jnp.mean(x * x, axis=-1, keepdims=True)
    h = (x * lax.rsqrt(ms + NORM_EPS) * g_ref[...]).astype(jnp.bfloat16)
    n_cols = o_ref.shape[1]
    li = lax.broadcasted_iota(jnp.int32, (NORM_COLS, NORM_COLS), 0)
    lj = lax.broadcasted_iota(jnp.int32, (NORM_COLS, NORM_COLS), 1)
    same_head = ((li // LANES == lj // LANES)
                 & (_qk_head_of_lane(li) == _qk_head_of_lane(lj))).astype(jnp.bfloat16)
    for c0 in range(0, n_cols, PROJ_COLS):
        r = jnp.dot(h, w_ref[:, c0:c0 + PROJ_COLS], preferred_element_type=jnp.float32)
        for p0 in range(0, PROJ_COLS, NORM_COLS):
            col = c0 + p0
            tiles = range(col // LANES, (col + NORM_COLS) // LANES)
            t = r[:, p0:p0 + NORM_COLS]
            if all(tile in rope_tiles for tile in tiles):
                ss = jnp.dot((t * t).astype(jnp.bfloat16), same_head, preferred_element_type=jnp.float32)
                t = t * lax.rsqrt(ss * (1.0 / HEAD_DIM) + NORM_EPS) * gain_ref[:, col:col + NORM_COLS]
                for t0 in range(0, NORM_COLS, LANES):
                    tn = t[:, t0:t0 + LANES]
                    tn = tn * cos_ref[...] + pltpu.roll(tn, LANES // 2, 1) * sin_ref[...]
                    o_ref[:, col + t0:col + t0 + LANES] = tn.astype(o_ref.dtype)
            else:
                assert not any(tile in rope_tiles for tile in tiles)
                o_ref[:, col:col + NORM_COLS] = t.astype(o_ref.dtype)


def _inproj(x2, g, w, gain, cos, sin, rope_tiles, seq):
    rows, d = x2.shape
    n_cols = w.shape[1]
    seq_tiles = seq // PROJ_ROWS
    return pl.pallas_call(
        functools.partial(_inproj_kernel, rope_tiles=rope_tiles),
        grid=(rows // PROJ_ROWS,),
        in_specs=[
            pl.BlockSpec((PROJ_ROWS, d), lambda i: (i, 0)),
            pl.BlockSpec((1, d), lambda i: (0, 0)),
            pl.BlockSpec((d, n_cols), lambda i: (0, 0)),
            pl.BlockSpec((1, n_cols), lambda i: (0, 0)),
            pl.BlockSpec((PROJ_ROWS, LANES), lambda i: (i % seq_tiles, 0)),
            pl.BlockSpec((PROJ_ROWS, LANES), lambda i: (i % seq_tiles, 0)),
        ],
        out_specs=pl.BlockSpec((PROJ_ROWS, n_cols), lambda i: (i, 0)),
        out_shape=jax.ShapeDtypeStruct((rows, n_cols), jnp.bfloat16),
        compiler_params=pltpu.CompilerParams(
            dimension_semantics=("parallel",), vmem_limit_bytes=VMEM_LIMIT),
        name="inproj",
    )(x2, g, w, gain, cos, sin)


def _outproj_kernel(*refs, n_parts):
    x_ref = refs[0]
    y_refs = refs[1:1 + n_parts]
    z_refs = refs[1 + n_parts:1 + 2 * n_parts]
    w_ref = refs[1 + 2 * n_parts]
    o_ref = refs[2 + 2 * n_parts]
    acc = x_ref[...]
    k0 = 0
    for y_ref, z_ref in zip(y_refs, z_refs):
        z = z_ref[...].astype(jnp.float32)
        gated = y_ref[...].astype(jnp.float32) * (z / (1.0 + jnp.exp(-z)))
        kw = y_ref.shape[1]
        acc = acc + jnp.dot(gated.astype(jnp.bfloat16), w_ref[k0:k0 + kw, :],
                            preferred_element_type=jnp.float32)
        k0 += kw
    o_ref[...] = acc


def _outproj(x2, ys, zs, w, part_width):
    rows, d = x2.shape
    n_parts = len(ys)
    in_specs = [pl.BlockSpec((PROJ_ROWS, d), lambda i: (i, 0))]
    for _, blk in list(ys) + list(zs):
        in_specs.append(pl.BlockSpec((PROJ_ROWS, part_width), lambda i, blk=blk: (i, blk)))
    in_specs.append(pl.BlockSpec(w.shape, lambda i: (0, 0)))
    return pl.pallas_call(
        functools.partial(_outproj_kernel, n_parts=n_parts),
        grid=(rows // PROJ_ROWS,),
        in_specs=in_specs,
        out_specs=pl.BlockSpec((PROJ_ROWS, d), lambda i: (i, 0)),
        out_shape=jax.ShapeDtypeStruct((rows, d), jnp.float32),
        compiler_params=pltpu.CompilerParams(
            dimension_semantics=("parallel",), vmem_limit_bytes=VMEM_LIMIT),
        name="outproj",
    )(x2, *[y for y, _ in ys], *[z for z, _ in zs], w)


def _fill_band_bias(bias_ref, first_key):
    kj = lax.broadcasted_iota(jnp.int32, (2 * BAND, PAIR * BAND), 0)
    qi = lax.broadcasted_iota(jnp.int32, (2 * BAND, PAIR * BAND), 1) % BAND
    band = (kj >= qi + first_key) & (kj <= qi + BAND)
    bias_ref[0] = jnp.where(band, 0.0, NEG)
    bias_ref[1] = jnp.where(band & (kj >= BAND), 0.0, NEG)


def _band_blocks(operands, sink_row):
    scores = [lax.dot_general(kw, _split_heads_rows(q), _NT, preferred_element_type=jnp.float32) + bias
              for q, kw, _, bias in operands]
    probs = []
    for s in scores:
        m = jnp.max(s, axis=0, keepdims=True)
        if sink_row is not None:
            m = jnp.maximum(m, sink_row)
        probs.append((jnp.exp2(s - m).astype(jnp.bfloat16), m))
    return [(jnp.dot(vtw, p, preferred_element_type=jnp.float32), m)
            for (_, _, vtw, _), (p, m) in zip(operands, probs)]


def _pair_tile(x):
    return jnp.concatenate([x[0:HEAD_DIM, 0:BAND], x[HEAD_DIM:LANES, BAND:PAIR * BAND]], axis=0)


def _pair_stat_tile(v):
    return jnp.concatenate([jnp.broadcast_to(v[:, 0:BAND], (HEAD_DIM, BAND)),
                            jnp.broadcast_to(v[:, BAND:PAIR * BAND], (HEAD_DIM, BAND))], axis=0)


def _attn_a_kernel(q_ref, k_ref, v_ref, o_ref, qf, kf, vf, qd, kd, vdt, bias_s, m_s, l_s, acc_s):
    seq = q_ref.shape[0]
    n_blocks = seq // BAND
    piece = 2 * BAND
    qf[...] = q_ref[...].astype(jnp.float32)
    kf[...] = k_ref[...].astype(jnp.float32)
    vf[...] = v_ref[...].astype(jnp.float32)
    kd[0:BAND, :] = jnp.zeros((BAND, LANES), kd.dtype)
    vdt[:, 0:BAND] = jnp.zeros((LANES + SUM_ROWS, BAND), vdt.dtype)
    vdt[LANES:, BAND:] = jnp.ones((SUM_ROWS, seq), vdt.dtype)
    _fill_band_bias(bias_s, 0)

    for dil in DILATIONS:
        sub = seq // dil
        blocks_per_sub = sub // BAND
        pieces_per_sub = sub // piece

        def deinterleave(i, carry, dil=dil, pieces_per_sub=pieces_per_sub):
            dst = pl.multiple_of(i * piece, piece)
            if dil == 1:
                src = pl.ds(dst, piece)
                qd[pl.ds(dst, piece), :] = q_ref[src, :]
                kd[pl.ds(BAND + dst, piece), :] = k_ref[src, :]
            else:
                r = i // pieces_per_sub
                j = i % pieces_per_sub
                src = pl.ds(r + j * (piece * dil), piece, stride=dil)
                qd[pl.ds(dst, piece), :] = qf[src, :].astype(qd.dtype)
                kd[pl.ds(BAND + dst, piece), :] = kf[src, :].astype(kd.dtype)
            vdt[0:LANES, pl.ds(BAND + dst, piece)] = vf[src, :].T.astype(vdt.dtype)
            return carry

        lax.fori_loop(0, seq // piece, deinterleave, 0, unroll=PREP_UNROLL)

        def body(it, carry, dil=dil, blocks_per_sub=blocks_per_sub):
            blocks = [it * DILATED_UNROLL + u for u in range(DILATED_UNROLL)]
            operands = []
            for n in blocks:
                row0 = pl.multiple_of(n * BAND, BAND)
                first = (n % blocks_per_sub) == 0
                operands.append((qd[pl.ds(row0, BAND), :], kd[pl.ds(row0, 2 * BAND), :],
                                 vdt[:, pl.ds(row0, 2 * BAND)], bias_s[jnp.where(first, 1, 0)]))
            for n, (o_t, m) in zip(blocks, _band_blocks(operands, None)):
                merge_block(n, o_t, m, dil, blocks_per_sub)
            return carry

        def merge_block(n, o_t, m, dil, blocks_per_sub):
            r = n // blocks_per_sub
            c = n % blocks_per_sub
            row0 = pl.multiple_of(n * BAND, BAND)
            acc = _pair_tile(o_t).T
            m_b = _pair_stat_tile(m).T
            l_b = _pair_stat_tile(o_t[LANES:LANES + 1, :]).T
            if dil == 1:
                rows = pl.ds(row0, BAND)
                m_s[rows, :] = m_b
                l_s[rows, :] = l_b
                acc_s[rows, :] = acc
            else:
                rows = pl.ds(c * (BAND * dil) + r, BAND, stride=dil)
                m_old = m_s[rows, :]
                m_new = jnp.maximum(m_old, m_b)
                a_old = jnp.exp2(m_old - m_new)
                a_new = jnp.exp2(m_b - m_new)
                m_s[rows, :] = m_new
                l_s[rows, :] = a_old * l_s[rows, :] + a_new * l_b
                acc_s[rows, :] = a_old * acc_s[rows, :] + a_new * acc

        lax.fori_loop(0, n_blocks // DILATED_UNROLL, body, 0)

    o_ref[...] = (acc_s[...] / l_s[...]).astype(o_ref.dtype)


def _attn_a(proj, batch, seq):
    n_pairs = A_WIDTH // LANES
    blocks_per_batch = 1

    def col_spec(group):
        return pl.BlockSpec((seq, LANES), lambda b, p, group=group: (b, group * n_pairs + p))

    f32 = jnp.float32
    bf16 = jnp.bfloat16
    del blocks_per_batch
    return pl.pallas_call(
        _attn_a_kernel,
        grid=(batch, n_pairs),
        in_specs=[col_spec(0), col_spec(1), col_spec(2)],
        out_specs=pl.BlockSpec((seq, LANES), lambda b, p: (b, p)),
        out_shape=jax.ShapeDtypeStruct((batch * seq, A_WIDTH), bf16),
        scratch_shapes=[
            pltpu.VMEM((seq, LANES), f32), pltpu.VMEM((seq, LANES), f32), pltpu.VMEM((seq, LANES), f32),
            pltpu.VMEM((seq, LANES), bf16),
            pltpu.VMEM((seq + BAND, LANES), bf16),
            pltpu.VMEM((LANES + SUM_ROWS, seq + BAND), bf16),
            pltpu.VMEM((2, 2 * BAND, PAIR * BAND), f32),
            pltpu.VMEM((seq, LANES), f32), pltpu.VMEM((seq, LANES), f32), pltpu.VMEM((seq, LANES), f32),
        ],
        compiler_params=pltpu.CompilerParams(
            dimension_semantics=("parallel", "parallel"), vmem_limit_bytes=VMEM_LIMIT),
        name="attn_dilated",
    )(proj, proj, proj)


def _attn_c_kernel(sinks_ref, q_ref, k_ref, v_ref, o_ref, kd, vdt, bias_s):
    seq = q_ref.shape[0]
    pair = pl.program_id(1)
    piece = 2 * BAND
    kd[0:BAND, :] = jnp.zeros((BAND, LANES), kd.dtype)
    kd[BAND:, :] = k_ref[...]
    vdt[:, 0:BAND] = jnp.zeros((LANES + SUM_ROWS, BAND), vdt.dtype)
    vdt[LANES:, BAND:] = jnp.ones((SUM_ROWS, seq), vdt.dtype)

    def transpose_v(i, carry):
        rows = pl.ds(pl.multiple_of(i * piece, piece), piece)
        vdt[0:LANES, pl.ds(pl.multiple_of(BAND + i * piece, BAND), piece)] = (
            v_ref[rows, :].astype(jnp.float32).T.astype(vdt.dtype))
        return carry

    lax.fori_loop(0, seq // piece, transpose_v, 0, unroll=PREP_UNROLL)
    _fill_band_bias(bias_s, 1)
    col = lax.broadcasted_iota(jnp.int32, (1, PAIR * BAND), 1)
    sink_row = jnp.where(col < BAND, sinks_ref[PAIR * pair], sinks_ref[PAIR * pair + 1]) * LOG2E

    def body(it, carry):
        blocks = [it * SWA_UNROLL + u for u in range(SWA_UNROLL)]
        operands = []
        for n in blocks:
            row0 = pl.multiple_of(n * BAND, BAND)
            operands.append((q_ref[pl.ds(row0, BAND), :], kd[pl.ds(row0, 2 * BAND), :],
                             vdt[:, pl.ds(row0, 2 * BAND)], bias_s[jnp.where(n == 0, 1, 0)]))
        for n, (o_t, m) in zip(blocks, _band_blocks(operands, sink_row)):
            l = o_t[LANES:LANES + 1, :] + jnp.exp2(sink_row - m)
            out_t = _pair_tile(o_t) * _pair_stat_tile(1.0 / l)
            o_ref[pl.ds(pl.multiple_of(n * BAND, BAND), BAND), :] = out_t.T.astype(o_ref.dtype)
        return carry

    lax.fori_loop(0, seq // (BAND * SWA_UNROLL), body, 0)


def _attn_c(proj, sinks, batch, seq):
    n_pairs = C_WIDTH // LANES
    pairs_per_kv = C_GROUP // PAIR
    k_tile0 = C_WIDTH // LANES
    v_tile0 = k_tile0 + C_KV_HEADS
    bf16 = jnp.bfloat16
    return pl.pallas_call(
        _attn_c_kernel,
        grid=(batch, n_pairs),
        in_specs=[
            pl.BlockSpec(memory_space=pltpu.SMEM),
            pl.BlockSpec((seq, LANES), lambda b, p: (b, p)),
            pl.BlockSpec((seq, LANES), lambda b, p: (b, k_tile0 + p // pairs_per_kv)),
            pl.BlockSpec((seq, LANES), lambda b, p: (b, v_tile0 + p // pairs_per_kv)),
        ],
        out_specs=pl.BlockSpec((seq, LANES), lambda b, p: (b, p)),
        out_shape=jax.ShapeDtypeStruct((batch * seq, C_WIDTH), bf16),
        scratch_shapes=[
            pltpu.VMEM((seq + BAND, LANES), bf16),
            pltpu.VMEM((LANES + SUM_ROWS, seq + BAND), bf16),
            pltpu.VMEM((2, 2 * BAND, PAIR * BAND), jnp.float32),
        ],
        compiler_params=pltpu.CompilerParams(
            dimension_semantics=("parallel", "parallel"), vmem_limit_bytes=VMEM_LIMIT),
        name="attn_swa",
    )(sinks, proj, proj, proj)


_T_TILE, _T_CH, _T_FIRST, _T_LAST, _T_ROWS = range(5)


def _moba_step_table(n_tiles):
    tiles, chs, firsts, lasts = [], [], [], []
    for tile in range(n_tiles):
        order = [tile] + list(range(tile))
        for i, ch in enumerate(order):
            tiles.append(tile)
            chs.append(ch)
            firsts.append(int(i == 0))
            lasts.append(int(i == len(order) - 1))
    n_steps = len(tiles)
    rows = [None] * _T_ROWS
    rows[_T_TILE] = tiles + [tiles[-1]]
    rows[_T_CH] = chs + [chs[-1]]
    rows[_T_FIRST] = firsts + [0]
    rows[_T_LAST] = lasts + [0]
    return n_steps, [v for row in rows for v in row]


def _moba_kernel(tbl_ref, q_ref, k_ref, v_ref, o_ref, vt, q2_all, kmean, bias_s, causal_s, m_s, acc_s,
                 s_a, s_b, *, n_steps):
    seq = q_ref.shape[0]
    nb = seq // MOBA_BLOCK
    nq2 = PAIR * MOBA_BLOCK
    nq = MOBA_CHUNK_BLOCKS * nq2
    n_tiles = nb // MOBA_CHUNK_BLOCKS
    chunk = MOBA_CHUNK_BLOCKS * MOBA_BLOCK

    def tbl(row, t):
        return tbl_ref[row * (n_steps + 1) + t]

    def prep(j, carry):
        rows = pl.ds(pl.multiple_of(j * MOBA_BLOCK, MOBA_BLOCK), MOBA_BLOCK)
        v_t = v_ref[rows, :].astype(jnp.float32).T.astype(vt.dtype)
        for h in range(PAIR):
            vt[h, 0:HEAD_DIM, rows] = v_t[h * HEAD_DIM:(h + 1) * HEAD_DIM]
        kmean[pl.ds(j, 1), :] = jnp.mean(k_ref[rows, :].astype(jnp.float32), axis=0, keepdims=True)
        q2_all[pl.ds(pl.multiple_of(j * nq2, nq2), nq2), :] = _split_heads_rows(q_ref[rows, :])
        return carry

    lax.fori_loop(0, nb, prep, 0, unroll=PREP_UNROLL)
    for h in range(PAIR):
        vt[h, HEAD_DIM:, :] = jnp.ones((SUM_ROWS, seq), vt.dtype)
    km = kmean[...]
    km_hi = km.astype(jnp.bfloat16)
    km_lo = (km - km_hi.astype(jnp.float32)).astype(jnp.bfloat16)
    blk_id = lax.broadcasted_iota(jnp.int32, (nb, nq2), 0)
    key_id = lax.broadcasted_iota(jnp.int32, (chunk, nq), 0)
    col_id = lax.broadcasted_iota(jnp.int32, (chunk, nq), 1)
    causal_s[...] = jnp.where(key_id <= col_id % MOBA_BLOCK + (col_id // nq2) * MOBA_BLOCK, 0.0, NEG)

    def chunk_rows(ch):
        return pl.ds(pl.multiple_of(ch * chunk, chunk), chunk)

    def scores_t(t):
        q2 = q2_all[pl.ds(pl.multiple_of(tbl(_T_TILE, t) * nq, nq), nq), :]
        return lax.dot_general(k_ref[chunk_rows(tbl(_T_CH, t)), :], q2, _NT,
                               preferred_element_type=jnp.float32)

    def pv_t(t, p):
        p = p.astype(vt.dtype)
        rows = chunk_rows(tbl(_T_CH, t))
        return jnp.concatenate(
            [jnp.dot(vt[g % PAIR, :, rows], p[:, g * MOBA_BLOCK:(g + 1) * MOBA_BLOCK],
                     preferred_element_type=jnp.float32) for g in range(nq // MOBA_BLOCK)], axis=1)

    def block_bias(t):
        row0 = tbl(_T_TILE, t) * nb + tbl(_T_CH, t) * MOBA_CHUNK_BLOCKS
        return [bias_s[pl.ds(row0 + i, 1), :] for i in range(MOBA_CHUNK_BLOCKS)]

    def per_block(x, fn):
        return [fn(x[i * MOBA_BLOCK:(i + 1) * MOBA_BLOCK], i) for i in range(MOBA_CHUNK_BLOCKS)]

    def select_blocks(tile, carry):
        for i in range(MOBA_CHUNK_BLOCKS):
            qb = tile * MOBA_CHUNK_BLOCKS + i
            q2 = q2_all[pl.ds(pl.multiple_of(qb * nq2, nq2), nq2), :]
            gate = (lax.dot_general(km_hi, q2, _NT, preferred_element_type=jnp.float32)
                    + lax.dot_general(km_lo, q2, _NT, preferred_element_type=jnp.float32))
            past = blk_id < qb
            gate = jnp.where(past, gate, NEG)
            rank = jnp.zeros((nb, nq2), jnp.int32)
            for j in range(nb - 1):
                gj = gate[j:j + 1, :]
                beats = (gj > gate) | ((gj == gate) & (j < blk_id))
                rank = rank + jnp.where(beats & (j < qb), 1, 0)
            allowed = (past & (rank < MOBA_TOPK)) | (blk_id == qb)
            bias_s[pl.ds(pl.multiple_of(tile * nb, nb), nb), i * nq2:(i + 1) * nq2] = jnp.where(allowed, 0.0, NEG)
        return carry

    lax.fori_loop(0, n_tiles, select_blocks, 0, unroll=2)

    def attend_first(t, s_cur):
        bias = block_bias(t)
        s = s_cur[...] + causal_s[...]
        s = jnp.concatenate(per_block(s, lambda x, i: x + bias[i]), axis=0)
        m0 = jnp.max(s, axis=0, keepdims=True)
        p = jnp.exp2(s - m0)
        m_s[...] = m0
        acc_s[...] = pv_t(t, p)

    def attend_next(t, s_cur):
        s = s_cur[...]
        bias = block_bias(t)
        m_old = m_s[...]
        m_blk = per_block(s, lambda x, i: jnp.max(x, axis=0, keepdims=True) + bias[i])
        m_new = functools.reduce(jnp.maximum, m_blk, m_old)
        alpha = jnp.exp2(m_old - m_new)
        p = jnp.concatenate(per_block(s, lambda x, i: jnp.exp2(x - (m_new - bias[i]))), axis=0)
        m_s[...] = m_new
        acc_s[...] = alpha * acc_s[...] + pv_t(t, p)

    def finalize(tile):
        o_all = acc_s[0:HEAD_DIM, :] * (1.0 / acc_s[HEAD_DIM:HEAD_DIM + 1, :])
        for i in range(MOBA_CHUNK_BLOCKS):
            c0 = i * nq2
            o_t = jnp.concatenate([o_all[:, c0:c0 + MOBA_BLOCK], o_all[:, c0 + MOBA_BLOCK:c0 + nq2]],
                                  axis=0)
            rows = pl.ds(pl.multiple_of((tile * MOBA_CHUNK_BLOCKS + i) * MOBA_BLOCK, MOBA_BLOCK), MOBA_BLOCK)
            o_ref[rows, :] = o_t.T.astype(o_ref.dtype)

    def step(t, s_cur, s_nxt):
        def first():
            s_nxt[...] = scores_t(t + 1)
            attend_first(t, s_cur)

        def later():
            s_nxt[...] = scores_t(t + 1)
            attend_next(t, s_cur)

        lax.cond(tbl(_T_FIRST, t) == 1, first, later)

        @pl.when(tbl(_T_LAST, t) == 1)
        def _():
            finalize(tbl(_T_TILE, t))

    s_a[...] = scores_t(0)

    def two_steps(i, carry):
        step(2 * i, s_a, s_b)
        step(2 * i + 1, s_b, s_a)
        return carry

    lax.fori_loop(0, n_steps // 2, two_steps, 0)


def _attn_b(proj, batch, seq):
    n_pairs = B_WIDTH // LANES
    tile0 = 4 * (A_WIDTH // LANES)

    def col_spec(group):
        return pl.BlockSpec((seq, LANES), lambda b, p, group=group: (b, tile0 + group * n_pairs + p))

    nb = seq // MOBA_BLOCK
    nq2 = PAIR * MOBA_BLOCK
    chunk = MOBA_CHUNK_BLOCKS * MOBA_BLOCK
    assert nb % MOBA_CHUNK_BLOCKS == 0
    n_tiles = nb // MOBA_CHUNK_BLOCKS
    nq = MOBA_CHUNK_BLOCKS * nq2
    n_steps, table = _moba_step_table(n_tiles)
    assert n_steps % 2 == 0
    f32 = jnp.float32
    bf16 = jnp.bfloat16
    return pl.pallas_call(
        functools.partial(_moba_kernel, n_steps=n_steps),
        grid=(batch, n_pairs),
        in_specs=[pl.BlockSpec(memory_space=pltpu.SMEM), col_spec(0), col_spec(1), col_spec(2)],
        out_specs=pl.BlockSpec((seq, LANES), lambda b, p: (b, p)),
        out_shape=jax.ShapeDtypeStruct((batch * seq, B_WIDTH), bf16),
        scratch_shapes=[
            pltpu.VMEM((PAIR, HEAD_DIM + SUM_ROWS, seq), bf16),
            pltpu.VMEM((nb * nq2, LANES), bf16),
            pltpu.VMEM((nb, LANES), f32),
            pltpu.VMEM((n_tiles * nb, nq), f32),
            pltpu.VMEM((chunk, nq), f32),
            pltpu.VMEM((1, nq), f32),
            pltpu.VMEM((HEAD_DIM + SUM_ROWS, nq), f32),
            pltpu.VMEM((chunk, nq), f32), pltpu.VMEM((chunk, nq), f32),
        ],
        compiler_params=pltpu.CompilerParams(
            dimension_semantics=("parallel", "parallel"), vmem_limit_bytes=VMEM_LIMIT),
        name="attn_moba",
    )(jnp.asarray(table, jnp.int32), proj, proj, proj)


def _rope_tables(seq):
    pos = jnp.arange(seq, dtype=jnp.float32)
    inv_freq = ROPE_THETA ** (-jnp.arange(0, HEAD_DIM, 2, dtype=jnp.float32) / HEAD_DIM)
    ang = pos[:, None] * inv_freq[None, :]
    cos = jnp.concatenate([jnp.cos(ang)] * (2 * PAIR), axis=-1)
    sin = jnp.concatenate([-jnp.sin(ang)] * PAIR + [jnp.sin(ang)] * PAIR, axis=-1)
    return cos, sin


def _tile_gain(g, width):
    return _qk_columns(jnp.tile(g, width // HEAD_DIM)[None, :])[0]


def _qk_columns(w):
    order = jnp.asarray(_qk_lane_order(), jnp.int32)
    tiles = w.reshape(w.shape[:-1] + (w.shape[-1] // LANES, LANES))
    return jnp.take(tiles, order, axis=-1).reshape(w.shape)


def kernel(x, norm_even, w_in_even, w_out_even, qnorm_a, knorm_a, qnorm_b, knorm_b,
           norm_odd, w_in_odd, w_out_odd, qnorm_c, knorm_c, sinks_c):
    batch, seq, d = x.shape
    bf16 = jnp.bfloat16
    scale = HEAD_DIM ** -0.5 * LOG2E
    cos, sin = _rope_tables(seq)
    x2 = x.reshape(batch * seq, d)

    ones = jnp.ones((A_WIDTH,), jnp.float32)
    gain0 = jnp.concatenate([
        _tile_gain(qnorm_a[0], A_WIDTH) * scale, _tile_gain(knorm_a[0], A_WIDTH), ones, ones,
        _tile_gain(qnorm_b[0], B_WIDTH) * scale, _tile_gain(knorm_b[0], B_WIDTH), ones, ones])[None, :]
    tiles_per_group = A_WIDTH // LANES
    rope0 = frozenset(t for grp in (0, 1, 4, 5) for t in range(grp * tiles_per_group, (grp + 1) * tiles_per_group))
    w0 = w_in_even[0]
    groups = [w0[:, g * A_WIDTH:(g + 1) * A_WIDTH] for g in range(8)]
    w0r = jnp.concatenate([_qk_columns(w) if g in (0, 1, 4, 5) else w for g, w in enumerate(groups)], axis=1)
    proj0 = _inproj(x2, norm_even[0][None, :], w0r.astype(bf16), gain0, cos, sin, rope0, seq)
    oa = _attn_a(proj0, batch, seq)
    ob = _attn_b(proj0, batch, seq)
    x1 = _outproj(x2, [(oa, 0), (ob, 0)], [(proj0, 3), (proj0, 7)], w_out_even[0].astype(bf16), A_WIDTH)

    w1 = w_in_odd[0]
    kv_w = C_KV_HEADS * HEAD_DIM
    wq, wk, wv, wz = (w1[:, :C_WIDTH], w1[:, C_WIDTH:C_WIDTH + kv_w],
                      w1[:, C_WIDTH + kv_w:C_WIDTH + 2 * kv_w], w1[:, C_WIDTH + 2 * kv_w:])

    def dup_heads(w):
        return jnp.concatenate([w[:, h * HEAD_DIM:(h + 1) * HEAD_DIM]
                                for h in range(C_KV_HEADS) for _ in range(PAIR)], axis=1)

    w1r = jnp.concatenate([_qk_columns(wq), _qk_columns(dup_heads(wk)), dup_heads(wv), wz],
                          axis=1).astype(bf16)
    dup_w = PAIR * kv_w
    gain1 = jnp.concatenate([
        _tile_gain(qnorm_c[0], C_WIDTH) * scale, _tile_gain(knorm_c[0], dup_w),
        jnp.ones((dup_w + C_WIDTH,), jnp.float32)])[None, :]
    rope1 = frozenset(range((C_WIDTH + dup_w) // LANES))
    proj1 = _inproj(x1, norm_odd[0][None, :], w1r, gain1, cos, sin, rope1, seq)
    oc = _attn_c(proj1, sinks_c[0], batch, seq)
    half = C_WIDTH // 2
    z_blk = (C_WIDTH + 2 * dup_w) // half
    out = _outproj(x1, [(oc, 0), (oc, 1)], [(proj1, z_blk), (proj1, z_blk + 1)],
                   w_out_odd[0].astype(bf16), half)
    return out.reshape(batch, seq, d)
```

```python
import functools

import jax
import jax.numpy as jnp
from jax import lax
from jax.experimental import pallas as pl
from jax.experimental.pallas import tpu as pltpu

D_MODEL = 1024
HEAD_DIM = 64
LANES = 128
PAIR = LANES // HEAD_DIM
HALF_DIM = HEAD_DIM // 2
NORM_COLS = 2 * LANES
A_WIDTH = 512
B_WIDTH = 512
C_WIDTH = 1024
C_KV_HEADS = 2
C_GROUP = 8
DILATIONS = (1, 4, 16)
BAND = 128
MOBA_BLOCK = 256
MOBA_TOPK = 3
MOBA_CHUNK_BLOCKS = 2
DILATED_UNROLL = 8
SWA_UNROLL = 16
PREP_UNROLL = 4
SUM_ROWS = 16
LOG2E = 1.4426950408889634
ROPE_THETA = 10000.0
NORM_EPS = 1e-6
NEG = -1e30
NO_TIE = 3e38
VMEM_LIMIT = 56 * 1024 * 1024
PROJ_ROWS = 512
PROJ_COLS = 512

_NT = (((1,), (1,)), ((), ()))


def _lane_ids(shape):
    return lax.broadcasted_iota(jnp.int32, shape, len(shape) - 1)


def _qk_head_of_lane(lane):
    return (lane // HALF_DIM) % PAIR


def _qk_lane_order():
    return [h * HEAD_DIM + half * HALF_DIM + d for half in range(2) for h in range(PAIR) for d in range(HALF_DIM)]


def _split_heads_rows(t):
    head = _qk_head_of_lane(_lane_ids(t.shape))
    zero = jnp.zeros_like(t)
    return jnp.concatenate([jnp.where(head == 0, t, zero), jnp.where(head == 1, t, zero)], axis=0)


def _inproj_kernel(x_ref, g_ref, w_ref, gain_ref, cos_ref, sin_ref, o_ref, *, rope_tiles):
    x = x_ref[...]
    ms = jnp.mean(x * x, axis=-1, keepdims=True)
    h = (x * lax.rsqrt(ms + NORM_EPS) * g_ref[...]).astype(jnp.bfloat16)
    n_cols = o_ref.shape[1]
    li = lax.broadcasted_iota(jnp.int32, (NORM_COLS, NORM_COLS), 0)
    lj = lax.broadcasted_iota(jnp.int32, (NORM_COLS, NORM_COLS), 1)
    same_head = ((li // LANES == lj // LANES)
                 & (_qk_head_of_lane(li) == _qk_head_of_lane(lj))).astype(jnp.bfloat16)
    for c0 in range(0, n_cols, PROJ_COLS):
        r = jnp.dot(h, w_ref[:, c0:c0 + PROJ_COLS], preferred_element_type=jnp.float32)
        for p0 in range(0, PROJ_COLS, NORM_COLS):
            col = c0 + p0
            tiles = range(col // LANES, (col + NORM_COLS) // LANES)
            t = r[:, p0:p0 + NORM_COLS]
            if all(tile in rope_tiles for tile in tiles):
                ss = jnp.dot((t * t).astype(jnp.bfloat16), same_head, preferred_element_type=jnp.float32)
                t = t * lax.rsqrt(ss * (1.0 / HEAD_DIM) + NORM_EPS) * gain_ref[:, col:col + NORM_COLS]
                for t0 in range(0, NORM_COLS, LANES):
                    tn = t[:, t0:t0 + LANES]
                    tn = tn * cos_ref[...] + pltpu.roll(tn, LANES // 2, 1) * sin_ref[...]
                    o_ref[:, col + t0:col + t0 + LANES] = tn.astype(o_ref.dtype)
            else:
                assert not any(tile in rope_tiles for tile in tiles)
                o_ref[:, col:col + NORM_COLS] = t.astype(o_ref.dtype)


def _inproj(x2, g, w, gain, cos, sin, rope_tiles, seq):
    rows, d = x2.shape
    n_cols = w.shape[1]
    seq_tiles = seq // PROJ_ROWS
    return pl.pallas_call(
        functools.partial(_inproj_kernel, rope_tiles=rope_tiles),
        grid=(rows // PROJ_ROWS,),
        in_specs=[
            pl.BlockSpec((PROJ_ROWS, d), lambda i: (i, 0)),
            pl.BlockSpec((1, d), lambda i: (0, 0)),
            pl.BlockSpec((d, n_cols), lambda i: (0, 0)),
            pl.BlockSpec((1, n_cols), lambda i: (0, 0)),
            pl.BlockSpec((PROJ_ROWS, LANES), lambda i: (i % seq_tiles, 0)),
            pl.BlockSpec((PROJ_ROWS, LANES), lambda i: (i % seq_tiles, 0)),
        ],
        out_specs=pl.BlockSpec((PROJ_ROWS, n_cols), lambda i: (i, 0)),
        out_shape=jax.ShapeDtypeStruct((rows, n_cols), jnp.bfloat16),
        compiler_params=pltpu.CompilerParams(
            dimension_semantics=("parallel",), vmem_limit_bytes=VMEM_LIMIT),
        name="inproj",
    )(x2, g, w, gain, cos, sin)


def _outproj_kernel(*refs, n_parts):
    x_ref = refs[0]
    y_refs = refs[1:1 + n_parts]
    z_refs = refs[1 + n_parts:1 + 2 * n_parts]
    w_ref = refs[1 + 2 * n_parts]
    o_ref = refs[2 + 2 * n_parts]
    acc = x_ref[...]
    k0 = 0
    for y_ref, z_ref in zip(y_refs, z_refs):
        z = z_ref[...].astype(jnp.float32)
        gated = y_ref[...].astype(jnp.float32) * (z / (1.0 + jnp.exp(-z)))
        kw = y_ref.shape[1]
        acc = acc + jnp.dot(gated.astype(jnp.bfloat16), w_ref[k0:k0 + kw, :],
                            preferred_element_type=jnp.float32)
        k0 += kw
    o_ref[...] = acc


def _outproj(x2, ys, zs, w, part_width):
    rows, d = x2.shape
    n_parts = len(ys)
    in_specs = [pl.BlockSpec((PROJ_ROWS, d), lambda i: (i, 0))]
    for _, blk in list(ys) + list(zs):
        in_specs.append(pl.BlockSpec((PROJ_ROWS, part_width), lambda i, blk=blk: (i, blk)))
    in_specs.append(pl.BlockSpec(w.shape, lambda i: (0, 0)))
    return pl.pallas_call(
        functools.partial(_outproj_kernel, n_parts=n_parts),
        grid=(rows // PROJ_ROWS,),
        in_specs=in_specs,
        out_specs=pl.BlockSpec((PROJ_ROWS, d), lambda i: (i, 0)),
        out_shape=jax.ShapeDtypeStruct((rows, d), jnp.float32),
        compiler_params=pltpu.CompilerParams(
            dimension_semantics=("parallel",), vmem_limit_bytes=VMEM_LIMIT),
        name="outproj",
    )(x2, *[y for y, _ in ys], *[z for z, _ in zs], w)


def _fill_band_bias(bias_ref, first_key):
    kj = lax.broadcasted_iota(jnp.int32, (2 * BAND, PAIR * BAND), 0)
    qi = lax.broadcasted_iota(jnp.int32, (2 * BAND, PAIR * BAND), 1) % BAND
    band = (kj >= qi + first_key) & (kj <= qi + BAND)
    bias_ref[0] = jnp.where(band, 0.0, NEG)
    bias_ref[1] = jnp.where(band & (kj >= BAND), 0.0, NEG)


def _band_blocks(operands, sink_row):
    scores = [lax.dot_general(kw, _split_heads_rows(q), _NT, preferred_element_type=jnp.float32) + bias
              for q, kw, _, bias in operands]
    probs = []
    for s in scores:
        m = jnp.max(s, axis=0, keepdims=True)
        if sink_row is not None:
            m = jnp.maximum(m, sink_row)
        probs.append((jnp.exp2(s - m).astype(jnp.bfloat16), m))
    return [(jnp.dot(vtw, p, preferred_element_type=jnp.float32), m)
            for (_, _, vtw, _), (p, m) in zip(operands, probs)]


def _pair_tile(x):
    return jnp.concatenate([x[0:HEAD_DIM, 0:BAND], x[HEAD_DIM:LANES, BAND:PAIR * BAND]], axis=0)


def _pair_stat_tile(v):
    return jnp.concatenate([jnp.broadcast_to(v[:, 0:BAND], (HEAD_DIM, BAND)),
                            jnp.broadcast_to(v[:, BAND:PAIR * BAND], (HEAD_DIM, BAND))], axis=0)


def _attn_a_kernel(q_ref, k_ref, v_ref, o_ref, qf, kf, vf, qd, kd, vdt, bias_s, m_s, l_s, acc_s):
    seq = q_ref.shape[0]
    n_blocks = seq // BAND
    piece = 2 * BAND
    qf[...] = q_ref[...].astype(jnp.float32)
    kf[...] = k_ref[...].astype(jnp.float32)
    vf[...] = v_ref[...].astype(jnp.float32)
    kd[0:BAND, :] = jnp.zeros((BAND, LANES), kd.dtype)
    vdt[:, 0:BAND] = jnp.zeros((LANES + SUM_ROWS, BAND), vdt.dtype)
    vdt[LANES:, BAND:] = jnp.ones((SUM_ROWS, seq), vdt.dtype)
    _fill_band_bias(bias_s, 0)

    for dil in DILATIONS:
        sub = seq // dil
        blocks_per_sub = sub // BAND
        pieces_per_sub = sub // piece

        def deinterleave(i, carry, dil=dil, pieces_per_sub=pieces_per_sub):
            dst = pl.multiple_of(i * piece, piece)
            if dil == 1:
                src = pl.ds(dst, piece)
                qd[pl.ds(dst, piece), :] = q_ref[src, :]
                kd[pl.ds(BAND + dst, piece), :] = k_ref[src, :]
            else:
                r = i // pieces_per_sub
                j = i % pieces_per_sub
                src = pl.ds(r + j * (piece * dil), piece, stride=dil)
                qd[pl.ds(dst, piece), :] = qf[src, :].astype(qd.dtype)
                kd[pl.ds(BAND + dst, piece), :] = kf[src, :].astype(kd.dtype)
            vdt[0:LANES, pl.ds(BAND + dst, piece)] = vf[src, :].T.astype(vdt.dtype)
            return carry

        lax.fori_loop(0, seq // piece, deinterleave, 0, unroll=PREP_UNROLL)

        def body(it, carry, dil=dil, blocks_per_sub=blocks_per_sub):
            blocks = [it * DILATED_UNROLL + u for u in range(DILATED_UNROLL)]
            operands = []
            for n in blocks:
                row0 = pl.multiple_of(n * BAND, BAND)
                first = (n % blocks_per_sub) == 0
                operands.append((qd[pl.ds(row0, BAND), :], kd[pl.ds(row0, 2 * BAND), :],
                                 vdt[:, pl.ds(row0, 2 * BAND)], bias_s[jnp.where(first, 1, 0)]))
            for n, (o_t, m) in zip(blocks, _band_blocks(operands, None)):
                merge_block(n, o_t, m, dil, blocks_per_sub)
            return carry

        def merge_block(n, o_t, m, dil, blocks_per_sub):
            r = n // blocks_per_sub
            c = n % blocks_per_sub
            row0 = pl.multiple_of(n * BAND, BAND)
            acc = _pair_tile(o_t).T
            m_b = _pair_stat_tile(m).T
            l_b = _pair_stat_tile(o_t[LANES:LANES + 1, :]).T
            if dil == 1:
                rows = pl.ds(row0, BAND)
                m_s[rows, :] = m_b
                l_s[rows, :] = l_b
                acc_s[rows, :] = acc
            else:
                rows = pl.ds(c * (BAND * dil) + r, BAND, stride=dil)
                m_old = m_s[rows, :]
                m_new = jnp.maximum(m_old, m_b)
                a_old = jnp.exp2(m_old - m_new)
                a_new = jnp.exp2(m_b - m_new)
                m_s[rows, :] = m_new
                l_s[rows, :] = a_old * l_s[rows, :] + a_new * l_b
                acc_s[rows, :] = a_old * acc_s[rows, :] + a_new * acc

        lax.fori_loop(0, n_blocks // DILATED_UNROLL, body, 0)

    o_ref[...] = (acc_s[...] / l_s[...]).astype(o_ref.dtype)


def _attn_a(proj, batch, seq):
    n_pairs = A_WIDTH // LANES
    blocks_per_batch = 1

    def col_spec(group):
        return pl.BlockSpec((seq, LANES), lambda b, p, group=group: (b, group * n_pairs + p))

    f32 = jnp.float32
    bf16 = jnp.bfloat16
    del blocks_per_batch
    return pl.pallas_call(
        _attn_a_kernel,
        grid=(batch, n_pairs),
        in_specs=[col_spec(0), col_spec(1), col_spec(2)],
        out_specs=pl.BlockSpec((seq, LANES), lambda b, p: (b, p)),
        out_shape=jax.ShapeDtypeStruct((batch * seq, A_WIDTH), bf16),
        scratch_shapes=[
            pltpu.VMEM((seq, LANES), f32), pltpu.VMEM((seq, LANES), f32), pltpu.VMEM((seq, LANES), f32),
            pltpu.VMEM((seq, LANES), bf16),
            pltpu.VMEM((seq + BAND, LANES), bf16),
            pltpu.VMEM((LANES + SUM_ROWS, seq + BAND), bf16),
            pltpu.VMEM((2, 2 * BAND, PAIR * BAND), f32),
            pltpu.VMEM((seq, LANES), f32), pltpu.VMEM((seq, LANES), f32), pltpu.VMEM((seq, LANES), f32),
        ],
        compiler_params=pltpu.CompilerParams(
            dimension_semantics=("parallel", "parallel"), vmem_limit_bytes=VMEM_LIMIT),
        name="attn_dilated",
    )(proj, proj, proj)


def _attn_c_kernel(sinks_ref, q_ref, k_ref, v_ref, o_ref, kd, vdt, bias_s):
    seq = q_ref.shape[0]
    pairs_per_kv = q_ref.shape[1] // LANES
    piece = 2 * BAND
    kd[0:BAND, :] = jnp.zeros((BAND, LANES), kd.dtype)
    kd[BAND:, :] = k_ref[...]
    vdt[:, 0:BAND] = jnp.zeros((LANES + SUM_ROWS, BAND), vdt.dtype)
    vdt[LANES:, BAND:] = jnp.ones((SUM_ROWS, seq), vdt.dtype)

    def transpose_v(i, carry):
        rows = pl.ds(pl.multiple_of(i * piece, piece), piece)
        vdt[0:LANES, pl.ds(pl.multiple_of(BAND + i * piece, BAND), piece)] = (
            v_ref[rows, :].astype(jnp.float32).T.astype(vdt.dtype))
        return carry

    lax.fori_loop(0, seq // piece, transpose_v, 0, unroll=PREP_UNROLL)
    _fill_band_bias(bias_s, 1)
    col = lax.broadcasted_iota(jnp.int32, (1, PAIR * BAND), 1)
    iters_per_pair = seq // (BAND * SWA_UNROLL)

    def body(step, carry):
        pair = step // iters_per_pair
        it = step % iters_per_pair
        lanes = pl.ds(pl.multiple_of(pair * LANES, LANES), LANES)
        head0 = PAIR * (pl.program_id(1) * pairs_per_kv + pair)
        sink_row = jnp.where(col < BAND, sinks_ref[head0], sinks_ref[head0 + 1]) * LOG2E
        blocks = [it * SWA_UNROLL + u for u in range(SWA_UNROLL)]
        operands = []
        for n in blocks:
            row0 = pl.multiple_of(n * BAND, BAND)
            operands.append((q_ref[pl.ds(row0, BAND), lanes], kd[pl.ds(row0, 2 * BAND), :],
                             vdt[:, pl.ds(row0, 2 * BAND)], bias_s[jnp.where(n == 0, 1, 0)]))
        for n, (o_t, m) in zip(blocks, _band_blocks(operands, sink_row)):
            l = o_t[LANES:LANES + 1, :] + jnp.exp2(sink_row - m)
            out_t = _pair_tile(o_t) * _pair_stat_tile(1.0 / l)
            o_ref[pl.ds(pl.multiple_of(n * BAND, BAND), BAND), lanes] = out_t.T.astype(o_ref.dtype)
        return carry

    lax.fori_loop(0, pairs_per_kv * iters_per_pair, body, 0)


def _attn_c(proj, sinks, batch, seq):
    group_w = C_GROUP * HEAD_DIM
    k_tile0 = C_WIDTH // LANES
    v_tile0 = k_tile0 + C_KV_HEADS
    bf16 = jnp.bfloat16
    return pl.pallas_call(
        _attn_c_kernel,
        grid=(batch, C_KV_HEADS),
        in_specs=[
            pl.BlockSpec(memory_space=pltpu.SMEM),
            pl.BlockSpec((seq, group_w), lambda b, g: (b, g)),
            pl.BlockSpec((seq, LANES), lambda b, g: (b, k_tile0 + g)),
            pl.BlockSpec((seq, LANES), lambda b, g: (b, v_tile0 + g)),
        ],
        out_specs=pl.BlockSpec((seq, group_w), lambda b, g: (b, g)),
        out_shape=jax.ShapeDtypeStruct((batch * seq, C_WIDTH), bf16),
        scratch_shapes=[
            pltpu.VMEM((seq + BAND, LANES), bf16),
            pltpu.VMEM((LANES + SUM_ROWS, seq + BAND), bf16),
            pltpu.VMEM((2, 2 * BAND, PAIR * BAND), jnp.float32),
        ],
        compiler_params=pltpu.CompilerParams(
            dimension_semantics=("parallel", "parallel"), vmem_limit_bytes=VMEM_LIMIT),
        name="attn_swa",
    )(sinks, proj, proj, proj)


_T_TILE, _T_CH, _T_FIRST, _T_LAST, _T_ROWS = range(5)


def _moba_step_table(n_tiles):
    tiles, chs, firsts, lasts = [], [], [], []
    for tile in range(n_tiles):
        order = [tile] + list(range(tile))
        for i, ch in enumerate(order):
            tiles.append(tile)
            chs.append(ch)
            firsts.append(int(i == 0))
            lasts.append(int(i == len(order) - 1))
    n_steps = len(tiles)
    rows = [None] * _T_ROWS
    rows[_T_TILE] = tiles + [tiles[-1]]
    rows[_T_CH] = chs + [chs[-1]]
    rows[_T_FIRST] = firsts + [0]
    rows[_T_LAST] = lasts + [0]
    return n_steps, [v for row in rows for v in row]


def _moba_kernel(tbl_ref, q_ref, k_ref, v_ref, o_ref, vt, q2_all, kmean, bias_s, causal_s, m_s, acc_s,
                 s_a, s_b, *, n_steps):
    seq = q_ref.shape[0]
    nb = seq // MOBA_BLOCK
    nq2 = PAIR * MOBA_BLOCK
    nq = MOBA_CHUNK_BLOCKS * nq2
    n_tiles = nb // MOBA_CHUNK_BLOCKS
    chunk = MOBA_CHUNK_BLOCKS * MOBA_BLOCK

    def tbl(row, t):
        return tbl_ref[row * (n_steps + 1) + t]

    def prep(j, carry):
        rows = pl.ds(pl.multiple_of(j * MOBA_BLOCK, MOBA_BLOCK), MOBA_BLOCK)
        v_t = v_ref[rows, :].astype(jnp.float32).T.astype(vt.dtype)
        for h in range(PAIR):
            vt[h, 0:HEAD_DIM, rows] = v_t[h * HEAD_DIM:(h + 1) * HEAD_DIM]
        kmean[pl.ds(j, 1), :] = jnp.mean(k_ref[rows, :].astype(jnp.float32), axis=0, keepdims=True)
        q2_all[pl.ds(pl.multiple_of(j * nq2, nq2), nq2), :] = _split_heads_rows(q_ref[rows, :])
        return carry

    lax.fori_loop(0, nb, prep, 0, unroll=PREP_UNROLL)
    for h in range(PAIR):
        vt[h, HEAD_DIM:, :] = jnp.ones((SUM_ROWS, seq), vt.dtype)
    km = kmean[...]
    km_hi = km.astype(jnp.bfloat16)
    km_hilo = jnp.concatenate([km_hi, (km - km_hi.astype(jnp.float32)).astype(jnp.bfloat16)], axis=0)
    blk_id = lax.broadcasted_iota(jnp.int32, (nb, nq2), 0)
    key_id = lax.broadcasted_iota(jnp.int32, (chunk, nq), 0)
    col_id = lax.broadcasted_iota(jnp.int32, (chunk, nq), 1)
    causal_s[...] = jnp.where(key_id <= col_id % MOBA_BLOCK + (col_id // nq2) * MOBA_BLOCK, 0.0, NEG)

    def chunk_rows(ch):
        return pl.ds(pl.multiple_of(ch * chunk, chunk), chunk)

    def scores_t(t):
        q2 = q2_all[pl.ds(pl.multiple_of(tbl(_T_TILE, t) * nq, nq), nq), :]
        return lax.dot_general(k_ref[chunk_rows(tbl(_T_CH, t)), :], q2, _NT,
                               preferred_element_type=jnp.float32)

    def pv_t(t, p):
        p = p.astype(vt.dtype)
        rows = chunk_rows(tbl(_T_CH, t))
        return jnp.concatenate(
            [jnp.dot(vt[g % PAIR, :, rows], p[:, g * MOBA_BLOCK:(g + 1) * MOBA_BLOCK],
                     preferred_element_type=jnp.float32) for g in range(nq // MOBA_BLOCK)], axis=1)

    def block_bias(t):
        row0 = tbl(_T_TILE, t) * nb + tbl(_T_CH, t) * MOBA_CHUNK_BLOCKS
        return [bias_s[pl.ds(row0 + i, 1), :] for i in range(MOBA_CHUNK_BLOCKS)]

    def per_block(x, fn):
        return [fn(x[i * MOBA_BLOCK:(i + 1) * MOBA_BLOCK], i) for i in range(MOBA_CHUNK_BLOCKS)]

    def select_blocks(tile, carry):
        for i in range(MOBA_CHUNK_BLOCKS):
            qb = tile * MOBA_CHUNK_BLOCKS + i
            q2 = q2_all[pl.ds(pl.multiple_of(qb * nq2, nq2), nq2), :]
            g2 = lax.dot_general(km_hilo, q2, _NT, preferred_element_type=jnp.float32)
            past = blk_id < qb
            gate = jnp.where(past, g2[0:nb] + g2[nb:2 * nb], NEG)
            rank = jnp.zeros((nb, nq2), jnp.int32)
            for j in range(nb - 1):
                gj = gate[j:j + 1, :]
                tie_ref = jnp.where(blk_id > j, gate, NO_TIE)
                rank = rank + jnp.where(gj > gate, 1, 0) + jnp.where(gj == tie_ref, 1, 0)
            allowed = (past & (rank < MOBA_TOPK)) | (blk_id == qb)
            bias_s[pl.ds(pl.multiple_of(tile * nb, nb), nb), i * nq2:(i + 1) * nq2] = jnp.where(allowed, 0.0, NEG)
        return carry

    lax.fori_loop(0, n_tiles, select_blocks, 0, unroll=2)

    def attend_first(t, s_cur):
        bias = block_bias(t)
        s = s_cur[...] + causal_s[...]
        s = jnp.concatenate(per_block(s, lambda x, i: x + bias[i]), axis=0)
        m0 = jnp.max(s, axis=0, keepdims=True)
        p = jnp.exp2(s - m0)
        m_s[...] = m0
        acc_s[...] = pv_t(t, p)

    def attend_next(t, s_cur):
        s = s_cur[...]
        bias = block_bias(t)
        m_old = m_s[...]
        m_blk = per_block(s, lambda x, i: jnp.max(x, axis=0, keepdims=True) + bias[i])
        m_new = functools.reduce(jnp.maximum, m_blk, m_old)
        alpha = jnp.exp2(m_old - m_new)
        p = jnp.concatenate(per_block(s, lambda x, i: jnp.exp2(x - (m_new - bias[i]))), axis=0)
        m_s[...] = m_new
        acc_s[...] = alpha * acc_s[...] + pv_t(t, p)

    def finalize(tile):
        o_all = acc_s[0:HEAD_DIM, :] * (1.0 / acc_s[HEAD_DIM:HEAD_DIM + 1, :])
        for i in range(MOBA_CHUNK_BLOCKS):
            c0 = i * nq2
            o_t = jnp.concatenate([o_all[:, c0:c0 + MOBA_BLOCK], o_all[:, c0 + MOBA_BLOCK:c0 + nq2]],
                                  axis=0)
            rows = pl.ds(pl.multiple_of((tile * MOBA_CHUNK_BLOCKS + i) * MOBA_BLOCK, MOBA_BLOCK), MOBA_BLOCK)
            o_ref[rows, :] = o_t.T.astype(o_ref.dtype)

    def step(t, s_cur, s_nxt):
        def first():
            s_nxt[...] = scores_t(t + 1)
            attend_first(t, s_cur)

        def later():
            s_nxt[...] = scores_t(t + 1)
            attend_next(t, s_cur)

        lax.cond(tbl(_T_FIRST, t) == 1, first, later)

        @pl.when(tbl(_T_LAST, t) == 1)
        def _():
            finalize(tbl(_T_TILE, t))

    s_a[...] = scores_t(0)

    def two_steps(i, carry):
        step(2 * i, s_a, s_b)
        step(2 * i + 1, s_b, s_a)
        return carry

    lax.fori_loop(0, n_steps // 2, two_steps, 0)


def _attn_b(proj, batch, seq):
    n_pairs = B_WIDTH // LANES
    tile0 = 4 * (A_WIDTH // LANES)

    def col_spec(group):
        return pl.BlockSpec((seq, LANES), lambda b, p, group=group: (b, tile0 + group * n_pairs + p))

    nb = seq // MOBA_BLOCK
    nq2 = PAIR * MOBA_BLOCK
    chunk = MOBA_CHUNK_BLOCKS * MOBA_BLOCK
    assert nb % MOBA_CHUNK_BLOCKS == 0
    n_tiles = nb // MOBA_CHUNK_BLOCKS
    nq = MOBA_CHUNK_BLOCKS * nq2
    n_steps, table = _moba_step_table(n_tiles)
    assert n_steps % 2 == 0
    f32 = jnp.float32
    bf16 = jnp.bfloat16
    return pl.pallas_call(
        functools.partial(_moba_kernel, n_steps=n_steps),
        grid=(batch, n_pairs),
        in_specs=[pl.BlockSpec(memory_space=pltpu.SMEM), col_spec(0), col_spec(1), col_spec(2)],
        out_specs=pl.BlockSpec((seq, LANES), lambda b, p: (b, p)),
        out_shape=jax.ShapeDtypeStruct((batch * seq, B_WIDTH), bf16),
        scratch_shapes=[
            pltpu.VMEM((PAIR, HEAD_DIM + SUM_ROWS, seq), bf16),
            pltpu.VMEM((nb * nq2, LANES), bf16),
            pltpu.VMEM((nb, LANES), f32),
            pltpu.VMEM((n_tiles * nb, nq), f32),
            pltpu.VMEM((chunk, nq), f32),
            pltpu.VMEM((1, nq), f32),
            pltpu.VMEM((HEAD_DIM + SUM_ROWS, nq), f32),
            pltpu.VMEM((chunk, nq), f32), pltpu.VMEM((chunk, nq), f32),
        ],
        compiler_params=pltpu.CompilerParams(
            dimension_semantics=("parallel", "parallel"), vmem_limit_bytes=VMEM_LIMIT),
        name="attn_moba",
    )(jnp.asarray(table, jnp.int32), proj, proj, proj)


def _rope_tables(seq):
    pos = jnp.arange(seq, dtype=jnp.float32)
    inv_freq = ROPE_THETA ** (-jnp.arange(0, HEAD_DIM, 2, dtype=jnp.float32) / HEAD_DIM)
    ang = pos[:, None] * inv_freq[None, :]
    cos = jnp.concatenate([jnp.cos(ang)] * (2 * PAIR), axis=-1)
    sin = jnp.concatenate([-jnp.sin(ang)] * PAIR + [jnp.sin(ang)] * PAIR, axis=-1)
    return cos, sin


def _tile_gain(g, width):
    return _qk_columns(jnp.tile(g, width // HEAD_DIM)[None, :])[0]


def _qk_columns(w):
    lead = w.shape[:-1]
    tiles = w.reshape(lead + (w.shape[-1] // LANES, PAIR, 2, HALF_DIM))
    return jnp.swapaxes(tiles, -3, -2).reshape(w.shape)


def kernel(x, norm_even, w_in_even, w_out_even, qnorm_a, knorm_a, qnorm_b, knorm_b,
           norm_odd, w_in_odd, w_out_odd, qnorm_c, knorm_c, sinks_c):
    batch, seq, d = x.shape
    bf16 = jnp.bfloat16
    scale = HEAD_DIM ** -0.5 * LOG2E
    cos, sin = _rope_tables(seq)
    x2 = x.reshape(batch * seq, d)

    ones = jnp.ones((A_WIDTH,), jnp.float32)
    gain0 = jnp.concatenate([
        _tile_gain(qnorm_a[0], A_WIDTH) * scale, _tile_gain(knorm_a[0], A_WIDTH), ones, ones,
        _tile_gain(qnorm_b[0], B_WIDTH) * scale, _tile_gain(knorm_b[0], B_WIDTH), ones, ones])[None, :]
    tiles_per_group = A_WIDTH // LANES
    rope0 = frozenset(t for grp in (0, 1, 4, 5) for t in range(grp * tiles_per_group, (grp + 1) * tiles_per_group))
    w0 = w_in_even[0]
    groups = [w0[:, g * A_WIDTH:(g + 1) * A_WIDTH] for g in range(8)]
    w0r = jnp.concatenate([_qk_columns(w) if g in (0, 1, 4, 5) else w for g, w in enumerate(groups)], axis=1)
    proj0 = _inproj(x2, norm_even[0][None, :], w0r.astype(bf16), gain0, cos, sin, rope0, seq)
    oa = _attn_a(proj0, batch, seq)
    ob = _attn_b(proj0, batch, seq)
    x1 = _outproj(x2, [(oa, 0), (ob, 0)], [(proj0, 3), (proj0, 7)], w_out_even[0].astype(bf16), A_WIDTH)

    w1 = w_in_odd[0]
    kv_w = C_KV_HEADS * HEAD_DIM
    wq, wk, wv, wz = (w1[:, :C_WIDTH], w1[:, C_WIDTH:C_WIDTH + kv_w],
                      w1[:, C_WIDTH + kv_w:C_WIDTH + 2 * kv_w], w1[:, C_WIDTH + 2 * kv_w:])

    def dup_heads(w):
        return jnp.concatenate([w[:, h * HEAD_DIM:(h + 1) * HEAD_DIM]
                                for h in range(C_KV_HEADS) for _ in range(PAIR)], axis=1)

    w1r = jnp.concatenate([_qk_columns(wq), _qk_columns(dup_heads(wk)), dup_heads(wv), wz],
                          axis=1).astype(bf16)
    dup_w = PAIR * kv_w
    gain1 = jnp.concatenate([
        _tile_gain(qnorm_c[0], C_WIDTH) * scale, _tile_gain(knorm_c[0], dup_w),
        jnp.ones((dup_w + C_WIDTH,), jnp.float32)])[None, :]
    rope1 = frozenset(range((C_WIDTH + dup_w) // LANES))
    proj1 = _inproj(x1, norm_odd[0][None, :], w1r, gain1, cos, sin, rope1, seq)
    oc = _attn_c(proj1, sinks_c[0], batch, seq)
    half = C_WIDTH // 2
    z_blk = (C_WIDTH + 2 * dup_w) // half
    out = _outproj(x1, [(oc, 0), (oc, 1)], [(proj1, z_blk), (proj1, z_blk + 1)],
                   w_out_odd[0].astype(bf16), half)
    return out.reshape(batch, seq, d)
```

```python
import functools

import jax
import jax.numpy as jnp
from jax import lax
from jax.experimental import pallas as pl
from jax.experimental.pallas import tpu as pltpu

D_MODEL = 1024
HEAD_DIM = 64
LANES = 128
PAIR = LANES // HEAD_DIM
HALF_DIM = HEAD_DIM // 2
NORM_COLS = 2 * LANES
A_WIDTH = 512
B_WIDTH = 512
C_WIDTH = 1024
C_KV_HEADS = 2
C_GROUP = 8
DILATIONS = (16, 4, 1)
BAND = 128
MOBA_BLOCK = 256
MOBA_TOPK = 3
MOBA_CHUNK_BLOCKS = 2
DILATED_UNROLL = 8
SWA_UNROLL = 16
PREP_UNROLL = 4
SUM_ROWS = 16
LOG2E = 1.4426950408889634
ROPE_THETA = 10000.0
NORM_EPS = 1e-6
NEG = -1e30
NO_TIE = 3e38
VMEM_LIMIT = 56 * 1024 * 1024
PROJ_ROWS = 512
PROJ_COLS = 512

_NT = (((1,), (1,)), ((), ()))


def _lane_ids(shape):
    return lax.broadcasted_iota(jnp.int32, shape, len(shape) - 1)


def _qk_head_of_lane(lane):
    return (lane // HALF_DIM) % PAIR


def _qk_lane_order():
    return [h * HEAD_DIM + half * HALF_DIM + d for half in range(2) for h in range(PAIR) for d in range(HALF_DIM)]


def _split_heads_rows(t):
    head = _qk_head_of_lane(_lane_ids(t.shape))
    zero = jnp.zeros_like(t)
    return jnp.concatenate([jnp.where(head == 0, t, zero), jnp.where(head == 1, t, zero)], axis=0)


def _rms_rows(x, g_ref):
    ms = jnp.mean(x * x, axis=-1, keepdims=True)
    return (x * lax.rsqrt(ms + NORM_EPS) * g_ref[...]).astype(jnp.bfloat16)


def _project_columns(h, w_ref, gain_ref, cos_ref, sin_ref, o_ref, rope_tiles):
    n_cols = o_ref.shape[1]
    li = lax.broadcasted_iota(jnp.int32, (NORM_COLS, NORM_COLS), 0)
    lj = lax.broadcasted_iota(jnp.int32, (NORM_COLS, NORM_COLS), 1)
    same_head = ((li // LANES == lj // LANES)
                 & (_qk_head_of_lane(li) == _qk_head_of_lane(lj))).astype(jnp.bfloat16)
    for c0 in range(0, n_cols, PROJ_COLS):
        r = jnp.dot(h, w_ref[:, c0:c0 + PROJ_COLS], preferred_element_type=jnp.float32)
        for p0 in range(0, PROJ_COLS, NORM_COLS):
            col = c0 + p0
            tiles = range(col // LANES, (col + NORM_COLS) // LANES)
            t = r[:, p0:p0 + NORM_COLS]
            if all(tile in rope_tiles for tile in tiles):
                ss = jnp.dot((t * t).astype(jnp.bfloat16), same_head, preferred_element_type=jnp.float32)
                t = t * lax.rsqrt(ss * (1.0 / HEAD_DIM) + NORM_EPS) * gain_ref[:, col:col + NORM_COLS]
                for t0 in range(0, NORM_COLS, LANES):
                    tn = t[:, t0:t0 + LANES]
                    tn = tn * cos_ref[...] + pltpu.roll(tn, LANES // 2, 1) * sin_ref[...]
                    o_ref[:, col + t0:col + t0 + LANES] = tn.astype(o_ref.dtype)
            else:
                assert not any(tile in rope_tiles for tile in tiles)
                o_ref[:, col:col + NORM_COLS] = t.astype(o_ref.dtype)


def _gated_residual(x, y_refs, z_refs, w_ref):
    acc = x
    k0 = 0
    for y_ref, z_ref in zip(y_refs, z_refs):
        z = z_ref[...].astype(jnp.float32)
        gated = y_ref[...].astype(jnp.float32) * (z / (1.0 + jnp.exp(-z)))
        kw = y_ref.shape[1]
        acc = acc + jnp.dot(gated.astype(jnp.bfloat16), w_ref[k0:k0 + kw, :],
                            preferred_element_type=jnp.float32)
        k0 += kw
    return acc


def _inproj_kernel(x_ref, g_ref, w_ref, gain_ref, cos_ref, sin_ref, o_ref, *, rope_tiles):
    _project_columns(_rms_rows(x_ref[...], g_ref), w_ref, gain_ref, cos_ref, sin_ref, o_ref, rope_tiles)


def _outproj_kernel(*refs, n_parts):
    x_ref, y_refs, z_refs = refs[0], refs[1:1 + n_parts], refs[1 + n_parts:1 + 2 * n_parts]
    w_ref, o_ref = refs[1 + 2 * n_parts:]
    o_ref[...] = _gated_residual(x_ref[...], y_refs, z_refs, w_ref)


def _out_in_kernel(*refs, n_parts, rope_tiles):
    x_ref, y_refs, z_refs = refs[0], refs[1:1 + n_parts], refs[1 + n_parts:1 + 2 * n_parts]
    w_out_ref, g_ref, w_in_ref, gain_ref, cos_ref, sin_ref, x_out_ref, proj_ref = refs[1 + 2 * n_parts:]
    x_new = _gated_residual(x_ref[...], y_refs, z_refs, w_out_ref)
    x_out_ref[...] = x_new
    _project_columns(_rms_rows(x_new, g_ref), w_in_ref, gain_ref, cos_ref, sin_ref, proj_ref, rope_tiles)


def _row_spec(width, blk=0):
    return pl.BlockSpec((PROJ_ROWS, width), lambda i, blk=blk: (i, blk))


def _whole_spec(a):
    return pl.BlockSpec(a.shape, lambda i: (0,) * a.ndim)


def _proj_params():
    return pltpu.CompilerParams(dimension_semantics=("parallel",), vmem_limit_bytes=VMEM_LIMIT)


def _inproj_operands(g, w, gain, cos, sin, seq):
    seq_tiles = seq // PROJ_ROWS
    rope_spec = pl.BlockSpec((PROJ_ROWS, LANES), lambda i: (i % seq_tiles, 0))
    return [g, w, gain, cos, sin], [_whole_spec(g), _whole_spec(w), _whole_spec(gain), rope_spec, rope_spec]


def _gated_operands(ys, zs, w, part_width):
    arrays = [a for a, _ in list(ys) + list(zs)] + [w]
    specs = [_row_spec(part_width, blk) for _, blk in list(ys) + list(zs)] + [_whole_spec(w)]
    return arrays, specs


def _inproj(x2, g, w, gain, cos, sin, rope_tiles, seq):
    rows, d = x2.shape
    arrays, specs = _inproj_operands(g, w, gain, cos, sin, seq)
    return pl.pallas_call(
        functools.partial(_inproj_kernel, rope_tiles=rope_tiles),
        grid=(rows // PROJ_ROWS,),
        in_specs=[_row_spec(d)] + specs,
        out_specs=_row_spec(w.shape[1]),
        out_shape=jax.ShapeDtypeStruct((rows, w.shape[1]), jnp.bfloat16),
        compiler_params=_proj_params(),
        name="inproj",
    )(x2, *arrays)


def _outproj(x2, ys, zs, w, part_width):
    rows, d = x2.shape
    arrays, specs = _gated_operands(ys, zs, w, part_width)
    return pl.pallas_call(
        functools.partial(_outproj_kernel, n_parts=len(ys)),
        grid=(rows // PROJ_ROWS,),
        in_specs=[_row_spec(d)] + specs,
        out_specs=_row_spec(d),
        out_shape=jax.ShapeDtypeStruct((rows, d), jnp.float32),
        compiler_params=_proj_params(),
        name="outproj",
    )(x2, *arrays)


def _outproj_inproj(x2, ys, zs, w_out, part_width, g, w_in, gain, cos, sin, rope_tiles, seq):
    rows, d = x2.shape
    out_arrays, out_specs = _gated_operands(ys, zs, w_out, part_width)
    in_arrays, in_specs = _inproj_operands(g, w_in, gain, cos, sin, seq)
    return pl.pallas_call(
        functools.partial(_out_in_kernel, n_parts=len(ys), rope_tiles=rope_tiles),
        grid=(rows // PROJ_ROWS,),
        in_specs=[_row_spec(d)] + out_specs + in_specs,
        out_specs=[_row_spec(d), _row_spec(w_in.shape[1])],
        out_shape=[jax.ShapeDtypeStruct((rows, d), jnp.float32),
                   jax.ShapeDtypeStruct((rows, w_in.shape[1]), jnp.bfloat16)],
        compiler_params=_proj_params(),
        name="outproj_inproj",
    )(x2, *out_arrays, *in_arrays)


def _fill_band_bias(bias_ref, first_key):
    kj = lax.broadcasted_iota(jnp.int32, (2 * BAND, PAIR * BAND), 0)
    qi = lax.broadcasted_iota(jnp.int32, (2 * BAND, PAIR * BAND), 1) % BAND
    band = (kj >= qi + first_key) & (kj <= qi + BAND)
    bias_ref[0] = jnp.where(band, 0.0, NEG)
    bias_ref[1] = jnp.where(band & (kj >= BAND), 0.0, NEG)


def _band_blocks(operands, sink_row):
    scores = [lax.dot_general(kw, _split_heads_rows(q), _NT, preferred_element_type=jnp.float32) + bias
              for q, kw, _, bias in operands]
    probs = []
    for s in scores:
        m = jnp.max(s, axis=0, keepdims=True)
        if sink_row is not None:
            m = jnp.maximum(m, sink_row)
        probs.append((jnp.exp2(s - m).astype(jnp.bfloat16), m))
    return [(jnp.dot(vtw, p, preferred_element_type=jnp.float32), m)
            for (_, _, vtw, _), (p, m) in zip(operands, probs)]


def _pair_tile(x):
    return jnp.concatenate([x[0:HEAD_DIM, 0:BAND], x[HEAD_DIM:LANES, BAND:PAIR * BAND]], axis=0)


def _pair_stat_tile(v):
    return jnp.concatenate([jnp.broadcast_to(v[:, 0:BAND], (HEAD_DIM, BAND)),
                            jnp.broadcast_to(v[:, BAND:PAIR * BAND], (HEAD_DIM, BAND))], axis=0)


def _attn_a_kernel(q_ref, k_ref, v_ref, o_ref, qf, kf, vf, qf4, kf4, vf4, qd, kd, vdt, bias_s, m_s, l_s, acc_s):
    seq = q_ref.shape[0]
    n_blocks = seq // BAND
    piece = 2 * BAND
    mid = 4
    qf[...] = q_ref[...].astype(jnp.float32)
    kf[...] = k_ref[...].astype(jnp.float32)
    vf[...] = v_ref[...].astype(jnp.float32)
    for r in range(mid):
        dst = pl.ds(r * (seq // mid), seq // mid)
        src = pl.ds(r, seq // mid, stride=mid)
        qf4[dst, :] = qf[src, :]
        kf4[dst, :] = kf[src, :]
        vf4[dst, :] = vf[src, :]
    kd[0:BAND, :] = jnp.zeros((BAND, LANES), kd.dtype)
    vdt[:, 0:BAND] = jnp.zeros((LANES + SUM_ROWS, BAND), vdt.dtype)
    vdt[LANES:, BAND:] = jnp.ones((SUM_ROWS, seq), vdt.dtype)
    _fill_band_bias(bias_s, 0)

    for dil in DILATIONS:
        sub = seq // dil
        blocks_per_sub = sub // BAND
        pieces_per_sub = sub // piece

        def deinterleave(i, carry, dil=dil, pieces_per_sub=pieces_per_sub):
            dst = pl.multiple_of(i * piece, piece)
            if dil == 1:
                src = pl.ds(dst, piece)
                q_src, k_src, v_src = q_ref, k_ref, vf
            elif dil == mid:
                src = pl.ds(dst, piece)
                q_src, k_src, v_src = qf4, kf4, vf4
            else:
                r = i // pieces_per_sub
                j = i % pieces_per_sub
                step = dil // mid
                src = pl.ds((r % mid) * (seq // mid) + r // mid + j * (piece * step), piece, stride=step)
                q_src, k_src, v_src = qf4, kf4, vf4
            qd[pl.ds(dst, piece), :] = q_src[src, :].astype(qd.dtype)
            kd[pl.ds(BAND + dst, piece), :] = k_src[src, :].astype(kd.dtype)
            vdt[0:LANES, pl.ds(BAND + dst, piece)] = v_src[src, :].astype(jnp.float32).T.astype(vdt.dtype)
            return carry

        lax.fori_loop(0, seq // piece, deinterleave, 0, unroll=PREP_UNROLL)

        def body(it, carry, dil=dil, blocks_per_sub=blocks_per_sub):
            blocks = [it * DILATED_UNROLL + u for u in range(DILATED_UNROLL)]
            operands = []
            for n in blocks:
                row0 = pl.multiple_of(n * BAND, BAND)
                first = (n % blocks_per_sub) == 0
                operands.append((qd[pl.ds(row0, BAND), :], kd[pl.ds(row0, 2 * BAND), :],
                                 vdt[:, pl.ds(row0, 2 * BAND)], bias_s[jnp.where(first, 1, 0)]))
            for n, (o_t, m) in zip(blocks, _band_blocks(operands, None)):
                merge_block(n, o_t, m, dil, blocks_per_sub)
            return carry

        def merge_block(n, o_t, m, dil, blocks_per_sub):
            r = n // blocks_per_sub
            c = n % blocks_per_sub
            row0 = pl.multiple_of(n * BAND, BAND)
            acc = _pair_tile(o_t).T
            m_b = _pair_stat_tile(m).T
            l_b = _pair_stat_tile(o_t[LANES:LANES + 1, :]).T
            rows = pl.ds(row0, BAND) if dil == 1 else pl.ds(c * (BAND * dil) + r, BAND, stride=dil)
            if dil == DILATIONS[0]:
                m_s[rows, :] = m_b
                l_s[rows, :] = l_b
                acc_s[rows, :] = acc
            else:
                m_old = m_s[rows, :]
                m_new = jnp.maximum(m_old, m_b)
                a_old = jnp.exp2(m_old - m_new)
                a_new = jnp.exp2(m_b - m_new)
                m_s[rows, :] = m_new
                l_s[rows, :] = a_old * l_s[rows, :] + a_new * l_b
                acc_s[rows, :] = a_old * acc_s[rows, :] + a_new * acc

        lax.fori_loop(0, n_blocks // DILATED_UNROLL, body, 0)

    o_ref[...] = (acc_s[...] / l_s[...]).astype(o_ref.dtype)


def _attn_a(proj, batch, seq):
    n_pairs = A_WIDTH // LANES
    blocks_per_batch = 1

    def col_spec(group):
        return pl.BlockSpec((seq, LANES), lambda b, p, group=group: (b, group * n_pairs + p))

    f32 = jnp.float32
    bf16 = jnp.bfloat16
    del blocks_per_batch
    return pl.pallas_call(
        _attn_a_kernel,
        grid=(batch, n_pairs),
        in_specs=[col_spec(0), col_spec(1), col_spec(2)],
        out_specs=pl.BlockSpec((seq, LANES), lambda b, p: (b, p)),
        out_shape=jax.ShapeDtypeStruct((batch * seq, A_WIDTH), bf16),
        scratch_shapes=[
            pltpu.VMEM((seq, LANES), f32), pltpu.VMEM((seq, LANES), f32), pltpu.VMEM((seq, LANES), f32),
            pltpu.VMEM((seq, LANES), f32), pltpu.VMEM((seq, LANES), f32), pltpu.VMEM((seq, LANES), f32),
            pltpu.VMEM((seq, LANES), bf16),
            pltpu.VMEM((seq + BAND, LANES), bf16),
            pltpu.VMEM((LANES + SUM_ROWS, seq + BAND), bf16),
            pltpu.VMEM((2, 2 * BAND, PAIR * BAND), f32),
            pltpu.VMEM((seq, LANES), f32), pltpu.VMEM((seq, LANES), f32), pltpu.VMEM((seq, LANES), f32),
        ],
        compiler_params=pltpu.CompilerParams(
            dimension_semantics=("parallel", "parallel"), vmem_limit_bytes=VMEM_LIMIT),
        name="attn_dilated",
    )(proj, proj, proj)


def _attn_c_kernel(sinks_ref, q_ref, k_ref, v_ref, o_ref, kd, vdt, bias_s):
    seq = q_ref.shape[0]
    pairs_per_kv = q_ref.shape[1] // LANES
    piece = 2 * BAND
    kd[0:BAND, :] = jnp.zeros((BAND, LANES), kd.dtype)
    kd[BAND:, :] = k_ref[...]
    vdt[:, 0:BAND] = jnp.zeros((LANES + SUM_ROWS, BAND), vdt.dtype)
    vdt[LANES:, BAND:] = jnp.ones((SUM_ROWS, seq), vdt.dtype)

    def transpose_v(i, carry):
        rows = pl.ds(pl.multiple_of(i * piece, piece), piece)
        vdt[0:LANES, pl.ds(pl.multiple_of(BAND + i * piece, BAND), piece)] = (
            v_ref[rows, :].astype(jnp.float32).T.astype(vdt.dtype))
        return carry

    lax.fori_loop(0, seq // piece, transpose_v, 0, unroll=PREP_UNROLL)
    _fill_band_bias(bias_s, 1)
    col = lax.broadcasted_iota(jnp.int32, (1, PAIR * BAND), 1)
    iters_per_pair = seq // (BAND * SWA_UNROLL)

    def body(step, carry):
        pair = step // iters_per_pair
        it = step % iters_per_pair
        lanes = pl.ds(pl.multiple_of(pair * LANES, LANES), LANES)
        head0 = PAIR * (pl.program_id(1) * pairs_per_kv + pair)
        sink_row = jnp.where(col < BAND, sinks_ref[head0], sinks_ref[head0 + 1]) * LOG2E
        blocks = [it * SWA_UNROLL + u for u in range(SWA_UNROLL)]
        operands = []
        for n in blocks:
            row0 = pl.multiple_of(n * BAND, BAND)
            operands.append((q_ref[pl.ds(row0, BAND), lanes], kd[pl.ds(row0, 2 * BAND), :],
                             vdt[:, pl.ds(row0, 2 * BAND)], bias_s[jnp.where(n == 0, 1, 0)]))
        for n, (o_t, m) in zip(blocks, _band_blocks(operands, sink_row)):
            l = o_t[LANES:LANES + 1, :] + jnp.exp2(sink_row - m)
            out_t = _pair_tile(o_t) * _pair_stat_tile(1.0 / l)
            o_ref[pl.ds(pl.multiple_of(n * BAND, BAND), BAND), lanes] = out_t.T.astype(o_ref.dtype)
        return carry

    lax.fori_loop(0, pairs_per_kv * iters_per_pair, body, 0)


def _attn_c(proj, sinks, batch, seq):
    group_w = C_GROUP * HEAD_DIM
    k_tile0 = C_WIDTH // LANES
    v_tile0 = k_tile0 + C_KV_HEADS
    bf16 = jnp.bfloat16
    return pl.pallas_call(
        _attn_c_kernel,
        grid=(batch, C_KV_HEADS),
        in_specs=[
            pl.BlockSpec(memory_space=pltpu.SMEM),
            pl.BlockSpec((seq, group_w), lambda b, g: (b, g)),
            pl.BlockSpec((seq, LANES), lambda b, g: (b, k_tile0 + g)),
            pl.BlockSpec((seq, LANES), lambda b, g: (b, v_tile0 + g)),
        ],
        out_specs=pl.BlockSpec((seq, group_w), lambda b, g: (b, g)),
        out_shape=jax.ShapeDtypeStruct((batch * seq, C_WIDTH), bf16),
        scratch_shapes=[
            pltpu.VMEM((seq + BAND, LANES), bf16),
            pltpu.VMEM((LANES + SUM_ROWS, seq + BAND), bf16),
            pltpu.VMEM((2, 2 * BAND, PAIR * BAND), jnp.float32),
        ],
        compiler_params=pltpu.CompilerParams(
            dimension_semantics=("parallel", "parallel"), vmem_limit_bytes=VMEM_LIMIT),
        name="attn_swa",
    )(sinks, proj, proj, proj)


_T_TILE, _T_CH, _T_FIRST, _T_LAST, _T_ROWS = range(5)


def _moba_step_table(n_tiles):
    tiles, chs, firsts, lasts = [], [], [], []
    for tile in range(n_tiles):
        order = [tile] + list(range(tile))
        for i, ch in enumerate(order):
            tiles.append(tile)
            chs.append(ch)
            firsts.append(int(i == 0))
            lasts.append(int(i == len(order) - 1))
    n_steps = len(tiles)
    rows = [None] * _T_ROWS
    rows[_T_TILE] = tiles + [tiles[-1]]
    rows[_T_CH] = chs + [chs[-1]]
    rows[_T_FIRST] = firsts + [0]
    rows[_T_LAST] = lasts + [0]
    return n_steps, [v for row in rows for v in row]


def _moba_kernel(tbl_ref, q_ref, k_ref, v_ref, o_ref, vt, q2_all, kmean, bias_s, causal_s, m_s, acc_s,
                 s_a, s_b, *, n_steps):
    seq = q_ref.shape[0]
    nb = seq // MOBA_BLOCK
    nq2 = PAIR * MOBA_BLOCK
    nq = MOBA_CHUNK_BLOCKS * nq2
    n_tiles = nb // MOBA_CHUNK_BLOCKS
    chunk = MOBA_CHUNK_BLOCKS * MOBA_BLOCK

    def tbl(row, t):
        return tbl_ref[row * (n_steps + 1) + t]

    def prep(j, carry):
        rows = pl.ds(pl.multiple_of(j * MOBA_BLOCK, MOBA_BLOCK), MOBA_BLOCK)
        v_t = v_ref[rows, :].astype(jnp.float32).T.astype(vt.dtype)
        for h in range(PAIR):
            vt[h, 0:HEAD_DIM, rows] = v_t[h * HEAD_DIM:(h + 1) * HEAD_DIM]
        kmean[pl.ds(j, 1), :] = jnp.mean(k_ref[rows, :].astype(jnp.float32), axis=0, keepdims=True)
        q2_all[pl.ds(pl.multiple_of(j * nq2, nq2), nq2), :] = _split_heads_rows(q_ref[rows, :])
        return carry

    lax.fori_loop(0, nb, prep, 0, unroll=PREP_UNROLL)
    for h in range(PAIR):
        vt[h, HEAD_DIM:, :] = jnp.ones((SUM_ROWS, seq), vt.dtype)
    km = kmean[...]
    km_hi = km.astype(jnp.bfloat16)
    km_hilo = jnp.concatenate([km_hi, (km - km_hi.astype(jnp.float32)).astype(jnp.bfloat16)], axis=0)
    blk_id = lax.broadcasted_iota(jnp.int32, (nb, nq2), 0)
    key_id = lax.broadcasted_iota(jnp.int32, (chunk, nq), 0)
    col_id = lax.broadcasted_iota(jnp.int32, (chunk, nq), 1)
    causal_s[...] = jnp.where(key_id <= col_id % MOBA_BLOCK + (col_id // nq2) * MOBA_BLOCK, 0.0, NEG)

    def chunk_rows(ch):
        return pl.ds(pl.multiple_of(ch * chunk, chunk), chunk)

    def scores_t(t):
        q2 = q2_all[pl.ds(pl.multiple_of(tbl(_T_TILE, t) * nq, nq), nq), :]
        return lax.dot_general(k_ref[chunk_rows(tbl(_T_CH, t)), :], q2, _NT,
                               preferred_element_type=jnp.float32)

    def pv_t(t, p):
        p = p.astype(vt.dtype)
        rows = chunk_rows(tbl(_T_CH, t))
        return jnp.concatenate(
            [jnp.dot(vt[g % PAIR, :, rows], p[:, g * MOBA_BLOCK:(g + 1) * MOBA_BLOCK],
                     preferred_element_type=jnp.float32) for g in range(nq // MOBA_BLOCK)], axis=1)

    def block_bias(t):
        row0 = tbl(_T_TILE, t) * nb + tbl(_T_CH, t) * MOBA_CHUNK_BLOCKS
        return [bias_s[pl.ds(row0 + i, 1), :] for i in range(MOBA_CHUNK_BLOCKS)]

    def per_block(x, fn):
        return [fn(x[i * MOBA_BLOCK:(i + 1) * MOBA_BLOCK], i) for i in range(MOBA_CHUNK_BLOCKS)]

    def select_blocks(tile, carry):
        for i in range(MOBA_CHUNK_BLOCKS):
            qb = tile * MOBA_CHUNK_BLOCKS + i
            q2 = q2_all[pl.ds(pl.multiple_of(qb * nq2, nq2), nq2), :]
            g2 = lax.dot_general(km_hilo, q2, _NT, preferred_element_type=jnp.float32)
            past = blk_id < qb
            gate = jnp.where(past, g2[0:nb] + g2[nb:2 * nb], NEG)
            rank = jnp.zeros((nb, nq2), jnp.int32)
            for j in range(nb - 1):
                gj = gate[j:j + 1, :]
                tie_ref = jnp.where(blk_id > j, gate, NO_TIE)
                rank = rank + jnp.where(gj > gate, 1, 0) + jnp.where(gj == tie_ref, 1, 0)
            allowed = (past & (rank < MOBA_TOPK)) | (blk_id == qb)
            bias_s[pl.ds(pl.multiple_of(tile * nb, nb), nb), i * nq2:(i + 1) * nq2] = jnp.where(allowed, 0.0, NEG)
        return carry

    lax.fori_loop(0, n_tiles, select_blocks, 0, unroll=2)

    def attend_first(t, s_cur):
        bias = block_bias(t)
        s = s_cur[...] + causal_s[...]
        s = jnp.concatenate(per_block(s, lambda x, i: x + bias[i]), axis=0)
        m0 = jnp.max(s, axis=0, keepdims=True)
        p = jnp.exp2(s - m0)
        m_s[...] = m0
        acc_s[...] = pv_t(t, p)

    def attend_next(t, s_cur):
        s = s_cur[...]
        bias = block_bias(t)
        m_old = m_s[...]
        m_blk = per_block(s, lambda x, i: jnp.max(x, axis=0, keepdims=True) + bias[i])
        m_new = functools.reduce(jnp.maximum, m_blk, m_old)
        alpha = jnp.exp2(m_old - m_new)
        p = jnp.concatenate(per_block(s, lambda x, i: jnp.exp2(x - (m_new - bias[i]))), axis=0)
        m_s[...] = m_new
        acc_s[...] = alpha * acc_s[...] + pv_t(t, p)

    def finalize(tile):
        o_all = acc_s[0:HEAD_DIM, :] * (1.0 / acc_s[HEAD_DIM:HEAD_DIM + 1, :])
        for i in range(MOBA_CHUNK_BLOCKS):
            c0 = i * nq2
            o_t = jnp.concatenate([o_all[:, c0:c0 + MOBA_BLOCK], o_all[:, c0 + MOBA_BLOCK:c0 + nq2]],
                                  axis=0)
            rows = pl.ds(pl.multiple_of((tile * MOBA_CHUNK_BLOCKS + i) * MOBA_BLOCK, MOBA_BLOCK), MOBA_BLOCK)
            o_ref[rows, :] = o_t.T.astype(o_ref.dtype)

    def step(t, s_cur, s_nxt):
        def first():
            s_nxt[...] = scores_t(t + 1)
            attend_first(t, s_cur)

        def later():
            s_nxt[...] = scores_t(t + 1)
            attend_next(t, s_cur)

        lax.cond(tbl(_T_FIRST, t) == 1, first, later)

        @pl.when(tbl(_T_LAST, t) == 1)
        def _():
            finalize(tbl(_T_TILE, t))

    s_a[...] = scores_t(0)

    def two_steps(i, carry):
        step(2 * i, s_a, s_b)
        step(2 * i + 1, s_b, s_a)
        return carry

    lax.fori_loop(0, n_steps // 2, two_steps, 0)


def _attn_b(proj, batch, seq):
    n_pairs = B_WIDTH // LANES
    tile0 = 4 * (A_WIDTH // LANES)

    def col_spec(group):
        return pl.BlockSpec((seq, LANES), lambda b, p, group=group: (b, tile0 + group * n_pairs + p))

    nb = seq // MOBA_BLOCK
    nq2 = PAIR * MOBA_BLOCK
    chunk = MOBA_CHUNK_BLOCKS * MOBA_BLOCK
    assert nb % MOBA_CHUNK_BLOCKS == 0
    n_tiles = nb // MOBA_CHUNK_BLOCKS
    nq = MOBA_CHUNK_BLOCKS * nq2
    n_steps, table = _moba_step_table(n_tiles)
    assert n_steps % 2 == 0
    f32 = jnp.float32
    bf16 = jnp.bfloat16
    return pl.pallas_call(
        functools.partial(_moba_kernel, n_steps=n_steps),
        grid=(batch, n_pairs),
        in_specs=[pl.BlockSpec(memory_space=pltpu.SMEM), col_spec(0), col_spec(1), col_spec(2)],
        out_specs=pl.BlockSpec((seq, LANES), lambda b, p: (b, p)),
        out_shape=jax.ShapeDtypeStruct((batch * seq, B_WIDTH), bf16),
        scratch_shapes=[
            pltpu.VMEM((PAIR, HEAD_DIM + SUM_ROWS, seq), bf16),
            pltpu.VMEM((nb * nq2, LANES), bf16),
            pltpu.VMEM((nb, LANES), f32),
            pltpu.VMEM((n_tiles * nb, nq), f32),
            pltpu.VMEM((chunk, nq), f32),
            pltpu.VMEM((1, nq), f32),
            pltpu.VMEM((HEAD_DIM + SUM_ROWS, nq), f32),
            pltpu.VMEM((chunk, nq), f32), pltpu.VMEM((chunk, nq), f32),
        ],
        compiler_params=pltpu.CompilerParams(
            dimension_semantics=("parallel", "parallel"), vmem_limit_bytes=VMEM_LIMIT),
        name="attn_moba",
    )(jnp.asarray(table, jnp.int32), proj, proj, proj)


def _rope_tables(seq):
    pos = jnp.arange(seq, dtype=jnp.float32)
    inv_freq = ROPE_THETA ** (-jnp.arange(0, HEAD_DIM, 2, dtype=jnp.float32) / HEAD_DIM)
    ang = pos[:, None] * inv_freq[None, :]
    cos = jnp.concatenate([jnp.cos(ang)] * (2 * PAIR), axis=-1)
    sin = jnp.concatenate([-jnp.sin(ang)] * PAIR + [jnp.sin(ang)] * PAIR, axis=-1)
    return cos, sin


def _tile_gain(g, width):
    return _qk_columns(jnp.tile(g, width // HEAD_DIM)[None, :])[0]


def _qk_columns(w):
    lead = w.shape[:-1]
    tiles = w.reshape(lead + (w.shape[-1] // LANES, PAIR, 2, HALF_DIM))
    return jnp.swapaxes(tiles, -3, -2).reshape(w.shape)


def kernel(x, norm_even, w_in_even, w_out_even, qnorm_a, knorm_a, qnorm_b, knorm_b,
           norm_odd, w_in_odd, w_out_odd, qnorm_c, knorm_c, sinks_c):
    batch, seq, d = x.shape
    bf16 = jnp.bfloat16
    scale = HEAD_DIM ** -0.5 * LOG2E
    cos, sin = _rope_tables(seq)
    x2 = x.reshape(batch * seq, d)

    ones = jnp.ones((A_WIDTH,), jnp.float32)
    gain0 = jnp.concatenate([
        _tile_gain(qnorm_a[0], A_WIDTH) * scale, _tile_gain(knorm_a[0], A_WIDTH), ones, ones,
        _tile_gain(qnorm_b[0], B_WIDTH) * scale, _tile_gain(knorm_b[0], B_WIDTH), ones, ones])[None, :]
    tiles_per_group = A_WIDTH // LANES
    rope0 = frozenset(t for grp in (0, 1, 4, 5) for t in range(grp * tiles_per_group, (grp + 1) * tiles_per_group))
    w0 = w_in_even[0]
    groups = [w0[:, g * A_WIDTH:(g + 1) * A_WIDTH] for g in range(8)]
    w0r = jnp.concatenate([_qk_columns(w) if g in (0, 1, 4, 5) else w for g, w in enumerate(groups)], axis=1)
    proj0 = _inproj(x2, norm_even[0][None, :], w0r.astype(bf16), gain0, cos, sin, rope0, seq)
    oa = _attn_a(proj0, batch, seq)
    ob = _attn_b(proj0, batch, seq)

    w1 = w_in_odd[0]
    kv_w = C_KV_HEADS * HEAD_DIM
    wq, wk, wv, wz = (w1[:, :C_WIDTH], w1[:, C_WIDTH:C_WIDTH + kv_w],
                      w1[:, C_WIDTH + kv_w:C_WIDTH + 2 * kv_w], w1[:, C_WIDTH + 2 * kv_w:])

    def dup_heads(w):
        return jnp.concatenate([w[:, h * HEAD_DIM:(h + 1) * HEAD_DIM]
                                for h in range(C_KV_HEADS) for _ in range(PAIR)], axis=1)

    w1r = jnp.concatenate([_qk_columns(wq), _qk_columns(dup_heads(wk)), dup_heads(wv), wz],
                          axis=1).astype(bf16)
    dup_w = PAIR * kv_w
    gain1 = jnp.concatenate([
        _tile_gain(qnorm_c[0], C_WIDTH) * scale, _tile_gain(knorm_c[0], dup_w),
        jnp.ones((dup_w + C_WIDTH,), jnp.float32)])[None, :]
    rope1 = frozenset(range((C_WIDTH + dup_w) // LANES))
    x1, proj1 = _outproj_inproj(x2, [(oa, 0), (ob, 0)], [(proj0, 3), (proj0, 7)], w_out_even[0].astype(bf16),
                                A_WIDTH, norm_odd[0][None, :], w1r, gain1, cos, sin, rope1, seq)
    oc = _attn_c(proj1, sinks_c[0], batch, seq)
    half = C_WIDTH // 2
    z_blk = (C_WIDTH + 2 * dup_w) // half
    out = _outproj(x1, [(oc, 0), (oc, 1)], [(proj1, z_blk), (proj1, z_blk + 1)],
                   w_out_odd[0].astype(bf16), half)
    return out.reshape(batch, seq, d)
```

```python
import functools

import jax
import jax.numpy as jnp
from jax import lax
from jax.experimental import pallas as pl
from jax.experimental.pallas import tpu as pltpu

D_MODEL = 1024
HEAD_DIM = 64
LANES = 128
PAIR = LANES // HEAD_DIM
HALF_DIM = HEAD_DIM // 2
NORM_COLS = 2 * LANES
A_WIDTH = 512
B_WIDTH = 512
C_WIDTH = 1024
C_KV_HEADS = 2
C_GROUP = 8
DILATIONS = (16, 4, 1)
BAND = 128
MOBA_BLOCK = 256
MOBA_TOPK = 3
MOBA_CHUNK_BLOCKS = 2
DILATED_UNROLL = 8
SWA_UNROLL = 16
PREP_UNROLL = 4
SUM_ROWS = 16
LOG2E = 1.4426950408889634
ROPE_THETA = 10000.0
NORM_EPS = 1e-6
NEG = -1e30
NO_TIE = 3e38
VMEM_LIMIT = 56 * 1024 * 1024
PROJ_ROWS = 512
PROJ_COLS = 512

_NT = (((1,), (1,)), ((), ()))


def _lane_ids(shape):
    return lax.broadcasted_iota(jnp.int32, shape, len(shape) - 1)


def _qk_head_of_lane(lane):
    return (lane // HALF_DIM) % PAIR


def _qk_lane_order():
    return [h * HEAD_DIM + half * HALF_DIM + d for half in range(2) for h in range(PAIR) for d in range(HALF_DIM)]


def _split_heads_rows(t):
    head = _qk_head_of_lane(_lane_ids(t.shape))
    zero = jnp.zeros_like(t)
    return jnp.concatenate([jnp.where(head == 0, t, zero), jnp.where(head == 1, t, zero)], axis=0)


def _rms_rows(x, g_ref):
    ms = jnp.mean(x * x, axis=-1, keepdims=True)
    return (x * lax.rsqrt(ms + NORM_EPS) * g_ref[...]).astype(jnp.bfloat16)


def _project_columns(h, w_ref, gain_ref, cos_ref, sin_ref, o_ref, rope_tiles):
    n_cols = o_ref.shape[1]
    li = lax.broadcasted_iota(jnp.int32, (NORM_COLS, NORM_COLS), 0)
    lj = lax.broadcasted_iota(jnp.int32, (NORM_COLS, NORM_COLS), 1)
    same_head = ((li // LANES == lj // LANES)
                 & (_qk_head_of_lane(li) == _qk_head_of_lane(lj))).astype(jnp.bfloat16)
    for c0 in range(0, n_cols, PROJ_COLS):
        r = jnp.dot(h, w_ref[:, c0:c0 + PROJ_COLS], preferred_element_type=jnp.float32)
        for p0 in range(0, PROJ_COLS, NORM_COLS):
            col = c0 + p0
            tiles = range(col // LANES, (col + NORM_COLS) // LANES)
            t = r[:, p0:p0 + NORM_COLS]
            if all(tile in rope_tiles for tile in tiles):
                ss = jnp.dot((t * t).astype(jnp.bfloat16), same_head, preferred_element_type=jnp.float32)
                t = t * lax.rsqrt(ss * (1.0 / HEAD_DIM) + NORM_EPS) * gain_ref[:, col:col + NORM_COLS]
                for t0 in range(0, NORM_COLS, LANES):
                    tn = t[:, t0:t0 + LANES]
                    tn = tn * cos_ref[...] + pltpu.roll(tn, LANES // 2, 1) * sin_ref[...]
                    o_ref[:, col + t0:col + t0 + LANES] = tn.astype(o_ref.dtype)
            else:
                assert not any(tile in rope_tiles for tile in tiles)
                o_ref[:, col:col + NORM_COLS] = t.astype(o_ref.dtype)


def _gated_residual(x, y_refs, z_refs, w_ref):
    acc = x
    k0 = 0
    for y_ref, z_ref in zip(y_refs, z_refs):
        z = z_ref[...].astype(jnp.float32)
        gated = y_ref[...].astype(jnp.float32) * (z / (1.0 + jnp.exp(-z)))
        kw = y_ref.shape[1]
        acc = acc + jnp.dot(gated.astype(jnp.bfloat16), w_ref[k0:k0 + kw, :],
                            preferred_element_type=jnp.float32)
        k0 += kw
    return acc


def _inproj_kernel(x_ref, g_ref, w_ref, gain_ref, cos_ref, sin_ref, o_ref, *, rope_tiles):
    _project_columns(_rms_rows(x_ref[...], g_ref), w_ref, gain_ref, cos_ref, sin_ref, o_ref, rope_tiles)


def _outproj_kernel(*refs, n_parts):
    x_ref, y_refs, z_refs = refs[0], refs[1:1 + n_parts], refs[1 + n_parts:1 + 2 * n_parts]
    w_ref, o_ref = refs[1 + 2 * n_parts:]
    o_ref[...] = _gated_residual(x_ref[...], y_refs, z_refs, w_ref)


def _out_in_kernel(*refs, n_parts, rope_tiles):
    x_ref, y_refs, z_refs = refs[0], refs[1:1 + n_parts], refs[1 + n_parts:1 + 2 * n_parts]
    w_out_ref, g_ref, w_in_ref, gain_ref, cos_ref, sin_ref, x_out_ref, proj_ref = refs[1 + 2 * n_parts:]
    x_new = _gated_residual(x_ref[...], y_refs, z_refs, w_out_ref)
    x_out_ref[...] = x_new
    _project_columns(_rms_rows(x_new, g_ref), w_in_ref, gain_ref, cos_ref, sin_ref, proj_ref, rope_tiles)


def _row_spec(width, blk=0):
    return pl.BlockSpec((PROJ_ROWS, width), lambda i, blk=blk: (i, blk))


def _whole_spec(a):
    return pl.BlockSpec(a.shape, lambda i: (0,) * a.ndim)


def _proj_params():
    return pltpu.CompilerParams(dimension_semantics=("parallel",), vmem_limit_bytes=VMEM_LIMIT)


def _inproj_operands(g, w, gain, cos, sin, seq):
    seq_tiles = seq // PROJ_ROWS
    rope_spec = pl.BlockSpec((PROJ_ROWS, LANES), lambda i: (i % seq_tiles, 0))
    return [g, w, gain, cos, sin], [_whole_spec(g), _whole_spec(w), _whole_spec(gain), rope_spec, rope_spec]


def _gated_operands(ys, zs, w, part_width):
    arrays = [a for a, _ in list(ys) + list(zs)] + [w]
    specs = [_row_spec(part_width, blk) for _, blk in list(ys) + list(zs)] + [_whole_spec(w)]
    return arrays, specs


def _inproj(x2, g, w, gain, cos, sin, rope_tiles, seq):
    rows, d = x2.shape
    arrays, specs = _inproj_operands(g, w, gain, cos, sin, seq)
    return pl.pallas_call(
        functools.partial(_inproj_kernel, rope_tiles=rope_tiles),
        grid=(rows // PROJ_ROWS,),
        in_specs=[_row_spec(d)] + specs,
        out_specs=_row_spec(w.shape[1]),
        out_shape=jax.ShapeDtypeStruct((rows, w.shape[1]), jnp.bfloat16),
        compiler_params=_proj_params(),
        name="inproj",
    )(x2, *arrays)


def _outproj(x2, ys, zs, w, part_width):
    rows, d = x2.shape
    arrays, specs = _gated_operands(ys, zs, w, part_width)
    return pl.pallas_call(
        functools.partial(_outproj_kernel, n_parts=len(ys)),
        grid=(rows // PROJ_ROWS,),
        in_specs=[_row_spec(d)] + specs,
        out_specs=_row_spec(d),
        out_shape=jax.ShapeDtypeStruct((rows, d), jnp.float32),
        compiler_params=_proj_params(),
        name="outproj",
    )(x2, *arrays)


def _outproj_inproj(x2, ys, zs, w_out, part_width, g, w_in, gain, cos, sin, rope_tiles, seq):
    rows, d = x2.shape
    out_arrays, out_specs = _gated_operands(ys, zs, w_out, part_width)
    in_arrays, in_specs = _inproj_operands(g, w_in, gain, cos, sin, seq)
    return pl.pallas_call(
        functools.partial(_out_in_kernel, n_parts=len(ys), rope_tiles=rope_tiles),
        grid=(rows // PROJ_ROWS,),
        in_specs=[_row_spec(d)] + out_specs + in_specs,
        out_specs=[_row_spec(d), _row_spec(w_in.shape[1])],
        out_shape=[jax.ShapeDtypeStruct((rows, d), jnp.float32),
                   jax.ShapeDtypeStruct((rows, w_in.shape[1]), jnp.bfloat16)],
        compiler_params=_proj_params(),
        name="outproj_inproj",
    )(x2, *out_arrays, *in_arrays)


def _fill_band_bias(bias_ref, first_key):
    kj = lax.broadcasted_iota(jnp.int32, (2 * BAND, PAIR * BAND), 0)
    qi = lax.broadcasted_iota(jnp.int32, (2 * BAND, PAIR * BAND), 1) % BAND
    band = (kj >= qi + first_key) & (kj <= qi + BAND)
    bias_ref[0] = jnp.where(band, 0.0, NEG)
    bias_ref[1] = jnp.where(band & (kj >= BAND), 0.0, NEG)


def _band_blocks(operands, sink_row):
    scores = [lax.dot_general(kw, _split_heads_rows(q), _NT, preferred_element_type=jnp.float32) + bias
              for q, kw, _, bias in operands]
    probs = []
    for s in scores:
        m = jnp.max(s, axis=0, keepdims=True)
        if sink_row is not None:
            m = jnp.maximum(m, sink_row)
        probs.append((jnp.exp2(s - m).astype(jnp.bfloat16), m))
    return [(jnp.dot(vtw, p, preferred_element_type=jnp.float32), m)
            for (_, _, vtw, _), (p, m) in zip(operands, probs)]


def _pair_tile(x):
    return jnp.concatenate([x[0:HEAD_DIM, 0:BAND], x[HEAD_DIM:LANES, BAND:PAIR * BAND]], axis=0)


def _pair_stat_tile(v):
    return jnp.concatenate([jnp.broadcast_to(v[:, 0:BAND], (HEAD_DIM, BAND)),
                            jnp.broadcast_to(v[:, BAND:PAIR * BAND], (HEAD_DIM, BAND))], axis=0)


def _attn_a_kernel(q_ref, k_ref, v_ref, o_ref, qf, kf, vf, qf4, kf4, vf4, qd, kd, vdt, bias_s, m_s, l_s, acc_s):
    seq = q_ref.shape[0]
    n_blocks = seq // BAND
    piece = 2 * BAND
    mid = 4
    qf[...] = q_ref[...].astype(jnp.float32)
    kf[...] = k_ref[...].astype(jnp.float32)
    vf[...] = v_ref[...].astype(jnp.float32)
    for r in range(mid):
        dst = pl.ds(r * (seq // mid), seq // mid)
        src = pl.ds(r, seq // mid, stride=mid)
        qf4[dst, :] = qf[src, :]
        kf4[dst, :] = kf[src, :]
        vf4[dst, :] = vf[src, :]
    kd[0:BAND, :] = jnp.zeros((BAND, LANES), kd.dtype)
    vdt[:, 0:BAND] = jnp.zeros((LANES + SUM_ROWS, BAND), vdt.dtype)
    vdt[LANES:, BAND:] = jnp.ones((SUM_ROWS, seq), vdt.dtype)
    _fill_band_bias(bias_s, 0)

    for dil in DILATIONS:
        sub = seq // dil
        blocks_per_sub = sub // BAND
        pieces_per_sub = sub // piece

        def deinterleave(i, carry, dil=dil, pieces_per_sub=pieces_per_sub):
            dst = pl.multiple_of(i * piece, piece)
            if dil == 1:
                src = pl.ds(dst, piece)
                q_src, k_src, v_src = q_ref, k_ref, vf
            elif dil == mid:
                src = pl.ds(dst, piece)
                q_src, k_src, v_src = qf4, kf4, vf4
            else:
                r = i // pieces_per_sub
                j = i % pieces_per_sub
                step = dil // mid
                src = pl.ds((r % mid) * (seq // mid) + r // mid + j * (piece * step), piece, stride=step)
                q_src, k_src, v_src = qf4, kf4, vf4
            qd[pl.ds(dst, piece), :] = q_src[src, :].astype(qd.dtype)
            kd[pl.ds(BAND + dst, piece), :] = k_src[src, :].astype(kd.dtype)
            vdt[0:LANES, pl.ds(BAND + dst, piece)] = v_src[src, :].astype(jnp.float32).T.astype(vdt.dtype)
            return carry

        lax.fori_loop(0, seq // piece, deinterleave, 0, unroll=PREP_UNROLL)

        def body(it, carry, dil=dil, blocks_per_sub=blocks_per_sub):
            blocks = [it * DILATED_UNROLL + u for u in range(DILATED_UNROLL)]
            operands = []
            for n in blocks:
                row0 = pl.multiple_of(n * BAND, BAND)
                first = (n % blocks_per_sub) == 0
                operands.append((qd[pl.ds(row0, BAND), :], kd[pl.ds(row0, 2 * BAND), :],
                                 vdt[:, pl.ds(row0, 2 * BAND)], bias_s[jnp.where(first, 1, 0)]))
            for n, (o_t, m) in zip(blocks, _band_blocks(operands, None)):
                merge_block(n, o_t, m, dil, blocks_per_sub)
            return carry

        def merge_block(n, o_t, m, dil, blocks_per_sub):
            r = n // blocks_per_sub
            c = n % blocks_per_sub
            row0 = pl.multiple_of(n * BAND, BAND)
            acc = _pair_tile(o_t).T
            m_b = _pair_stat_tile(m).T
            l_b = _pair_stat_tile(o_t[LANES:LANES + 1, :]).T
            rows = pl.ds(row0, BAND) if dil == 1 else pl.ds(c * (BAND * dil) + r, BAND, stride=dil)
            if dil == DILATIONS[0]:
                m_s[rows, :] = m_b
                l_s[rows, :] = l_b
                acc_s[rows, :] = acc
            else:
                m_old = m_s[rows, :]
                m_new = jnp.maximum(m_old, m_b)
                a_old = jnp.exp2(m_old - m_new)
                a_new = jnp.exp2(m_b - m_new)
                m_s[rows, :] = m_new
                l_s[rows, :] = a_old * l_s[rows, :] + a_new * l_b
                acc_s[rows, :] = a_old * acc_s[rows, :] + a_new * acc

        lax.fori_loop(0, n_blocks // DILATED_UNROLL, body, 0)

    o_ref[...] = (acc_s[...] / l_s[...]).astype(o_ref.dtype)


def _attn_a(proj, batch, seq):
    n_pairs = A_WIDTH // LANES
    blocks_per_batch = 1

    def col_spec(group):
        return pl.BlockSpec((seq, LANES), lambda b, p, group=group: (b, group * n_pairs + p))

    f32 = jnp.float32
    bf16 = jnp.bfloat16
    del blocks_per_batch
    return pl.pallas_call(
        _attn_a_kernel,
        grid=(batch, n_pairs),
        in_specs=[col_spec(0), col_spec(1), col_spec(2)],
        out_specs=pl.BlockSpec((seq, LANES), lambda b, p: (b, p)),
        out_shape=jax.ShapeDtypeStruct((batch * seq, A_WIDTH), bf16),
        scratch_shapes=[
            pltpu.VMEM((seq, LANES), f32), pltpu.VMEM((seq, LANES), f32), pltpu.VMEM((seq, LANES), f32),
            pltpu.VMEM((seq, LANES), f32), pltpu.VMEM((seq, LANES), f32), pltpu.VMEM((seq, LANES), f32),
            pltpu.VMEM((seq, LANES), bf16),
            pltpu.VMEM((seq + BAND, LANES), bf16),
            pltpu.VMEM((LANES + SUM_ROWS, seq + BAND), bf16),
            pltpu.VMEM((2, 2 * BAND, PAIR * BAND), f32),
            pltpu.VMEM((seq, LANES), f32), pltpu.VMEM((seq, LANES), f32), pltpu.VMEM((seq, LANES), f32),
        ],
        compiler_params=pltpu.CompilerParams(
            dimension_semantics=("parallel", "parallel"), vmem_limit_bytes=VMEM_LIMIT),
        name="attn_dilated",
    )(proj, proj, proj)


def _attn_c_kernel(sinks_ref, q_ref, k_ref, v_ref, o_ref, kd, vdt, bias_s):
    seq = q_ref.shape[0]
    pairs_per_kv = q_ref.shape[1] // LANES
    piece = 2 * BAND
    kd[0:BAND, :] = jnp.zeros((BAND, LANES), kd.dtype)
    kd[BAND:, :] = k_ref[...]
    vdt[:, 0:BAND] = jnp.zeros((LANES + SUM_ROWS, BAND), vdt.dtype)
    vdt[LANES:, BAND:] = jnp.ones((SUM_ROWS, seq), vdt.dtype)

    def transpose_v(i, carry):
        rows = pl.ds(pl.multiple_of(i * piece, piece), piece)
        vdt[0:LANES, pl.ds(pl.multiple_of(BAND + i * piece, BAND), piece)] = (
            v_ref[rows, :].astype(jnp.float32).T.astype(vdt.dtype))
        return carry

    lax.fori_loop(0, seq // piece, transpose_v, 0, unroll=PREP_UNROLL)
    _fill_band_bias(bias_s, 1)
    col = lax.broadcasted_iota(jnp.int32, (1, PAIR * BAND), 1)
    iters_per_pair = seq // (BAND * SWA_UNROLL)

    def body(step, carry):
        pair = step // iters_per_pair
        it = step % iters_per_pair
        lanes = pl.ds(pl.multiple_of(pair * LANES, LANES), LANES)
        head0 = PAIR * (pl.program_id(1) * pairs_per_kv + pair)
        sink_row = jnp.where(col < BAND, sinks_ref[head0], sinks_ref[head0 + 1]) * LOG2E
        blocks = [it * SWA_UNROLL + u for u in range(SWA_UNROLL)]
        operands = []
        for n in blocks:
            row0 = pl.multiple_of(n * BAND, BAND)
            operands.append((q_ref[pl.ds(row0, BAND), lanes], kd[pl.ds(row0, 2 * BAND), :],
                             vdt[:, pl.ds(row0, 2 * BAND)], bias_s[jnp.where(n == 0, 1, 0)]))
        for n, (o_t, m) in zip(blocks, _band_blocks(operands, sink_row)):
            l = o_t[LANES:LANES + 1, :] + jnp.exp2(sink_row - m)
            out_t = _pair_tile(o_t) * _pair_stat_tile(1.0 / l)
            o_ref[pl.ds(pl.multiple_of(n * BAND, BAND), BAND), lanes] = out_t.T.astype(o_ref.dtype)
        return carry

    lax.fori_loop(0, pairs_per_kv * iters_per_pair, body, 0)


def _attn_c(proj, sinks, batch, seq):
    group_w = C_GROUP * HEAD_DIM
    k_tile0 = C_WIDTH // LANES
    v_tile0 = k_tile0 + C_KV_HEADS
    bf16 = jnp.bfloat16
    return pl.pallas_call(
        _attn_c_kernel,
        grid=(batch, C_KV_HEADS),
        in_specs=[
            pl.BlockSpec(memory_space=pltpu.SMEM),
            pl.BlockSpec((seq, group_w), lambda b, g: (b, g)),
            pl.BlockSpec((seq, LANES), lambda b, g: (b, k_tile0 + g)),
            pl.BlockSpec((seq, LANES), lambda b, g: (b, v_tile0 + g)),
        ],
        out_specs=pl.BlockSpec((seq, group_w), lambda b, g: (b, g)),
        out_shape=jax.ShapeDtypeStruct((batch * seq, C_WIDTH), bf16),
        scratch_shapes=[
            pltpu.VMEM((seq + BAND, LANES), bf16),
            pltpu.VMEM((LANES + SUM_ROWS, seq + BAND), bf16),
            pltpu.VMEM((2, 2 * BAND, PAIR * BAND), jnp.float32),
        ],
        compiler_params=pltpu.CompilerParams(
            dimension_semantics=("parallel", "parallel"), vmem_limit_bytes=VMEM_LIMIT),
        name="attn_swa",
    )(sinks, proj, proj, proj)


_T_TILE, _T_CH, _T_FIRST, _T_ROWS = range(4)


def _moba_step_table(n_tiles):
    tiles, chs, firsts = [], [], []
    for tile in range(n_tiles):
        for i, ch in enumerate([tile] + list(range(tile))):
            tiles.append(tile)
            chs.append(ch)
            firsts.append(int(i == 0))
    n_steps = len(tiles)
    rows = [None] * _T_ROWS
    rows[_T_TILE] = tiles + [tiles[-1]]
    rows[_T_CH] = chs + [chs[-1]]
    rows[_T_FIRST] = firsts + [0]
    return n_steps, [v for row in rows for v in row]


def _moba_kernel(tbl_ref, q_ref, k_ref, v_ref, o_ref, vt, q2_all, kmean, bias_s, causal_s, m_s, acc_s,
                 s_a, s_b, *, n_steps):
    seq = q_ref.shape[0]
    nb = seq // MOBA_BLOCK
    nq2 = PAIR * MOBA_BLOCK
    nq = MOBA_CHUNK_BLOCKS * nq2
    n_tiles = nb // MOBA_CHUNK_BLOCKS
    chunk = MOBA_CHUNK_BLOCKS * MOBA_BLOCK

    def tbl(row, t):
        return tbl_ref[row * (n_steps + 1) + t]

    def prep(j, carry):
        rows = pl.ds(pl.multiple_of(j * MOBA_BLOCK, MOBA_BLOCK), MOBA_BLOCK)
        v_t = v_ref[rows, :].astype(jnp.float32).T.astype(vt.dtype)
        for h in range(PAIR):
            vt[h, 0:HEAD_DIM, rows] = v_t[h * HEAD_DIM:(h + 1) * HEAD_DIM]
        kmean[pl.ds(j, 1), :] = jnp.mean(k_ref[rows, :].astype(jnp.float32), axis=0, keepdims=True)
        q2_all[pl.ds(pl.multiple_of(j * nq2, nq2), nq2), :] = _split_heads_rows(q_ref[rows, :])
        return carry

    lax.fori_loop(0, nb, prep, 0, unroll=PREP_UNROLL)
    for h in range(PAIR):
        vt[h, HEAD_DIM:, :] = jnp.ones((SUM_ROWS, seq), vt.dtype)
    km = kmean[...]
    km_hi = km.astype(jnp.bfloat16)
    km_hilo = jnp.concatenate([km_hi, (km - km_hi.astype(jnp.float32)).astype(jnp.bfloat16)], axis=0)
    blk_id = lax.broadcasted_iota(jnp.int32, (nb, nq2), 0)
    key_id = lax.broadcasted_iota(jnp.int32, (chunk, nq), 0)
    col_id = lax.broadcasted_iota(jnp.int32, (chunk, nq), 1)
    causal_s[...] = jnp.where(key_id <= col_id % MOBA_BLOCK + (col_id // nq2) * MOBA_BLOCK, 0.0, NEG)

    def chunk_rows(ch):
        return pl.ds(pl.multiple_of(ch * chunk, chunk), chunk)

    def scores_t(t):
        q2 = q2_all[pl.ds(pl.multiple_of(tbl(_T_TILE, t) * nq, nq), nq), :]
        return lax.dot_general(k_ref[chunk_rows(tbl(_T_CH, t)), :], q2, _NT,
                               preferred_element_type=jnp.float32)

    def pv_t(t, p):
        p = p.astype(vt.dtype)
        rows = chunk_rows(tbl(_T_CH, t))
        return jnp.concatenate(
            [jnp.dot(vt[g % PAIR, :, rows], p[:, g * MOBA_BLOCK:(g + 1) * MOBA_BLOCK],
                     preferred_element_type=jnp.float32) for g in range(nq // MOBA_BLOCK)], axis=1)

    def block_bias(t):
        row0 = tbl(_T_TILE, t) * nb + tbl(_T_CH, t) * MOBA_CHUNK_BLOCKS
        return [bias_s[pl.ds(row0 + i, 1), :] for i in range(MOBA_CHUNK_BLOCKS)]

    def per_block(x, fn):
        return [fn(x[i * MOBA_BLOCK:(i + 1) * MOBA_BLOCK], i) for i in range(MOBA_CHUNK_BLOCKS)]

    def select_blocks(tile, carry):
        for i in range(MOBA_CHUNK_BLOCKS):
            qb = tile * MOBA_CHUNK_BLOCKS + i
            q2 = q2_all[pl.ds(pl.multiple_of(qb * nq2, nq2), nq2), :]
            g2 = lax.dot_general(km_hilo, q2, _NT, preferred_element_type=jnp.float32)
            past = blk_id < qb
            gate = jnp.where(past, g2[0:nb] + g2[nb:2 * nb], NEG)
            rank = jnp.zeros((nb, nq2), jnp.int32)
            for j in range(nb - 1):
                gj = gate[j:j + 1, :]
                tie_ref = jnp.where(blk_id > j, gate, NO_TIE)
                rank = rank + jnp.where(gj > gate, 1, 0) + jnp.where(gj == tie_ref, 1, 0)
            allowed = (past & (rank < MOBA_TOPK)) | (blk_id == qb)
            bias_s[pl.ds(pl.multiple_of(tile * nb, nb), nb), i * nq2:(i + 1) * nq2] = jnp.where(allowed, 0.0, NEG)
        return carry

    lax.fori_loop(0, n_tiles, select_blocks, 0, unroll=2)

    def attend_first(t, s_cur):
        bias = block_bias(t)

        def left_pad(x, i, fill):
            return x if i == 0 else jnp.concatenate([jnp.full((x.shape[0], i * nq2), fill, x.dtype), x], axis=1)

        parts = []
        for i in range(MOBA_CHUNK_BLOCKS):
            rows, cols = slice(i * MOBA_BLOCK, (i + 1) * MOBA_BLOCK), slice(i * nq2, nq)
            parts.append(s_cur[rows, cols] + causal_s[rows, cols] + bias[i][:, cols])
        m0 = functools.reduce(jnp.maximum, [left_pad(jnp.max(x, axis=0, keepdims=True), i, NEG)
                                            for i, x in enumerate(parts)])
        p = jnp.concatenate([left_pad(jnp.exp2(x - m0[:, i * nq2:]), i, 0.0) for i, x in enumerate(parts)], axis=0)
        m_s[...] = m0
        acc_s[...] = pv_t(t, p)

    def attend_next(t, s_cur):
        s = s_cur[...]
        bias = block_bias(t)
        m_old = m_s[...]
        m_blk = per_block(s, lambda x, i: jnp.max(x, axis=0, keepdims=True) + bias[i])
        m_new = functools.reduce(jnp.maximum, m_blk, m_old)
        alpha = jnp.exp2(m_old - m_new)
        p = jnp.concatenate(per_block(s, lambda x, i: jnp.exp2(x - (m_new - bias[i]))), axis=0)
        m_s[...] = m_new
        acc_s[...] = alpha * acc_s[...] + pv_t(t, p)

    def finalize(tile):
        o_all = acc_s[0:HEAD_DIM, :] * (1.0 / acc_s[HEAD_DIM:HEAD_DIM + 1, :])
        for i in range(MOBA_CHUNK_BLOCKS):
            c0 = i * nq2
            o_t = jnp.concatenate([o_all[:, c0:c0 + MOBA_BLOCK], o_all[:, c0 + MOBA_BLOCK:c0 + nq2]],
                                  axis=0)
            rows = pl.ds(pl.multiple_of((tile * MOBA_CHUNK_BLOCKS + i) * MOBA_BLOCK, MOBA_BLOCK), MOBA_BLOCK)
            o_ref[rows, :] = o_t.T.astype(o_ref.dtype)

    def step(t, s_cur, s_nxt):
        def first():
            finalize(jnp.maximum(tbl(_T_TILE, t) - 1, 0))
            s_nxt[...] = scores_t(t + 1)
            attend_first(t, s_cur)

        def later():
            s_nxt[...] = scores_t(t + 1)
            attend_next(t, s_cur)

        lax.cond(tbl(_T_FIRST, t) == 1, first, later)

    s_a[...] = scores_t(0)
    acc_s[...] = jnp.ones_like(acc_s)

    def two_steps(i, carry):
        step(2 * i, s_a, s_b)
        step(2 * i + 1, s_b, s_a)
        return carry

    lax.fori_loop(0, n_steps // 2, two_steps, 0)
    finalize(n_tiles - 1)


def _attn_b(proj, batch, seq):
    n_pairs = B_WIDTH // LANES
    tile0 = 4 * (A_WIDTH // LANES)

    def col_spec(group):
        return pl.BlockSpec((seq, LANES), lambda b, p, group=group: (b, tile0 + group * n_pairs + p))

    nb = seq // MOBA_BLOCK
    nq2 = PAIR * MOBA_BLOCK
    chunk = MOBA_CHUNK_BLOCKS * MOBA_BLOCK
    assert nb % MOBA_CHUNK_BLOCKS == 0
    n_tiles = nb // MOBA_CHUNK_BLOCKS
    nq = MOBA_CHUNK_BLOCKS * nq2
    n_steps, table = _moba_step_table(n_tiles)
    assert n_steps % 2 == 0
    f32 = jnp.float32
    bf16 = jnp.bfloat16
    return pl.pallas_call(
        functools.partial(_moba_kernel, n_steps=n_steps),
        grid=(batch, n_pairs),
        in_specs=[pl.BlockSpec(memory_space=pltpu.SMEM), col_spec(0), col_spec(1), col_spec(2)],
        out_specs=pl.BlockSpec((seq, LANES), lambda b, p: (b, p)),
        out_shape=jax.ShapeDtypeStruct((batch * seq, B_WIDTH), bf16),
        scratch_shapes=[
            pltpu.VMEM((PAIR, HEAD_DIM + SUM_ROWS, seq), bf16),
            pltpu.VMEM((nb * nq2, LANES), bf16),
            pltpu.VMEM((nb, LANES), f32),
            pltpu.VMEM((n_tiles * nb, nq), f32),
            pltpu.VMEM((chunk, nq), f32),
            pltpu.VMEM((1, nq), f32),
            pltpu.VMEM((HEAD_DIM + SUM_ROWS, nq), f32),
            pltpu.VMEM((chunk, nq), f32), pltpu.VMEM((chunk, nq), f32),
        ],
        compiler_params=pltpu.CompilerParams(
            dimension_semantics=("parallel", "parallel"), vmem_limit_bytes=VMEM_LIMIT),
        name="attn_moba",
    )(jnp.asarray(table, jnp.int32), proj, proj, proj)


def _rope_tables(seq):
    pos = jnp.arange(seq, dtype=jnp.float32)
    inv_freq = ROPE_THETA ** (-jnp.arange(0, HEAD_DIM, 2, dtype=jnp.float32) / HEAD_DIM)
    ang = pos[:, None] * inv_freq[None, :]
    cos = jnp.concatenate([jnp.cos(ang)] * (2 * PAIR), axis=-1)
    sin = jnp.concatenate([-jnp.sin(ang)] * PAIR + [jnp.sin(ang)] * PAIR, axis=-1)
    return cos, sin


def _tile_gain(g, width):
    return _qk_columns(jnp.tile(g, width // HEAD_DIM)[None, :])[0]


def _qk_columns(w):
    lead = w.shape[:-1]
    tiles = w.reshape(lead + (w.shape[-1] // LANES, PAIR, 2, HALF_DIM))
    return jnp.swapaxes(tiles, -3, -2).reshape(w.shape)


def kernel(x, norm_even, w_in_even, w_out_even, qnorm_a, knorm_a, qnorm_b, knorm_b,
           norm_odd, w_in_odd, w_out_odd, qnorm_c, knorm_c, sinks_c):
    batch, seq, d = x.shape
    bf16 = jnp.bfloat16
    scale = HEAD_DIM ** -0.5 * LOG2E
    cos, sin = _rope_tables(seq)
    x2 = x.reshape(batch * seq, d)

    ones = jnp.ones((A_WIDTH,), jnp.float32)
    gain0 = jnp.concatenate([
        _tile_gain(qnorm_a[0], A_WIDTH) * scale, _tile_gain(knorm_a[0], A_WIDTH), ones, ones,
        _tile_gain(qnorm_b[0], B_WIDTH) * scale, _tile_gain(knorm_b[0], B_WIDTH), ones, ones])[None, :]
    tiles_per_group = A_WIDTH // LANES
    rope0 = frozenset(t for grp in (0, 1, 4, 5) for t in range(grp * tiles_per_group, (grp + 1) * tiles_per_group))
    w0 = w_in_even[0]
    groups = [w0[:, g * A_WIDTH:(g + 1) * A_WIDTH] for g in range(8)]
    w0r = jnp.concatenate([_qk_columns(w) if g in (0, 1, 4, 5) else w for g, w in enumerate(groups)], axis=1)
    proj0 = _inproj(x2, norm_even[0][None, :], w0r.astype(bf16), gain0, cos, sin, rope0, seq)
    oa = _attn_a(proj0, batch, seq)
    ob = _attn_b(proj0, batch, seq)

    w1 = w_in_odd[0]
    kv_w = C_KV_HEADS * HEAD_DIM
    wq, wk, wv, wz = (w1[:, :C_WIDTH], w1[:, C_WIDTH:C_WIDTH + kv_w],
                      w1[:, C_WIDTH + kv_w:C_WIDTH + 2 * kv_w], w1[:, C_WIDTH + 2 * kv_w:])

    def dup_heads(w):
        return jnp.concatenate([w[:, h * HEAD_DIM:(h + 1) * HEAD_DIM]
                                for h in range(C_KV_HEADS) for _ in range(PAIR)], axis=1)

    w1r = jnp.concatenate([_qk_columns(wq), _qk_columns(dup_heads(wk)), dup_heads(wv), wz],
                          axis=1).astype(bf16)
    dup_w = PAIR * kv_w
    gain1 = jnp.concatenate([
        _tile_gain(qnorm_c[0], C_WIDTH) * scale, _tile_gain(knorm_c[0], dup_w),
        jnp.ones((dup_w + C_WIDTH,), jnp.float32)])[None, :]
    rope1 = frozenset(range((C_WIDTH + dup_w) // LANES))
    x1, proj1 = _outproj_inproj(x2, [(oa, 0), (ob, 0)], [(proj0, 3), (proj0, 7)], w_out_even[0].astype(bf16),
                                A_WIDTH, norm_odd[0][None, :], w1r, gain1, cos, sin, rope1, seq)
    oc = _attn_c(proj1, sinks_c[0], batch, seq)
    half = C_WIDTH // 2
    z_blk = (C_WIDTH + 2 * dup_w) // half
    out = _outproj(x1, [(oc, 0), (oc, 1)], [(proj1, z_blk), (proj1, z_blk + 1)],
                   w_out_odd[0].astype(bf16), half)
    return out.reshape(batch, seq, d)
```

```python
import functools

import jax
import jax.numpy as jnp
from jax import lax
from jax.experimental import pallas as pl
from jax.experimental.pallas import tpu as pltpu

D_MODEL = 1024
HEAD_DIM = 64
LANES = 128
PAIR = LANES // HEAD_DIM
HALF_DIM = HEAD_DIM // 2
NORM_COLS = 2 * LANES
A_WIDTH = 512
B_WIDTH = 512
C_WIDTH = 1024
C_KV_HEADS = 2
C_GROUP = 8
DILATIONS = (16, 4, 1)
MID_DILATION = 4
BAND = 128
MOBA_BLOCK = 256
MOBA_TOPK = 3
MOBA_CHUNK_BLOCKS = 2
DILATED_UNROLL = 8
SWA_UNROLL = 16
PREP_UNROLL = 4
SUM_ROWS = 16
LOG2E = 1.4426950408889634
ROPE_THETA = 10000.0
NORM_EPS = 1e-6
NEG = -1e30
NO_TIE = 3e38
VMEM_LIMIT = 56 * 1024 * 1024
PROJ_ROWS = 512
PROJ_COLS = 512

_NT = (((1,), (1,)), ((), ()))


def _lane_ids(shape):
    return lax.broadcasted_iota(jnp.int32, shape, len(shape) - 1)


def _qk_head_of_lane(lane):
    return (lane // HALF_DIM) % PAIR


def _split_heads_rows(t):
    head = _qk_head_of_lane(_lane_ids(t.shape))
    zero = jnp.zeros_like(t)
    return jnp.concatenate([jnp.where(head == 0, t, zero), jnp.where(head == 1, t, zero)], axis=0)


def _rms_rows(x, g_ref):
    ms = jnp.mean(x * x, axis=-1, keepdims=True)
    return (x * lax.rsqrt(ms + NORM_EPS) * g_ref[...]).astype(jnp.bfloat16)


def _project_columns(h, w_ref, gain_ref, cos_ref, sin_ref, o_ref, rope_tiles):
    n_cols = o_ref.shape[1]
    li = lax.broadcasted_iota(jnp.int32, (NORM_COLS, NORM_COLS), 0)
    lj = lax.broadcasted_iota(jnp.int32, (NORM_COLS, NORM_COLS), 1)
    same_head = ((li // LANES == lj // LANES)
                 & (_qk_head_of_lane(li) == _qk_head_of_lane(lj))).astype(jnp.bfloat16)
    for c0 in range(0, n_cols, PROJ_COLS):
        r = jnp.dot(h, w_ref[:, c0:c0 + PROJ_COLS], preferred_element_type=jnp.float32)
        for p0 in range(0, PROJ_COLS, NORM_COLS):
            col = c0 + p0
            tiles = range(col // LANES, (col + NORM_COLS) // LANES)
            t = r[:, p0:p0 + NORM_COLS]
            if all(tile in rope_tiles for tile in tiles):
                ss = jnp.dot((t * t).astype(jnp.bfloat16), same_head, preferred_element_type=jnp.float32)
                t = t * lax.rsqrt(ss * (1.0 / HEAD_DIM) + NORM_EPS) * gain_ref[:, col:col + NORM_COLS]
                for t0 in range(0, NORM_COLS, LANES):
                    tn = t[:, t0:t0 + LANES]
                    tn = tn * cos_ref[...] + pltpu.roll(tn, LANES // 2, 1) * sin_ref[...]
                    o_ref[:, col + t0:col + t0 + LANES] = tn.astype(o_ref.dtype)
            else:
                assert not any(tile in rope_tiles for tile in tiles)
                o_ref[:, col:col + NORM_COLS] = t.astype(o_ref.dtype)


def _gated_residual(x, y_refs, z_refs, w_ref):
    acc = x
    k0 = 0
    for y_ref, z_ref in zip(y_refs, z_refs):
        z = z_ref[...].astype(jnp.float32)
        gated = y_ref[...].astype(jnp.float32) * (z / (1.0 + jnp.exp(-z)))
        kw = y_ref.shape[1]
        acc = acc + jnp.dot(gated.astype(jnp.bfloat16), w_ref[k0:k0 + kw, :],
                            preferred_element_type=jnp.float32)
        k0 += kw
    return acc


def _inproj_kernel(x_ref, g_ref, w_ref, gain_ref, cos_ref, sin_ref, o_ref, *, rope_tiles):
    _project_columns(_rms_rows(x_ref[...], g_ref), w_ref, gain_ref, cos_ref, sin_ref, o_ref, rope_tiles)


def _outproj_kernel(*refs, n_parts):
    x_ref, y_refs, z_refs = refs[0], refs[1:1 + n_parts], refs[1 + n_parts:1 + 2 * n_parts]
    w_ref, o_ref = refs[1 + 2 * n_parts:]
    o_ref[...] = _gated_residual(x_ref[...], y_refs, z_refs, w_ref)


def _out_in_kernel(*refs, n_parts, rope_tiles):
    x_ref, y_refs, z_refs = refs[0], refs[1:1 + n_parts], refs[1 + n_parts:1 + 2 * n_parts]
    w_out_ref, g_ref, w_in_ref, gain_ref, cos_ref, sin_ref, x_out_ref, proj_ref = refs[1 + 2 * n_parts:]
    x_new = _gated_residual(x_ref[...], y_refs, z_refs, w_out_ref)
    x_out_ref[...] = x_new
    _project_columns(_rms_rows(x_new, g_ref), w_in_ref, gain_ref, cos_ref, sin_ref, proj_ref, rope_tiles)


def _row_spec(width, blk=0):
    return pl.BlockSpec((PROJ_ROWS, width), lambda i, blk=blk: (i, blk))


def _whole_spec(a):
    return pl.BlockSpec(a.shape, lambda i: (0,) * a.ndim)


def _proj_params():
    return pltpu.CompilerParams(dimension_semantics=("parallel",), vmem_limit_bytes=VMEM_LIMIT)


def _inproj_operands(g, w, gain, cos, sin, seq):
    seq_tiles = seq // PROJ_ROWS
    rope_spec = pl.BlockSpec((PROJ_ROWS, LANES), lambda i: (i % seq_tiles, 0))
    return [g, w, gain, cos, sin], [_whole_spec(g), _whole_spec(w), _whole_spec(gain), rope_spec, rope_spec]


def _gated_operands(ys, zs, w, part_width):
    arrays = [a for a, _ in list(ys) + list(zs)] + [w]
    specs = [_row_spec(part_width, blk) for _, blk in list(ys) + list(zs)] + [_whole_spec(w)]
    return arrays, specs


def _inproj(x2, g, w, gain, cos, sin, rope_tiles, seq):
    rows, d = x2.shape
    arrays, specs = _inproj_operands(g, w, gain, cos, sin, seq)
    return pl.pallas_call(
        functools.partial(_inproj_kernel, rope_tiles=rope_tiles),
        grid=(rows // PROJ_ROWS,),
        in_specs=[_row_spec(d)] + specs,
        out_specs=_row_spec(w.shape[1]),
        out_shape=jax.ShapeDtypeStruct((rows, w.shape[1]), jnp.bfloat16),
        compiler_params=_proj_params(),
        name="inproj",
    )(x2, *arrays)


def _outproj(x2, ys, zs, w, part_width):
    rows, d = x2.shape
    arrays, specs = _gated_operands(ys, zs, w, part_width)
    return pl.pallas_call(
        functools.partial(_outproj_kernel, n_parts=len(ys)),
        grid=(rows // PROJ_ROWS,),
        in_specs=[_row_spec(d)] + specs,
        out_specs=_row_spec(d),
        out_shape=jax.ShapeDtypeStruct((rows, d), jnp.float32),
        compiler_params=_proj_params(),
        name="outproj",
    )(x2, *arrays)


def _outproj_inproj(x2, ys, zs, w_out, part_width, g, w_in, gain, cos, sin, rope_tiles, seq):
    rows, d = x2.shape
    out_arrays, out_specs = _gated_operands(ys, zs, w_out, part_width)
    in_arrays, in_specs = _inproj_operands(g, w_in, gain, cos, sin, seq)
    return pl.pallas_call(
        functools.partial(_out_in_kernel, n_parts=len(ys), rope_tiles=rope_tiles),
        grid=(rows // PROJ_ROWS,),
        in_specs=[_row_spec(d)] + out_specs + in_specs,
        out_specs=[_row_spec(d), _row_spec(w_in.shape[1])],
        out_shape=[jax.ShapeDtypeStruct((rows, d), jnp.float32),
                   jax.ShapeDtypeStruct((rows, w_in.shape[1]), jnp.bfloat16)],
        compiler_params=_proj_params(),
        name="outproj_inproj",
    )(x2, *out_arrays, *in_arrays)


def _fill_band_bias(bias_ref, first_key):
    kj = lax.broadcasted_iota(jnp.int32, (2 * BAND, PAIR * BAND), 0)
    qi = lax.broadcasted_iota(jnp.int32, (2 * BAND, PAIR * BAND), 1) % BAND
    band = (kj >= qi + first_key) & (kj <= qi + BAND)
    bias_ref[0] = jnp.where(band, 0.0, NEG)
    bias_ref[1] = jnp.where(band & (kj >= BAND), 0.0, NEG)


def _band_blocks(operands, sink_row):
    scores = [lax.dot_general(kw, _split_heads_rows(q), _NT, preferred_element_type=jnp.float32) + bias
              for q, kw, _, bias in operands]
    probs = []
    for s in scores:
        m = jnp.max(s, axis=0, keepdims=True)
        if sink_row is not None:
            m = jnp.maximum(m, sink_row)
        probs.append((jnp.exp2(s - m).astype(jnp.bfloat16), m))
    return [(jnp.dot(vtw, p, preferred_element_type=jnp.float32), m)
            for (_, _, vtw, _), (p, m) in zip(operands, probs)]


def _pair_tile(x):
    return jnp.concatenate([x[0:HEAD_DIM, 0:BAND], x[HEAD_DIM:LANES, BAND:PAIR * BAND]], axis=0)


def _pair_stat_tile(v):
    return jnp.concatenate([jnp.broadcast_to(v[:, 0:BAND], (HEAD_DIM, BAND)),
                            jnp.broadcast_to(v[:, BAND:PAIR * BAND], (HEAD_DIM, BAND))], axis=0)


def _attn_a_kernel(q_ref, k_ref, v_ref, o_ref, qf, kf, vf, qf4, kf4, vf4, qd, kd, vdt, bias_s, m_s, l_s, acc_s):
    seq = q_ref.shape[0]
    n_blocks = seq // BAND
    piece = 2 * BAND
    mid = MID_DILATION
    qf[...] = q_ref[...].astype(jnp.float32)
    kf[...] = k_ref[...].astype(jnp.float32)
    vf[...] = v_ref[...].astype(jnp.float32)
    for r in range(mid):
        dst = pl.ds(r * (seq // mid), seq // mid)
        src = pl.ds(r, seq // mid, stride=mid)
        qf4[dst, :] = qf[src, :]
        kf4[dst, :] = kf[src, :]
        vf4[dst, :] = vf[src, :]
    kd[0:BAND, :] = jnp.zeros((BAND, LANES), kd.dtype)
    vdt[:, 0:BAND] = jnp.zeros((LANES + SUM_ROWS, BAND), vdt.dtype)
    vdt[LANES:, BAND:] = jnp.ones((SUM_ROWS, seq), vdt.dtype)
    _fill_band_bias(bias_s, 0)

    for dil in DILATIONS:
        sub = seq // dil
        blocks_per_sub = sub // BAND
        pieces_per_sub = sub // piece

        def deinterleave(i, carry, dil=dil, pieces_per_sub=pieces_per_sub):
            dst = pl.multiple_of(i * piece, piece)
            if dil == 1:
                src = pl.ds(dst, piece)
                q_src, k_src, v_src = q_ref, k_ref, vf
            elif dil == mid:
                src = pl.ds(dst, piece)
                q_src, k_src, v_src = qf4, kf4, vf4
            else:
                r = i // pieces_per_sub
                j = i % pieces_per_sub
                step = dil // mid
                src = pl.ds((r % mid) * (seq // mid) + r // mid + j * (piece * step), piece, stride=step)
                q_src, k_src, v_src = qf4, kf4, vf4
            qd[pl.ds(dst, piece), :] = q_src[src, :].astype(qd.dtype)
            kd[pl.ds(BAND + dst, piece), :] = k_src[src, :].astype(kd.dtype)
            vdt[0:LANES, pl.ds(BAND + dst, piece)] = v_src[src, :].astype(jnp.float32).T.astype(vdt.dtype)
            return carry

        lax.fori_loop(0, seq // piece, deinterleave, 0, unroll=PREP_UNROLL)

        def body(it, carry, dil=dil, blocks_per_sub=blocks_per_sub):
            blocks = [it * DILATED_UNROLL + u for u in range(DILATED_UNROLL)]
            operands = []
            for n in blocks:
                row0 = pl.multiple_of(n * BAND, BAND)
                first = (n % blocks_per_sub) == 0
                operands.append((qd[pl.ds(row0, BAND), :], kd[pl.ds(row0, 2 * BAND), :],
                                 vdt[:, pl.ds(row0, 2 * BAND)], bias_s[jnp.where(first, 1, 0)]))
            for n, (o_t, m) in zip(blocks, _band_blocks(operands, None)):
                merge_block(n, o_t, m, dil, blocks_per_sub)
            return carry

        def merge_block(n, o_t, m, dil, blocks_per_sub):
            r = n // blocks_per_sub
            c = n % blocks_per_sub
            row0 = pl.multiple_of(n * BAND, BAND)
            acc = _pair_tile(o_t).T
            m_b = _pair_stat_tile(m).T
            l_b = _pair_stat_tile(o_t[LANES:LANES + 1, :]).T
            rows = pl.ds(row0, BAND) if dil == 1 else pl.ds(c * (BAND * dil) + r, BAND, stride=dil)
            if dil == DILATIONS[0]:
                m_s[rows, :] = m_b
                l_s[rows, :] = l_b
                acc_s[rows, :] = acc
            else:
                m_old = m_s[rows, :]
                m_new = jnp.maximum(m_old, m_b)
                a_old = jnp.exp2(m_old - m_new)
                a_new = jnp.exp2(m_b - m_new)
                m_s[rows, :] = m_new
                l_s[rows, :] = a_old * l_s[rows, :] + a_new * l_b
                acc_s[rows, :] = a_old * acc_s[rows, :] + a_new * acc

        lax.fori_loop(0, n_blocks // DILATED_UNROLL, body, 0)

    o_ref[...] = (acc_s[...] / l_s[...]).astype(o_ref.dtype)


def _attn_a(proj, batch, seq):
    n_pairs = A_WIDTH // LANES

    def col_spec(group):
        return pl.BlockSpec((seq, LANES), lambda b, p, group=group: (b, group * n_pairs + p))

    f32 = jnp.float32
    bf16 = jnp.bfloat16
    return pl.pallas_call(
        _attn_a_kernel,
        grid=(batch, n_pairs),
        in_specs=[col_spec(0), col_spec(1), col_spec(2)],
        out_specs=pl.BlockSpec((seq, LANES), lambda b, p: (b, p)),
        out_shape=jax.ShapeDtypeStruct((batch * seq, A_WIDTH), bf16),
        scratch_shapes=[
            pltpu.VMEM((seq, LANES), f32), pltpu.VMEM((seq, LANES), f32), pltpu.VMEM((seq, LANES), f32),
            pltpu.VMEM((seq, LANES), f32), pltpu.VMEM((seq, LANES), f32), pltpu.VMEM((seq, LANES), f32),
            pltpu.VMEM((seq, LANES), bf16),
            pltpu.VMEM((seq + BAND, LANES), bf16),
            pltpu.VMEM((LANES + SUM_ROWS, seq + BAND), bf16),
            pltpu.VMEM((2, 2 * BAND, PAIR * BAND), f32),
            pltpu.VMEM((seq, LANES), f32), pltpu.VMEM((seq, LANES), f32), pltpu.VMEM((seq, LANES), f32),
        ],
        compiler_params=pltpu.CompilerParams(
            dimension_semantics=("parallel", "parallel"), vmem_limit_bytes=VMEM_LIMIT),
        name="attn_dilated",
    )(proj, proj, proj)


def _attn_c_kernel(sinks_ref, q_ref, k_ref, v_ref, o_ref, kd, vdt, bias_s):
    seq = q_ref.shape[0]
    pairs_per_kv = q_ref.shape[1] // LANES
    piece = 2 * BAND
    kd[0:BAND, :] = jnp.zeros((BAND, LANES), kd.dtype)
    kd[BAND:, :] = k_ref[...]
    vdt[:, 0:BAND] = jnp.zeros((LANES + SUM_ROWS, BAND), vdt.dtype)
    vdt[LANES:, BAND:] = jnp.ones((SUM_ROWS, seq), vdt.dtype)

    def transpose_v(i, carry):
        rows = pl.ds(pl.multiple_of(i * piece, piece), piece)
        vdt[0:LANES, pl.ds(pl.multiple_of(BAND + i * piece, BAND), piece)] = (
            v_ref[rows, :].astype(jnp.float32).T.astype(vdt.dtype))
        return carry

    lax.fori_loop(0, seq // piece, transpose_v, 0, unroll=PREP_UNROLL)
    _fill_band_bias(bias_s, 1)
    col = lax.broadcasted_iota(jnp.int32, (1, PAIR * BAND), 1)
    iters_per_pair = seq // (BAND * SWA_UNROLL)

    def body(step, carry):
        pair = step // iters_per_pair
        it = step % iters_per_pair
        lanes = pl.ds(pl.multiple_of(pair * LANES, LANES), LANES)
        head0 = PAIR * (pl.program_id(1) * pairs_per_kv + pair)
        sink_row = jnp.where(col < BAND, sinks_ref[head0], sinks_ref[head0 + 1]) * LOG2E
        blocks = [it * SWA_UNROLL + u for u in range(SWA_UNROLL)]
        operands = []
        for n in blocks:
            row0 = pl.multiple_of(n * BAND, BAND)
            operands.append((q_ref[pl.ds(row0, BAND), lanes], kd[pl.ds(row0, 2 * BAND), :],
                             vdt[:, pl.ds(row0, 2 * BAND)], bias_s[jnp.where(n == 0, 1, 0)]))
        for n, (o_t, m) in zip(blocks, _band_blocks(operands, sink_row)):
            l = o_t[LANES:LANES + 1, :] + jnp.exp2(sink_row - m)
            out_t = _pair_tile(o_t) * _pair_stat_tile(1.0 / l)
            o_ref[pl.ds(pl.multiple_of(n * BAND, BAND), BAND), lanes] = out_t.T.astype(o_ref.dtype)
        return carry

    lax.fori_loop(0, pairs_per_kv * iters_per_pair, body, 0)


def _attn_c(proj, sinks, batch, seq):
    group_w = C_GROUP * HEAD_DIM
    k_tile0 = C_WIDTH // LANES
    v_tile0 = k_tile0 + C_KV_HEADS
    bf16 = jnp.bfloat16
    return pl.pallas_call(
        _attn_c_kernel,
        grid=(batch, C_KV_HEADS),
        in_specs=[
            pl.BlockSpec(memory_space=pltpu.SMEM),
            pl.BlockSpec((seq, group_w), lambda b, g: (b, g)),
            pl.BlockSpec((seq, LANES), lambda b, g: (b, k_tile0 + g)),
            pl.BlockSpec((seq, LANES), lambda b, g: (b, v_tile0 + g)),
        ],
        out_specs=pl.BlockSpec((seq, group_w), lambda b, g: (b, g)),
        out_shape=jax.ShapeDtypeStruct((batch * seq, C_WIDTH), bf16),
        scratch_shapes=[
            pltpu.VMEM((seq + BAND, LANES), bf16),
            pltpu.VMEM((LANES + SUM_ROWS, seq + BAND), bf16),
            pltpu.VMEM((2, 2 * BAND, PAIR * BAND), jnp.float32),
        ],
        compiler_params=pltpu.CompilerParams(
            dimension_semantics=("parallel", "parallel"), vmem_limit_bytes=VMEM_LIMIT),
        name="attn_swa",
    )(sinks, proj, proj, proj)


_T_TILE, _T_CH, _T_FIRST, _T_ROWS = range(4)


def _moba_step_table(n_tiles):
    tiles, chs, firsts = [], [], []
    for tile in range(n_tiles):
        for i, ch in enumerate([tile] + list(range(tile))):
            tiles.append(tile)
            chs.append(ch)
            firsts.append(int(i == 0))
    n_steps = len(tiles)
    rows = [None] * _T_ROWS
    rows[_T_TILE] = tiles + [tiles[-1]]
    rows[_T_CH] = chs + [chs[-1]]
    rows[_T_FIRST] = firsts + [0]
    return n_steps, [v for row in rows for v in row]


def _moba_kernel(tbl_ref, q_ref, k_ref, v_ref, o_ref, vt, q2_all, kmean, bias_s, causal_s, m_s, acc_s,
                 s_a, s_b, *, n_steps):
    seq = q_ref.shape[0]
    nb = seq // MOBA_BLOCK
    nq2 = PAIR * MOBA_BLOCK
    nq = MOBA_CHUNK_BLOCKS * nq2
    n_tiles = nb // MOBA_CHUNK_BLOCKS
    chunk = MOBA_CHUNK_BLOCKS * MOBA_BLOCK

    def tbl(row, t):
        return tbl_ref[row * (n_steps + 1) + t]

    def prep(j, carry):
        rows = pl.ds(pl.multiple_of(j * MOBA_BLOCK, MOBA_BLOCK), MOBA_BLOCK)
        v_t = v_ref[rows, :].astype(jnp.float32).T.astype(vt.dtype)
        for h in range(PAIR):
            vt[h, 0:HEAD_DIM, rows] = v_t[h * HEAD_DIM:(h + 1) * HEAD_DIM]
        kmean[pl.ds(j, 1), :] = jnp.mean(k_ref[rows, :].astype(jnp.float32), axis=0, keepdims=True)
        q2_all[pl.ds(pl.multiple_of(j * nq2, nq2), nq2), :] = _split_heads_rows(q_ref[rows, :])
        return carry

    lax.fori_loop(0, nb, prep, 0, unroll=PREP_UNROLL)
    for h in range(PAIR):
        vt[h, HEAD_DIM:, :] = jnp.ones((SUM_ROWS, seq), vt.dtype)
    km = kmean[...]
    km_hi = km.astype(jnp.bfloat16)
    km_hilo = jnp.concatenate([km_hi, (km - km_hi.astype(jnp.float32)).astype(jnp.bfloat16)], axis=0)
    blk_id = lax.broadcasted_iota(jnp.int32, (nb, nq2), 0)
    key_id = lax.broadcasted_iota(jnp.int32, (chunk, nq), 0)
    col_id = lax.broadcasted_iota(jnp.int32, (chunk, nq), 1)
    causal_s[...] = jnp.where(key_id <= col_id % MOBA_BLOCK + (col_id // nq2) * MOBA_BLOCK, 0.0, NEG)

    def chunk_rows(ch):
        return pl.ds(pl.multiple_of(ch * chunk, chunk), chunk)

    def scores_t(t):
        q2 = q2_all[pl.ds(pl.multiple_of(tbl(_T_TILE, t) * nq, nq), nq), :]
        return lax.dot_general(k_ref[chunk_rows(tbl(_T_CH, t)), :], q2, _NT,
                               preferred_element_type=jnp.float32)

    def pv_t(t, p):
        p = p.astype(vt.dtype)
        rows = chunk_rows(tbl(_T_CH, t))
        return jnp.concatenate(
            [jnp.dot(vt[g % PAIR, :, rows], p[:, g * MOBA_BLOCK:(g + 1) * MOBA_BLOCK],
                     preferred_element_type=jnp.float32) for g in range(nq // MOBA_BLOCK)], axis=1)

    def block_bias(t):
        row0 = tbl(_T_TILE, t) * nb + tbl(_T_CH, t) * MOBA_CHUNK_BLOCKS
        return [bias_s[pl.ds(row0 + i, 1), :] for i in range(MOBA_CHUNK_BLOCKS)]

    def per_block(x, fn):
        return [fn(x[i * MOBA_BLOCK:(i + 1) * MOBA_BLOCK], i) for i in range(MOBA_CHUNK_BLOCKS)]

    def select_blocks(tile, carry):
        for i in range(MOBA_CHUNK_BLOCKS):
            qb = tile * MOBA_CHUNK_BLOCKS + i
            q2 = q2_all[pl.ds(pl.multiple_of(qb * nq2, nq2), nq2), :]
            g2 = lax.dot_general(km_hilo, q2, _NT, preferred_element_type=jnp.float32)
            past = blk_id < qb
            gate = jnp.where(past, g2[0:nb] + g2[nb:2 * nb], NEG)
            rank = jnp.zeros((nb, nq2), jnp.int32)
            for j in range(nb - 1):
                gj = gate[j:j + 1, :]
                tie_ref = jnp.where(blk_id > j, gate, NO_TIE)
                rank = rank + jnp.where(gj > gate, 1, 0) + jnp.where(gj == tie_ref, 1, 0)
            allowed = (past & (rank < MOBA_TOPK)) | (blk_id == qb)
            bias_s[pl.ds(pl.multiple_of(tile * nb, nb), nb), i * nq2:(i + 1) * nq2] = jnp.where(allowed, 0.0, NEG)
        return carry

    lax.fori_loop(0, n_tiles, select_blocks, 0, unroll=PREP_UNROLL)

    def attend_first(t, s_cur):
        bias = block_bias(t)

        def left_pad(x, i, fill):
            return x if i == 0 else jnp.concatenate([jnp.full((x.shape[0], i * nq2), fill, x.dtype), x], axis=1)

        parts = []
        for i in range(MOBA_CHUNK_BLOCKS):
            rows, cols = slice(i * MOBA_BLOCK, (i + 1) * MOBA_BLOCK), slice(i * nq2, nq)
            parts.append(s_cur[rows, cols] + causal_s[rows, cols] + bias[i][:, cols])
        m0 = functools.reduce(jnp.maximum, [left_pad(jnp.max(x, axis=0, keepdims=True), i, NEG)
                                            for i, x in enumerate(parts)])
        p = jnp.concatenate([left_pad(jnp.exp2(x - m0[:, i * nq2:]), i, 0.0) for i, x in enumerate(parts)], axis=0)
        m_s[...] = m0
        acc_s[...] = pv_t(t, p)

    def attend_next(t, s_cur):
        s = s_cur[...]
        bias = block_bias(t)
        m_old = m_s[...]
        m_blk = per_block(s, lambda x, i: jnp.max(x, axis=0, keepdims=True) + bias[i])
        m_new = functools.reduce(jnp.maximum, m_blk, m_old)
        alpha = jnp.exp2(m_old - m_new)
        p = jnp.concatenate(per_block(s, lambda x, i: jnp.exp2(x - (m_new - bias[i]))), axis=0)
        m_s[...] = m_new
        acc_s[...] = alpha * acc_s[...] + pv_t(t, p)

    def finalize(tile):
        o_all = acc_s[0:HEAD_DIM, :] * (1.0 / acc_s[HEAD_DIM:HEAD_DIM + 1, :])
        for i in range(MOBA_CHUNK_BLOCKS):
            c0 = i * nq2
            o_t = jnp.concatenate([o_all[:, c0:c0 + MOBA_BLOCK], o_all[:, c0 + MOBA_BLOCK:c0 + nq2]],
                                  axis=0)
            rows = pl.ds(pl.multiple_of((tile * MOBA_CHUNK_BLOCKS + i) * MOBA_BLOCK, MOBA_BLOCK), MOBA_BLOCK)
            o_ref[rows, :] = o_t.T.astype(o_ref.dtype)

    def step(t, s_cur, s_nxt):
        def first():
            finalize(jnp.maximum(tbl(_T_TILE, t) - 1, 0))
            s_nxt[...] = scores_t(t + 1)
            attend_first(t, s_cur)

        def later():
            s_nxt[...] = scores_t(t + 1)
            attend_next(t, s_cur)

        lax.cond(tbl(_T_FIRST, t) == 1, first, later)

    s_a[...] = scores_t(0)
    acc_s[...] = jnp.ones_like(acc_s)

    def two_steps(i, carry):
        step(2 * i, s_a, s_b)
        step(2 * i + 1, s_b, s_a)
        return carry

    lax.fori_loop(0, n_steps // 2, two_steps, 0)
    finalize(n_tiles - 1)


def _attn_b(proj, batch, seq):
    n_pairs = B_WIDTH // LANES
    tile0 = 4 * (A_WIDTH // LANES)

    def col_spec(group):
        return pl.BlockSpec((seq, LANES), lambda b, p, group=group: (b, tile0 + group * n_pairs + p))

    nb = seq // MOBA_BLOCK
    nq2 = PAIR * MOBA_BLOCK
    chunk = MOBA_CHUNK_BLOCKS * MOBA_BLOCK
    assert nb % MOBA_CHUNK_BLOCKS == 0
    n_tiles = nb // MOBA_CHUNK_BLOCKS
    nq = MOBA_CHUNK_BLOCKS * nq2
    n_steps, table = _moba_step_table(n_tiles)
    assert n_steps % 2 == 0
    f32 = jnp.float32
    bf16 = jnp.bfloat16
    return pl.pallas_call(
        functools.partial(_moba_kernel, n_steps=n_steps),
        grid=(batch, n_pairs),
        in_specs=[pl.BlockSpec(memory_space=pltpu.SMEM), col_spec(0), col_spec(1), col_spec(2)],
        out_specs=pl.BlockSpec((seq, LANES), lambda b, p: (b, p)),
        out_shape=jax.ShapeDtypeStruct((batch * seq, B_WIDTH), bf16),
        scratch_shapes=[
            pltpu.VMEM((PAIR, HEAD_DIM + SUM_ROWS, seq), bf16),
            pltpu.VMEM((nb * nq2, LANES), bf16),
            pltpu.VMEM((nb, LANES), f32),
            pltpu.VMEM((n_tiles * nb, nq), f32),
            pltpu.VMEM((chunk, nq), f32),
            pltpu.VMEM((1, nq), f32),
            pltpu.VMEM((HEAD_DIM + SUM_ROWS, nq), f32),
            pltpu.VMEM((chunk, nq), f32), pltpu.VMEM((chunk, nq), f32),
        ],
        compiler_params=pltpu.CompilerParams(
            dimension_semantics=("parallel", "parallel"), vmem_limit_bytes=VMEM_LIMIT),
        name="attn_moba",
    )(jnp.asarray(table, jnp.int32), proj, proj, proj)


def _rope_tables(seq):
    pos = jnp.arange(seq, dtype=jnp.float32)
    inv_freq = ROPE_THETA ** (-jnp.arange(0, HEAD_DIM, 2, dtype=jnp.float32) / HEAD_DIM)
    ang = pos[:, None] * inv_freq[None, :]
    cos = jnp.concatenate([jnp.cos(ang)] * (2 * PAIR), axis=-1)
    sin = jnp.concatenate([-jnp.sin(ang)] * PAIR + [jnp.sin(ang)] * PAIR, axis=-1)
    return cos, sin


def _tile_gain(g, width):
    return _qk_columns(jnp.tile(g, width // HEAD_DIM)[None, :])[0]


def _qk_columns(w):
    lead = w.shape[:-1]
    tiles = w.reshape(lead + (w.shape[-1] // LANES, PAIR, 2, HALF_DIM))
    return jnp.swapaxes(tiles, -3, -2).reshape(w.shape)


def kernel(x, norm_even, w_in_even, w_out_even, qnorm_a, knorm_a, qnorm_b, knorm_b,
           norm_odd, w_in_odd, w_out_odd, qnorm_c, knorm_c, sinks_c):
    batch, seq, d = x.shape
    bf16 = jnp.bfloat16
    scale = HEAD_DIM ** -0.5 * LOG2E
    cos, sin = _rope_tables(seq)
    x2 = x.reshape(batch * seq, d)

    ones = jnp.ones((A_WIDTH,), jnp.float32)
    gain0 = jnp.concatenate([
        _tile_gain(qnorm_a[0], A_WIDTH) * scale, _tile_gain(knorm_a[0], A_WIDTH), ones, ones,
        _tile_gain(qnorm_b[0], B_WIDTH) * scale, _tile_gain(knorm_b[0], B_WIDTH), ones, ones])[None, :]
    tiles_per_group = A_WIDTH // LANES
    rope0 = frozenset(t for grp in (0, 1, 4, 5) for t in range(grp * tiles_per_group, (grp + 1) * tiles_per_group))
    w0 = w_in_even[0]
    groups = [w0[:, g * A_WIDTH:(g + 1) * A_WIDTH] for g in range(8)]
    w0r = jnp.concatenate([_qk_columns(w) if g in (0, 1, 4, 5) else w for g, w in enumerate(groups)], axis=1)
    proj0 = _inproj(x2, norm_even[0][None, :], w0r.astype(bf16), gain0, cos, sin, rope0, seq)
    oa = _attn_a(proj0, batch, seq)
    ob = _attn_b(proj0, batch, seq)

    w1 = w_in_odd[0]
    kv_w = C_KV_HEADS * HEAD_DIM
    wq, wk, wv, wz = (w1[:, :C_WIDTH], w1[:, C_WIDTH:C_WIDTH + kv_w],
                      w1[:, C_WIDTH + kv_w:C_WIDTH + 2 * kv_w], w1[:, C_WIDTH + 2 * kv_w:])

    def dup_heads(w):
        return jnp.concatenate([w[:, h * HEAD_DIM:(h + 1) * HEAD_DIM]
                                for h in range(C_KV_HEADS) for _ in range(PAIR)], axis=1)

    w1r = jnp.concatenate([_qk_columns(wq), _qk_columns(dup_heads(wk)), dup_heads(wv), wz],
                          axis=1).astype(bf16)
    dup_w = PAIR * kv_w
    gain1 = jnp.concatenate([
        _tile_gain(qnorm_c[0], C_WIDTH) * scale, _tile_gain(knorm_c[0], dup_w),
        jnp.ones((dup_w + C_WIDTH,), jnp.float32)])[None, :]
    rope1 = frozenset(range((C_WIDTH + dup_w) // LANES))
    x1, proj1 = _outproj_inproj(x2, [(oa, 0), (ob, 0)], [(proj0, 3), (proj0, 7)], w_out_even[0].astype(bf16),
                                A_WIDTH, norm_odd[0][None, :], w1r, gain1, cos, sin, rope1, seq)
    oc = _attn_c(proj1, sinks_c[0], batch, seq)
    half = C_WIDTH // 2
    z_blk = (C_WIDTH + 2 * dup_w) // half
    out = _outproj(x1, [(oc, 0), (oc, 1)], [(proj1, z_blk), (proj1, z_blk + 1)],
                   w_out_odd[0].astype(bf16), half)
    return out.reshape(batch, seq, d)
```

```python
import functools

import numpy as np
import jax
import jax.numpy as jnp
from jax import lax
from jax.experimental import pallas as pl
from jax.experimental.pallas import tpu as pltpu

D_MODEL = 1024
HEAD_DIM = 64
LANES = 128
PAIR = LANES // HEAD_DIM
HALF_DIM = HEAD_DIM // 2
NORM_COLS = 2 * LANES
A_WIDTH = 512
B_WIDTH = 512
C_WIDTH = 1024
C_KV_HEADS = 2
C_GROUP = 8
DILATIONS = (16, 4, 1)
MID_DILATION = 4
BAND = 128
MOBA_BLOCK = 256
MOBA_TOPK = 3
MOBA_CHUNK_BLOCKS = 2
DILATED_UNROLL = 8
SWA_UNROLL = 16
PREP_UNROLL = 4
SUM_ROWS = 16
LOG2E = 1.4426950408889634
ROPE_THETA = 10000.0
NORM_EPS = 1e-6
NEG = -1e30
NO_TIE = 3e38
VMEM_LIMIT = 56 * 1024 * 1024
PROJ_ROWS = 512
PROJ_COLS = 512

_NT = (((1,), (1,)), ((), ()))


def _lane_ids(shape):
    return lax.broadcasted_iota(jnp.int32, shape, len(shape) - 1)


def _qk_head_of_lane(lane):
    return (lane // HALF_DIM) % PAIR


def _split_heads_rows(t):
    head = _qk_head_of_lane(_lane_ids(t.shape))
    zero = jnp.zeros_like(t)
    return jnp.concatenate([jnp.where(head == 0, t, zero), jnp.where(head == 1, t, zero)], axis=0)


def _rms_rows(x, g_ref):
    ms = jnp.mean(x * x, axis=-1, keepdims=True)
    return (x * lax.rsqrt(ms + NORM_EPS) * g_ref[...]).astype(jnp.bfloat16)


def _project_columns(h, w_ref, gain_ref, cos_ref, sin_ref, o_ref, rope_tiles):
    n_cols = o_ref.shape[1]
    li = lax.broadcasted_iota(jnp.int32, (NORM_COLS, NORM_COLS), 0)
    lj = lax.broadcasted_iota(jnp.int32, (NORM_COLS, NORM_COLS), 1)
    same_head = ((li // LANES == lj // LANES)
                 & (_qk_head_of_lane(li) == _qk_head_of_lane(lj))).astype(jnp.bfloat16)
    for c0 in range(0, n_cols, PROJ_COLS):
        r = jnp.dot(h, w_ref[:, c0:c0 + PROJ_COLS], preferred_element_type=jnp.float32)
        for p0 in range(0, PROJ_COLS, NORM_COLS):
            col = c0 + p0
            tiles = range(col // LANES, (col + NORM_COLS) // LANES)
            t = r[:, p0:p0 + NORM_COLS]
            if all(tile in rope_tiles for tile in tiles):
                ss = jnp.dot((t * t).astype(jnp.bfloat16), same_head, preferred_element_type=jnp.float32)
                t = t * lax.rsqrt(ss * (1.0 / HEAD_DIM) + NORM_EPS) * gain_ref[:, col:col + NORM_COLS]
                for t0 in range(0, NORM_COLS, LANES):
                    tn = t[:, t0:t0 + LANES]
                    tn = tn * cos_ref[...] + pltpu.roll(tn, LANES // 2, 1) * sin_ref[...]
                    o_ref[:, col + t0:col + t0 + LANES] = tn.astype(o_ref.dtype)
            else:
                assert not any(tile in rope_tiles for tile in tiles)
                o_ref[:, col:col + NORM_COLS] = t.astype(o_ref.dtype)


def _gated_residual(x, y_refs, z_refs, w_ref):
    acc = x
    k0 = 0
    for y_ref, z_ref in zip(y_refs, z_refs):
        z = z_ref[...].astype(jnp.float32)
        gated = y_ref[...].astype(jnp.float32) * (z / (1.0 + jnp.exp(-z)))
        kw = y_ref.shape[1]
        acc = acc + jnp.dot(gated.astype(jnp.bfloat16), w_ref[k0:k0 + kw, :],
                            preferred_element_type=jnp.float32)
        k0 += kw
    return acc


def _inproj_kernel(x_ref, g_ref, w_ref, gain_ref, cos_ref, sin_ref, o_ref, *, rope_tiles):
    _project_columns(_rms_rows(x_ref[...], g_ref), w_ref, gain_ref, cos_ref, sin_ref, o_ref, rope_tiles)


def _outproj_kernel(*refs, n_parts):
    x_ref, y_refs, z_refs = refs[0], refs[1:1 + n_parts], refs[1 + n_parts:1 + 2 * n_parts]
    w_ref, o_ref = refs[1 + 2 * n_parts:]
    o_ref[...] = _gated_residual(x_ref[...], y_refs, z_refs, w_ref)


def _out_in_kernel(*refs, n_parts, rope_tiles):
    x_ref, y_refs, z_refs = refs[0], refs[1:1 + n_parts], refs[1 + n_parts:1 + 2 * n_parts]
    w_out_ref, g_ref, w_in_ref, gain_ref, cos_ref, sin_ref, x_out_ref, proj_ref = refs[1 + 2 * n_parts:]
    x_new = _gated_residual(x_ref[...], y_refs, z_refs, w_out_ref)
    x_out_ref[...] = x_new
    _project_columns(_rms_rows(x_new, g_ref), w_in_ref, gain_ref, cos_ref, sin_ref, proj_ref, rope_tiles)


def _row_spec(width, blk=0):
    return pl.BlockSpec((PROJ_ROWS, width), lambda i, blk=blk: (i, blk))


def _whole_spec(a):
    return pl.BlockSpec(a.shape, lambda i: (0,) * a.ndim)


def _proj_params():
    return pltpu.CompilerParams(dimension_semantics=("parallel",), vmem_limit_bytes=VMEM_LIMIT)


def _inproj_operands(g, w, gain, cos, sin, seq):
    seq_tiles = seq // PROJ_ROWS
    rope_spec = pl.BlockSpec((PROJ_ROWS, LANES), lambda i: (i % seq_tiles, 0))
    return [g, w, gain, cos, sin], [_whole_spec(g), _whole_spec(w), _whole_spec(gain), rope_spec, rope_spec]


def _gated_operands(ys, zs, w, part_width):
    arrays = [a for a, _ in list(ys) + list(zs)] + [w]
    specs = [_row_spec(part_width, blk) for _, blk in list(ys) + list(zs)] + [_whole_spec(w)]
    return arrays, specs


def _inproj(x2, g, w, gain, cos, sin, rope_tiles, seq):
    rows, d = x2.shape
    arrays, specs = _inproj_operands(g, w, gain, cos, sin, seq)
    return pl.pallas_call(
        functools.partial(_inproj_kernel, rope_tiles=rope_tiles),
        grid=(rows // PROJ_ROWS,),
        in_specs=[_row_spec(d)] + specs,
        out_specs=_row_spec(w.shape[1]),
        out_shape=jax.ShapeDtypeStruct((rows, w.shape[1]), jnp.bfloat16),
        compiler_params=_proj_params(),
        name="inproj",
    )(x2, *arrays)


def _outproj(x2, ys, zs, w, part_width):
    rows, d = x2.shape
    arrays, specs = _gated_operands(ys, zs, w, part_width)
    return pl.pallas_call(
        functools.partial(_outproj_kernel, n_parts=len(ys)),
        grid=(rows // PROJ_ROWS,),
        in_specs=[_row_spec(d)] + specs,
        out_specs=_row_spec(d),
        out_shape=jax.ShapeDtypeStruct((rows, d), jnp.float32),
        compiler_params=_proj_params(),
        name="outproj",
    )(x2, *arrays)


def _outproj_inproj(x2, ys, zs, w_out, part_width, g, w_in, gain, cos, sin, rope_tiles, seq):
    rows, d = x2.shape
    out_arrays, out_specs = _gated_operands(ys, zs, w_out, part_width)
    in_arrays, in_specs = _inproj_operands(g, w_in, gain, cos, sin, seq)
    return pl.pallas_call(
        functools.partial(_out_in_kernel, n_parts=len(ys), rope_tiles=rope_tiles),
        grid=(rows // PROJ_ROWS,),
        in_specs=[_row_spec(d)] + out_specs + in_specs,
        out_specs=[_row_spec(d), _row_spec(w_in.shape[1])],
        out_shape=[jax.ShapeDtypeStruct((rows, d), jnp.float32),
                   jax.ShapeDtypeStruct((rows, w_in.shape[1]), jnp.bfloat16)],
        compiler_params=_proj_params(),
        name="outproj_inproj",
    )(x2, *out_arrays, *in_arrays)


def _band_bias_table(first_key):
    kj = np.arange(2 * BAND)[:, None]
    qi = np.arange(PAIR * BAND)[None, :] % BAND
    band = (kj >= qi + first_key) & (kj <= qi + BAND)
    return np.where(np.stack([band, band & (kj >= BAND)]), 0.0, NEG).astype(np.float32)


def _band_blocks(operands, sink_row):
    scores = [lax.dot_general(kw, _split_heads_rows(q), _NT, preferred_element_type=jnp.float32) + bias
              for q, kw, _, bias in operands]
    probs = []
    for s in scores:
        m = jnp.max(s, axis=0, keepdims=True)
        if sink_row is not None:
            m = jnp.maximum(m, sink_row)
        probs.append((jnp.exp2(s - m).astype(jnp.bfloat16), m))
    return [(jnp.dot(vtw, p, preferred_element_type=jnp.float32), m)
            for (_, _, vtw, _), (p, m) in zip(operands, probs)]


def _resident_spec(a):
    return pl.BlockSpec(a.shape, lambda *_: (0,) * a.ndim)


def _pair_tile(x):
    return jnp.concatenate([x[0:HEAD_DIM, 0:BAND], x[HEAD_DIM:LANES, BAND:PAIR * BAND]], axis=0)


def _pair_stat_tile(v):
    return jnp.concatenate([jnp.broadcast_to(v[:, 0:BAND], (HEAD_DIM, BAND)),
                            jnp.broadcast_to(v[:, BAND:PAIR * BAND], (HEAD_DIM, BAND))], axis=0)


def _attn_a_kernel(q_ref, k_ref, v_ref, bias_s, o_ref, qf, kf, vf, qf4, kf4, vf4, qd, kd, vdt, m_s, l_s, acc_s):
    seq = q_ref.shape[0]
    n_blocks = seq // BAND
    piece = 2 * BAND
    mid = MID_DILATION
    qf[...] = q_ref[...].astype(jnp.float32)
    kf[...] = k_ref[...].astype(jnp.float32)
    vf[...] = v_ref[...].astype(jnp.float32)
    for r in range(mid):
        dst = pl.ds(r * (seq // mid), seq // mid)
        src = pl.ds(r, seq // mid, stride=mid)
        qf4[dst, :] = qf[src, :]
        kf4[dst, :] = kf[src, :]
        vf4[dst, :] = vf[src, :]
    kd[0:BAND, :] = jnp.zeros((BAND, LANES), kd.dtype)
    vdt[:, 0:BAND] = jnp.zeros((LANES + SUM_ROWS, BAND), vdt.dtype)
    vdt[LANES:, BAND:] = jnp.ones((SUM_ROWS, seq), vdt.dtype)

    for dil in DILATIONS:
        sub = seq // dil
        blocks_per_sub = sub // BAND
        pieces_per_sub = sub // piece

        def deinterleave(i, carry, dil=dil, pieces_per_sub=pieces_per_sub):
            dst = pl.multiple_of(i * piece, piece)
            if dil == 1:
                src = pl.ds(dst, piece)
                q_src, k_src, v_src = q_ref, k_ref, vf
            elif dil == mid:
                src = pl.ds(dst, piece)
                q_src, k_src, v_src = qf4, kf4, vf4
            else:
                r = i // pieces_per_sub
                j = i % pieces_per_sub
                step = dil // mid
                src = pl.ds((r % mid) * (seq // mid) + r // mid + j * (piece * step), piece, stride=step)
                q_src, k_src, v_src = qf4, kf4, vf4
            qd[pl.ds(dst, piece), :] = q_src[src, :].astype(qd.dtype)
            kd[pl.ds(BAND + dst, piece), :] = k_src[src, :].astype(kd.dtype)
            vdt[0:LANES, pl.ds(BAND + dst, piece)] = v_src[src, :].astype(jnp.float32).T.astype(vdt.dtype)
            return carry

        lax.fori_loop(0, seq // piece, deinterleave, 0, unroll=PREP_UNROLL)

        def body(it, carry, dil=dil, blocks_per_sub=blocks_per_sub):
            blocks = [it * DILATED_UNROLL + u for u in range(DILATED_UNROLL)]
            operands = []
            for n in blocks:
                row0 = pl.multiple_of(n * BAND, BAND)
                first = (n % blocks_per_sub) == 0
                operands.append((qd[pl.ds(row0, BAND), :], kd[pl.ds(row0, 2 * BAND), :],
                                 vdt[:, pl.ds(row0, 2 * BAND)], bias_s[jnp.where(first, 1, 0)]))
            for n, (o_t, m) in zip(blocks, _band_blocks(operands, None)):
                merge_block(n, o_t, m, dil, blocks_per_sub)
            return carry

        def merge_block(n, o_t, m, dil, blocks_per_sub):
            r = n // blocks_per_sub
            c = n % blocks_per_sub
            row0 = pl.multiple_of(n * BAND, BAND)
            acc = _pair_tile(o_t).T
            m_b = _pair_stat_tile(m).T
            l_b = _pair_stat_tile(o_t[LANES:LANES + 1, :]).T
            rows = pl.ds(row0, BAND) if dil == 1 else pl.ds(c * (BAND * dil) + r, BAND, stride=dil)
            if dil == DILATIONS[0]:
                m_s[rows, :] = m_b
                l_s[rows, :] = l_b
                acc_s[rows, :] = acc
            else:
                m_old = m_s[rows, :]
                m_new = jnp.maximum(m_old, m_b)
                a_old = jnp.exp2(m_old - m_new)
                a_new = jnp.exp2(m_b - m_new)
                m_s[rows, :] = m_new
                l_s[rows, :] = a_old * l_s[rows, :] + a_new * l_b
                acc_s[rows, :] = a_old * acc_s[rows, :] + a_new * acc

        lax.fori_loop(0, n_blocks // DILATED_UNROLL, body, 0)

    o_ref[...] = (acc_s[...] / l_s[...]).astype(o_ref.dtype)


def _attn_a(proj, batch, seq):
    n_pairs = A_WIDTH // LANES

    def col_spec(group):
        return pl.BlockSpec((seq, LANES), lambda b, p, group=group: (b, group * n_pairs + p))

    f32 = jnp.float32
    bf16 = jnp.bfloat16
    bias = jnp.asarray(_band_bias_table(0))
    return pl.pallas_call(
        _attn_a_kernel,
        grid=(batch, n_pairs),
        in_specs=[col_spec(0), col_spec(1), col_spec(2), _resident_spec(bias)],
        out_specs=pl.BlockSpec((seq, LANES), lambda b, p: (b, p)),
        out_shape=jax.ShapeDtypeStruct((batch * seq, A_WIDTH), bf16),
        scratch_shapes=[
            pltpu.VMEM((seq, LANES), f32), pltpu.VMEM((seq, LANES), f32), pltpu.VMEM((seq, LANES), f32),
            pltpu.VMEM((seq, LANES), f32), pltpu.VMEM((seq, LANES), f32), pltpu.VMEM((seq, LANES), f32),
            pltpu.VMEM((seq, LANES), bf16),
            pltpu.VMEM((seq + BAND, LANES), bf16),
            pltpu.VMEM((LANES + SUM_ROWS, seq + BAND), bf16),
            pltpu.VMEM((seq, LANES), f32), pltpu.VMEM((seq, LANES), f32), pltpu.VMEM((seq, LANES), f32),
        ],
        compiler_params=pltpu.CompilerParams(
            dimension_semantics=("parallel", "parallel"), vmem_limit_bytes=VMEM_LIMIT),
        name="attn_dilated",
    )(proj, proj, proj, bias)


def _attn_c_kernel(sinks_ref, q_ref, k_ref, v_ref, bias_s, o_ref, kd, vdt):
    seq = q_ref.shape[0]
    pairs_per_kv = q_ref.shape[1] // LANES
    piece = 2 * BAND
    kd[0:BAND, :] = jnp.zeros((BAND, LANES), kd.dtype)
    kd[BAND:, :] = k_ref[...]
    vdt[:, 0:BAND] = jnp.zeros((LANES + SUM_ROWS, BAND), vdt.dtype)
    vdt[LANES:, BAND:] = jnp.ones((SUM_ROWS, seq), vdt.dtype)

    def transpose_v(i, carry):
        rows = pl.ds(pl.multiple_of(i * piece, piece), piece)
        vdt[0:LANES, pl.ds(pl.multiple_of(BAND + i * piece, BAND), piece)] = (
            v_ref[rows, :].astype(jnp.float32).T.astype(vdt.dtype))
        return carry

    lax.fori_loop(0, seq // piece, transpose_v, 0, unroll=PREP_UNROLL)
    col = lax.broadcasted_iota(jnp.int32, (1, PAIR * BAND), 1)
    iters_per_pair = seq // (BAND * SWA_UNROLL)

    def body(step, carry):
        pair = step // iters_per_pair
        it = step % iters_per_pair
        lanes = pl.ds(pl.multiple_of(pair * LANES, LANES), LANES)
        head0 = PAIR * (pl.program_id(1) * pairs_per_kv + pair)
        sink_row = jnp.where(col < BAND, sinks_ref[head0], sinks_ref[head0 + 1]) * LOG2E
        blocks = [it * SWA_UNROLL + u for u in range(SWA_UNROLL)]
        operands = []
        for n in blocks:
            row0 = pl.multiple_of(n * BAND, BAND)
            operands.append((q_ref[pl.ds(row0, BAND), lanes], kd[pl.ds(row0, 2 * BAND), :],
                             vdt[:, pl.ds(row0, 2 * BAND)], bias_s[jnp.where(n == 0, 1, 0)]))
        for n, (o_t, m) in zip(blocks, _band_blocks(operands, sink_row)):
            l = o_t[LANES:LANES + 1, :] + jnp.exp2(sink_row - m)
            out_t = _pair_tile(o_t) * _pair_stat_tile(1.0 / l)
            o_ref[pl.ds(pl.multiple_of(n * BAND, BAND), BAND), lanes] = out_t.T.astype(o_ref.dtype)
        return carry

    lax.fori_loop(0, pairs_per_kv * iters_per_pair, body, 0)


def _attn_c(proj, sinks, batch, seq):
    group_w = C_GROUP * HEAD_DIM
    k_tile0 = C_WIDTH // LANES
    v_tile0 = k_tile0 + C_KV_HEADS
    bf16 = jnp.bfloat16
    bias = jnp.asarray(_band_bias_table(1))
    return pl.pallas_call(
        _attn_c_kernel,
        grid=(batch, C_KV_HEADS),
        in_specs=[
            pl.BlockSpec(memory_space=pltpu.SMEM),
            pl.BlockSpec((seq, group_w), lambda b, g: (b, g)),
            pl.BlockSpec((seq, LANES), lambda b, g: (b, k_tile0 + g)),
            pl.BlockSpec((seq, LANES), lambda b, g: (b, v_tile0 + g)),
            _resident_spec(bias),
        ],
        out_specs=pl.BlockSpec((seq, group_w), lambda b, g: (b, g)),
        out_shape=jax.ShapeDtypeStruct((batch * seq, C_WIDTH), bf16),
        scratch_shapes=[
            pltpu.VMEM((seq + BAND, LANES), bf16),
            pltpu.VMEM((LANES + SUM_ROWS, seq + BAND), bf16),
        ],
        compiler_params=pltpu.CompilerParams(
            dimension_semantics=("parallel", "parallel"), vmem_limit_bytes=VMEM_LIMIT),
        name="attn_swa",
    )(sinks, proj, proj, proj, bias)


_T_TILE, _T_CH, _T_FIRST, _T_ROWS = range(4)


def _moba_step_table(n_tiles):
    tiles, chs, firsts = [], [], []
    for tile in range(n_tiles):
        for i, ch in enumerate([tile] + list(range(tile))):
            tiles.append(tile)
            chs.append(ch)
            firsts.append(int(i == 0))
    n_steps = len(tiles)
    rows = [None] * _T_ROWS
    rows[_T_TILE] = tiles + [tiles[-1]]
    rows[_T_CH] = chs + [chs[-1]]
    rows[_T_FIRST] = firsts + [0]
    return n_steps, [v for row in rows for v in row]


def _moba_kernel(tbl_ref, q_ref, k_ref, v_ref, causal_s, o_ref, vt, q2_all, kmean, bias_s, m_s, acc_s,
                 s_a, s_b, *, n_steps):
    seq = q_ref.shape[0]
    nb = seq // MOBA_BLOCK
    nq2 = PAIR * MOBA_BLOCK
    nq = MOBA_CHUNK_BLOCKS * nq2
    n_tiles = nb // MOBA_CHUNK_BLOCKS
    chunk = MOBA_CHUNK_BLOCKS * MOBA_BLOCK

    def tbl(row, t):
        return tbl_ref[row * (n_steps + 1) + t]

    def prep(j, carry):
        rows = pl.ds(pl.multiple_of(j * MOBA_BLOCK, MOBA_BLOCK), MOBA_BLOCK)
        v_t = v_ref[rows, :].astype(jnp.float32).T.astype(vt.dtype)
        for h in range(PAIR):
            vt[h, 0:HEAD_DIM, rows] = v_t[h * HEAD_DIM:(h + 1) * HEAD_DIM]
        kmean[pl.ds(j, 1), :] = jnp.mean(k_ref[rows, :].astype(jnp.float32), axis=0, keepdims=True)
        q2_all[pl.ds(pl.multiple_of(j * nq2, nq2), nq2), :] = _split_heads_rows(q_ref[rows, :])
        return carry

    lax.fori_loop(0, nb, prep, 0, unroll=PREP_UNROLL)
    for h in range(PAIR):
        vt[h, HEAD_DIM:, :] = jnp.ones((SUM_ROWS, seq), vt.dtype)
    km = kmean[...]
    km_hi = km.astype(jnp.bfloat16)
    km_hilo = jnp.concatenate([km_hi, (km - km_hi.astype(jnp.float32)).astype(jnp.bfloat16)], axis=0)
    blk_id = lax.broadcasted_iota(jnp.int32, (nb, nq2), 0)

    def chunk_rows(ch):
        return pl.ds(pl.multiple_of(ch * chunk, chunk), chunk)

    def scores_t(t):
        q2 = q2_all[pl.ds(pl.multiple_of(tbl(_T_TILE, t) * nq, nq), nq), :]
        return lax.dot_general(k_ref[chunk_rows(tbl(_T_CH, t)), :], q2, _NT,
                               preferred_element_type=jnp.float32)

    def pv_t(t, p):
        p = p.astype(vt.dtype)
        rows = chunk_rows(tbl(_T_CH, t))
        return jnp.concatenate(
            [jnp.dot(vt[g % PAIR, :, rows], p[:, g * MOBA_BLOCK:(g + 1) * MOBA_BLOCK],
                     preferred_element_type=jnp.float32) for g in range(nq // MOBA_BLOCK)], axis=1)

    def block_bias(t):
        row0 = tbl(_T_TILE, t) * nb + tbl(_T_CH, t) * MOBA_CHUNK_BLOCKS
        return [bias_s[pl.ds(row0 + i, 1), :] for i in range(MOBA_CHUNK_BLOCKS)]

    def per_block(x, fn):
        return [fn(x[i * MOBA_BLOCK:(i + 1) * MOBA_BLOCK], i) for i in range(MOBA_CHUNK_BLOCKS)]

    def select_blocks(tile, carry):
        for i in range(MOBA_CHUNK_BLOCKS):
            qb = tile * MOBA_CHUNK_BLOCKS + i
            q2 = q2_all[pl.ds(pl.multiple_of(qb * nq2, nq2), nq2), :]
            g2 = lax.dot_general(km_hilo, q2, _NT, preferred_element_type=jnp.float32)
            past = blk_id < qb
            gate = jnp.where(past, g2[0:nb] + g2[nb:2 * nb], NEG)
            rank = jnp.zeros((nb, nq2), jnp.int32)
            for j in range(nb - 1):
                gj = gate[j:j + 1, :]
                tie_ref = jnp.where(blk_id > j, gate, NO_TIE)
                rank = rank + jnp.where(gj > gate, 1, 0) + jnp.where(gj == tie_ref, 1, 0)
            allowed = (past & (rank < MOBA_TOPK)) | (blk_id == qb)
            bias_s[pl.ds(pl.multiple_of(tile * nb, nb), nb), i * nq2:(i + 1) * nq2] = jnp.where(allowed, 0.0, NEG)
        return carry

    lax.fori_loop(0, n_tiles, select_blocks, 0, unroll=PREP_UNROLL)

    def attend_first(t, s_cur):
        bias = block_bias(t)

        def left_pad(x, i, fill):
            return x if i == 0 else jnp.concatenate([jnp.full((x.shape[0], i * nq2), fill, x.dtype), x], axis=1)

        parts = []
        for i in range(MOBA_CHUNK_BLOCKS):
            rows, cols = slice(i * MOBA_BLOCK, (i + 1) * MOBA_BLOCK), slice(i * nq2, nq)
            parts.append(s_cur[rows, cols] + causal_s[rows, cols] + bias[i][:, cols])
        m0 = functools.reduce(jnp.maximum, [left_pad(jnp.max(x, axis=0, keepdims=True), i, NEG)
                                            for i, x in enumerate(parts)])
        p = jnp.concatenate([left_pad(jnp.exp2(x - m0[:, i * nq2:]), i, 0.0) for i, x in enumerate(parts)], axis=0)
        m_s[...] = m0
        acc_s[...] = pv_t(t, p)

    def attend_next(t, s_cur):
        s = s_cur[...]
        bias = block_bias(t)
        m_old = m_s[...]
        m_blk = per_block(s, lambda x, i: jnp.max(x, axis=0, keepdims=True) + bias[i])
        m_new = functools.reduce(jnp.maximum, m_blk, m_old)
        alpha = jnp.exp2(m_old - m_new)
        p = jnp.concatenate(per_block(s, lambda x, i: jnp.exp2(x - (m_new - bias[i]))), axis=0)
        m_s[...] = m_new
        acc_s[...] = alpha * acc_s[...] + pv_t(t, p)

    def finalize(tile):
        o_all = acc_s[0:HEAD_DIM, :] * (1.0 / acc_s[HEAD_DIM:HEAD_DIM + 1, :])
        for i in range(MOBA_CHUNK_BLOCKS):
            c0 = i * nq2
            o_t = jnp.concatenate([o_all[:, c0:c0 + MOBA_BLOCK], o_all[:, c0 + MOBA_BLOCK:c0 + nq2]],
                                  axis=0)
            rows = pl.ds(pl.multiple_of((tile * MOBA_CHUNK_BLOCKS + i) * MOBA_BLOCK, MOBA_BLOCK), MOBA_BLOCK)
            o_ref[rows, :] = o_t.T.astype(o_ref.dtype)

    def step(t, s_cur, s_nxt):
        def first():
            finalize(jnp.maximum(tbl(_T_TILE, t) - 1, 0))
            s_nxt[...] = scores_t(t + 1)
            attend_first(t, s_cur)

        def later():
            s_nxt[...] = scores_t(t + 1)
            attend_next(t, s_cur)

        lax.cond(tbl(_T_FIRST, t) == 1, first, later)

    s_a[...] = scores_t(0)
    acc_s[...] = jnp.ones_like(acc_s)

    def two_steps(i, carry):
        step(2 * i, s_a, s_b)
        step(2 * i + 1, s_b, s_a)
        return carry

    lax.fori_loop(0, n_steps // 2, two_steps, 0)
    finalize(n_tiles - 1)


def _attn_b(proj, batch, seq):
    n_pairs = B_WIDTH // LANES
    tile0 = 4 * (A_WIDTH // LANES)

    def col_spec(group):
        return pl.BlockSpec((seq, LANES), lambda b, p, group=group: (b, tile0 + group * n_pairs + p))

    nb = seq // MOBA_BLOCK
    nq2 = PAIR * MOBA_BLOCK
    chunk = MOBA_CHUNK_BLOCKS * MOBA_BLOCK
    assert nb % MOBA_CHUNK_BLOCKS == 0
    n_tiles = nb // MOBA_CHUNK_BLOCKS
    nq = MOBA_CHUNK_BLOCKS * nq2
    n_steps, table = _moba_step_table(n_tiles)
    assert n_steps % 2 == 0
    key_id = np.arange(chunk)[:, None]
    col_id = np.arange(nq)[None, :]
    causal = jnp.asarray(np.where(key_id <= col_id % MOBA_BLOCK + (col_id // nq2) * MOBA_BLOCK, 0.0, NEG)
                         .astype(np.float32))
    f32 = jnp.float32
    bf16 = jnp.bfloat16
    return pl.pallas_call(
        functools.partial(_moba_kernel, n_steps=n_steps),
        grid=(batch, n_pairs),
        in_specs=[pl.BlockSpec(memory_space=pltpu.SMEM), col_spec(0), col_spec(1), col_spec(2),
                  _resident_spec(causal)],
        out_specs=pl.BlockSpec((seq, LANES), lambda b, p: (b, p)),
        out_shape=jax.ShapeDtypeStruct((batch * seq, B_WIDTH), bf16),
        scratch_shapes=[
            pltpu.VMEM((PAIR, HEAD_DIM + SUM_ROWS, seq), bf16),
            pltpu.VMEM((nb * nq2, LANES), bf16),
            pltpu.VMEM((nb, LANES), f32),
            pltpu.VMEM((n_tiles * nb, nq), f32),
            pltpu.VMEM((1, nq), f32),
            pltpu.VMEM((HEAD_DIM + SUM_ROWS, nq), f32),
            pltpu.VMEM((chunk, nq), f32), pltpu.VMEM((chunk, nq), f32),
        ],
        compiler_params=pltpu.CompilerParams(
            dimension_semantics=("parallel", "parallel"), vmem_limit_bytes=VMEM_LIMIT),
        name="attn_moba",
    )(jnp.asarray(table, jnp.int32), proj, proj, proj, causal)


def _rope_tables(seq):
    pos = jnp.arange(seq, dtype=jnp.float32)
    inv_freq = ROPE_THETA ** (-jnp.arange(0, HEAD_DIM, 2, dtype=jnp.float32) / HEAD_DIM)
    lane = np.arange(LANES)
    ang = pos[:, None] * inv_freq[lane % HALF_DIM][None, :]
    sign = np.where(lane < LANES // 2, -1.0, 1.0).astype(np.float32)
    return jnp.cos(ang), jnp.sin(ang) * sign


def _tile_gain(g, width):
    lane = np.arange(LANES)
    return jnp.tile(g[lane % HALF_DIM + HALF_DIM * (lane // (LANES // 2))], width // LANES)


def _qk_columns(w):
    lead = w.shape[:-1]
    tiles = w.reshape(lead + (w.shape[-1] // LANES, PAIR, 2, HALF_DIM))
    return jnp.swapaxes(tiles, -3, -2).reshape(w.shape)


def kernel(x, norm_even, w_in_even, w_out_even, qnorm_a, knorm_a, qnorm_b, knorm_b,
           norm_odd, w_in_odd, w_out_odd, qnorm_c, knorm_c, sinks_c):
    batch, seq, d = x.shape
    bf16 = jnp.bfloat16
    scale = HEAD_DIM ** -0.5 * LOG2E
    cos, sin = _rope_tables(seq)
    x2 = x.reshape(batch * seq, d)

    ones = jnp.ones((A_WIDTH,), jnp.float32)
    gain0 = jnp.concatenate([
        _tile_gain(qnorm_a[0], A_WIDTH) * scale, _tile_gain(knorm_a[0], A_WIDTH), ones, ones,
        _tile_gain(qnorm_b[0], B_WIDTH) * scale, _tile_gain(knorm_b[0], B_WIDTH), ones, ones])[None, :]
    tiles_per_group = A_WIDTH // LANES
    rope0 = frozenset(t for grp in (0, 1, 4, 5) for t in range(grp * tiles_per_group, (grp + 1) * tiles_per_group))
    w0 = w_in_even[0].reshape(d, 2, 4, A_WIDTH // LANES, PAIR, 2, HALF_DIM)
    w0r = jnp.concatenate([jnp.swapaxes(w0[:, :, :2], -3, -2), w0[:, :, 2:]], axis=2).reshape(d, -1)
    proj0 = _inproj(x2, norm_even[0][None, :], w0r.astype(bf16), gain0, cos, sin, rope0, seq)
    oa = _attn_a(proj0, batch, seq)
    ob = _attn_b(proj0, batch, seq)

    w1 = w_in_odd[0]
    kv_w = C_KV_HEADS * HEAD_DIM
    wq, wk, wv, wz = (w1[:, :C_WIDTH], w1[:, C_WIDTH:C_WIDTH + kv_w],
                      w1[:, C_WIDTH + kv_w:C_WIDTH + 2 * kv_w], w1[:, C_WIDTH + 2 * kv_w:])

    def dup_heads(w):
        return jnp.repeat(w.reshape(d, C_KV_HEADS, HEAD_DIM), PAIR, axis=1).reshape(d, -1)

    w1r = jnp.concatenate([_qk_columns(wq), _qk_columns(dup_heads(wk)), dup_heads(wv), wz],
                          axis=1).astype(bf16)
    dup_w = PAIR * kv_w
    gain1 = jnp.concatenate([
        _tile_gain(qnorm_c[0], C_WIDTH) * scale, _tile_gain(knorm_c[0], dup_w),
        jnp.ones((dup_w + C_WIDTH,), jnp.float32)])[None, :]
    rope1 = frozenset(range((C_WIDTH + dup_w) // LANES))
    x1, proj1 = _outproj_inproj(x2, [(oa, 0), (ob, 0)], [(proj0, 3), (proj0, 7)], w_out_even[0].astype(bf16),
                                A_WIDTH, norm_odd[0][None, :], w1r, gain1, cos, sin, rope1, seq)
    oc = _attn_c(proj1, sinks_c[0], batch, seq)
    half = C_WIDTH // 2
    z_blk = (C_WIDTH + 2 * dup_w) // half
    out = _outproj(x1, [(oc, 0), (oc, 1)], [(proj1, z_blk), (proj1, z_blk + 1)],
                   w_out_odd[0].astype(bf16), half)
    return out.reshape(batch, seq, d)
```

```python
import functools

import jax
import jax.numpy as jnp
from jax import lax
from jax.experimental import pallas as pl
from jax.experimental.pallas import tpu as pltpu

D_MODEL = 1024
HEAD_DIM = 64
LANES = 128
PAIR = LANES // HEAD_DIM
HALF_DIM = HEAD_DIM // 2
NORM_COLS = 2 * LANES
A_WIDTH = 512
B_WIDTH = 512
C_WIDTH = 1024
C_KV_HEADS = 2
C_GROUP = 8
DILATIONS = (16, 4, 1)
MID_DILATION = 4
BAND = 128
MOBA_BLOCK = 256
MOBA_TOPK = 3
MOBA_CHUNK_BLOCKS = 2
DILATED_UNROLL = 8
SWA_UNROLL = 16
PREP_UNROLL = 4
SUM_ROWS = 16
LOG2E = 1.4426950408889634
ROPE_THETA = 10000.0
NORM_EPS = 1e-6
NEG = -1e30
NO_TIE = 3e38
VMEM_LIMIT = 56 * 1024 * 1024
PROJ_ROWS = 512
PROJ_COLS = 512

_NT = (((1,), (1,)), ((), ()))


def _lane_ids(shape):
    return lax.broadcasted_iota(jnp.int32, shape, len(shape) - 1)


def _qk_head_of_lane(lane):
    return (lane // HALF_DIM) % PAIR


def _split_heads_rows(t):
    head = _qk_head_of_lane(_lane_ids(t.shape))
    zero = jnp.zeros_like(t)
    return jnp.concatenate([jnp.where(head == 0, t, zero), jnp.where(head == 1, t, zero)], axis=0)


def _rms_rows(x, g_ref):
    ms = jnp.mean(x * x, axis=-1, keepdims=True)
    return (x * lax.rsqrt(ms + NORM_EPS) * g_ref[...]).astype(jnp.bfloat16)


def _project_columns(h, w_ref, gain_ref, cos_ref, sin_ref, o_ref, rope_tiles):
    n_cols = o_ref.shape[1]
    li = lax.broadcasted_iota(jnp.int32, (NORM_COLS, NORM_COLS), 0)
    lj = lax.broadcasted_iota(jnp.int32, (NORM_COLS, NORM_COLS), 1)
    same_head = ((li // LANES == lj // LANES)
                 & (_qk_head_of_lane(li) == _qk_head_of_lane(lj))).astype(jnp.bfloat16)
    for c0 in range(0, n_cols, PROJ_COLS):
        r = jnp.dot(h, w_ref[:, c0:c0 + PROJ_COLS], preferred_element_type=jnp.float32)
        for p0 in range(0, PROJ_COLS, NORM_COLS):
            col = c0 + p0
            tiles = range(col // LANES, (col + NORM_COLS) // LANES)
            t = r[:, p0:p0 + NORM_COLS]
            if all(tile in rope_tiles for tile in tiles):
                ss = jnp.dot((t * t).astype(jnp.bfloat16), same_head, preferred_element_type=jnp.float32)
                t = t * lax.rsqrt(ss * (1.0 / HEAD_DIM) + NORM_EPS) * gain_ref[:, col:col + NORM_COLS]
                for t0 in range(0, NORM_COLS, LANES):
                    tn = t[:, t0:t0 + LANES]
                    tn = tn * cos_ref[...] + pltpu.roll(tn, LANES // 2, 1) * sin_ref[...]
                    o_ref[:, col + t0:col + t0 + LANES] = tn.astype(o_ref.dtype)
            else:
                assert not any(tile in rope_tiles for tile in tiles)
                o_ref[:, col:col + NORM_COLS] = t.astype(o_ref.dtype)


def _gated_residual(x, y_refs, z_refs, w_ref):
    acc = x
    k0 = 0
    for y_ref, z_ref in zip(y_refs, z_refs):
        z = z_ref[...].astype(jnp.float32)
        gated = y_ref[...].astype(jnp.float32) * (z / (1.0 + jnp.exp(-z)))
        kw = y_ref.shape[1]
        acc = acc + jnp.dot(gated.astype(jnp.bfloat16), w_ref[k0:k0 + kw, :],
                            preferred_element_type=jnp.float32)
        k0 += kw
    return acc


def _inproj_kernel(x_ref, g_ref, w_ref, gain_ref, cos_ref, sin_ref, o_ref, *, rope_tiles):
    _project_columns(_rms_rows(x_ref[...], g_ref), w_ref, gain_ref, cos_ref, sin_ref, o_ref, rope_tiles)


def _outproj_kernel(*refs, n_parts):
    x_ref, y_refs, z_refs = refs[0], refs[1:1 + n_parts], refs[1 + n_parts:1 + 2 * n_parts]
    w_ref, o_ref = refs[1 + 2 * n_parts:]
    o_ref[...] = _gated_residual(x_ref[...], y_refs, z_refs, w_ref)


def _out_in_kernel(*refs, n_parts, rope_tiles):
    x_ref, y_refs, z_refs = refs[0], refs[1:1 + n_parts], refs[1 + n_parts:1 + 2 * n_parts]
    w_out_ref, g_ref, w_in_ref, gain_ref, cos_ref, sin_ref, x_out_ref, proj_ref = refs[1 + 2 * n_parts:]
    x_new = _gated_residual(x_ref[...], y_refs, z_refs, w_out_ref)
    x_out_ref[...] = x_new
    _project_columns(_rms_rows(x_new, g_ref), w_in_ref, gain_ref, cos_ref, sin_ref, proj_ref, rope_tiles)


def _row_spec(width, blk=0):
    return pl.BlockSpec((PROJ_ROWS, width), lambda i, blk=blk: (i, blk))


def _whole_spec(a):
    return pl.BlockSpec(a.shape, lambda i: (0,) * a.ndim)


def _proj_params():
    return pltpu.CompilerParams(dimension_semantics=("parallel",), vmem_limit_bytes=VMEM_LIMIT)


def _inproj_operands(g, w, gain, cos, sin, seq):
    seq_tiles = seq // PROJ_ROWS
    rope_spec = pl.BlockSpec((PROJ_ROWS, LANES), lambda i: (i % seq_tiles, 0))
    return [g, w, gain, cos, sin], [_whole_spec(g), _whole_spec(w), _whole_spec(gain), rope_spec, rope_spec]


def _gated_operands(ys, zs, w, part_width):
    arrays = [a for a, _ in list(ys) + list(zs)] + [w]
    specs = [_row_spec(part_width, blk) for _, blk in list(ys) + list(zs)] + [_whole_spec(w)]
    return arrays, specs


def _inproj(x2, g, w, gain, cos, sin, rope_tiles, seq):
    rows, d = x2.shape
    arrays, specs = _inproj_operands(g, w, gain, cos, sin, seq)
    return pl.pallas_call(
        functools.partial(_inproj_kernel, rope_tiles=rope_tiles),
        grid=(rows // PROJ_ROWS,),
        in_specs=[_row_spec(d)] + specs,
        out_specs=_row_spec(w.shape[1]),
        out_shape=jax.ShapeDtypeStruct((rows, w.shape[1]), jnp.bfloat16),
        compiler_params=_proj_params(),
        name="inproj",
    )(x2, *arrays)


def _outproj(x2, ys, zs, w, part_width):
    rows, d = x2.shape
    arrays, specs = _gated_operands(ys, zs, w, part_width)
    return pl.pallas_call(
        functools.partial(_outproj_kernel, n_parts=len(ys)),
        grid=(rows // PROJ_ROWS,),
        in_specs=[_row_spec(d)] + specs,
        out_specs=_row_spec(d),
        out_shape=jax.ShapeDtypeStruct((rows, d), jnp.float32),
        compiler_params=_proj_params(),
        name="outproj",
    )(x2, *arrays)


def _outproj_inproj(x2, ys, zs, w_out, part_width, g, w_in, gain, cos, sin, rope_tiles, seq):
    rows, d = x2.shape
    out_arrays, out_specs = _gated_operands(ys, zs, w_out, part_width)
    in_arrays, in_specs = _inproj_operands(g, w_in, gain, cos, sin, seq)
    return pl.pallas_call(
        functools.partial(_out_in_kernel, n_parts=len(ys), rope_tiles=rope_tiles),
        grid=(rows // PROJ_ROWS,),
        in_specs=[_row_spec(d)] + out_specs + in_specs,
        out_specs=[_row_spec(d), _row_spec(w_in.shape[1])],
        out_shape=[jax.ShapeDtypeStruct((rows, d), jnp.float32),
                   jax.ShapeDtypeStruct((rows, w_in.shape[1]), jnp.bfloat16)],
        compiler_params=_proj_params(),
        name="outproj_inproj",
    )(x2, *out_arrays, *in_arrays)


def _fill_band_bias(bias_ref, first_key):
    kj = lax.broadcasted_iota(jnp.int32, (2 * BAND, PAIR * BAND), 0)
    qi = lax.broadcasted_iota(jnp.int32, (2 * BAND, PAIR * BAND), 1) % BAND
    band = (kj >= qi + first_key) & (kj <= qi + BAND)
    bias_ref[0] = jnp.where(band, 0.0, NEG)
    bias_ref[1] = jnp.where(band & (kj >= BAND), 0.0, NEG)


def _band_blocks(operands, sink_row):
    scores = [lax.dot_general(kw, _split_heads_rows(q), _NT, preferred_element_type=jnp.float32) + bias
              for q, kw, _, bias in operands]
    probs = []
    for s in scores:
        m = jnp.max(s, axis=0, keepdims=True)
        if sink_row is not None:
            m = jnp.maximum(m, sink_row)
        probs.append((jnp.exp2(s - m).astype(jnp.bfloat16), m))
    return [(jnp.dot(vtw, p, preferred_element_type=jnp.float32), m)
            for (_, _, vtw, _), (p, m) in zip(operands, probs)]


def _pair_tile(x):
    return jnp.concatenate([x[0:HEAD_DIM, 0:BAND], x[HEAD_DIM:LANES, BAND:PAIR * BAND]], axis=0)


def _pair_stat_tile(v):
    return jnp.concatenate([jnp.broadcast_to(v[:, 0:BAND], (HEAD_DIM, BAND)),
                            jnp.broadcast_to(v[:, BAND:PAIR * BAND], (HEAD_DIM, BAND))], axis=0)


def _attn_a_kernel(q_ref, k_ref, v_ref, o_ref, qf, kf, vf, qf4, kf4, vf4, qd, kd, vdt, bias_s, m_s, l_s, acc_s):
    seq = q_ref.shape[0]
    n_blocks = seq // BAND
    piece = 2 * BAND
    mid = MID_DILATION
    qf[...] = q_ref[...].astype(jnp.float32)
    kf[...] = k_ref[...].astype(jnp.float32)
    vf[...] = v_ref[...].astype(jnp.float32)
    for r in range(mid):
        dst = pl.ds(r * (seq // mid), seq // mid)
        src = pl.ds(r, seq // mid, stride=mid)
        qf4[dst, :] = qf[src, :]
        kf4[dst, :] = kf[src, :]
        vf4[dst, :] = vf[src, :]
    kd[0:BAND, :] = jnp.zeros((BAND, LANES), kd.dtype)
    vdt[:, 0:BAND] = jnp.zeros((LANES + SUM_ROWS, BAND), vdt.dtype)
    vdt[LANES:, BAND:] = jnp.ones((SUM_ROWS, seq), vdt.dtype)
    _fill_band_bias(bias_s, 0)

    for dil in DILATIONS:
        sub = seq // dil
        blocks_per_sub = sub // BAND
        pieces_per_sub = sub // piece

        def deinterleave(i, carry, dil=dil, pieces_per_sub=pieces_per_sub):
            dst = pl.multiple_of(i * piece, piece)
            if dil == 1:
                src = pl.ds(dst, piece)
                q_src, k_src, v_src = q_ref, k_ref, vf
            elif dil == mid:
                src = pl.ds(dst, piece)
                q_src, k_src, v_src = qf4, kf4, vf4
            else:
                r = i // pieces_per_sub
                j = i % pieces_per_sub
                step = dil // mid
                src = pl.ds((r % mid) * (seq // mid) + r // mid + j * (piece * step), piece, stride=step)
                q_src, k_src, v_src = qf4, kf4, vf4
            qd[pl.ds(dst, piece), :] = q_src[src, :].astype(qd.dtype)
            kd[pl.ds(BAND + dst, piece), :] = k_src[src, :].astype(kd.dtype)
            vdt[0:LANES, pl.ds(BAND + dst, piece)] = v_src[src, :].astype(jnp.float32).T.astype(vdt.dtype)
            return carry

        lax.fori_loop(0, seq // piece, deinterleave, 0, unroll=PREP_UNROLL)

        def body(it, carry, dil=dil, blocks_per_sub=blocks_per_sub):
            blocks = [it * DILATED_UNROLL + u for u in range(DILATED_UNROLL)]
            operands = []
            for n in blocks:
                row0 = pl.multiple_of(n * BAND, BAND)
                first = (n % blocks_per_sub) == 0
                operands.append((qd[pl.ds(row0, BAND), :], kd[pl.ds(row0, 2 * BAND), :],
                                 vdt[:, pl.ds(row0, 2 * BAND)], bias_s[jnp.where(first, 1, 0)]))
            for n, (o_t, m) in zip(blocks, _band_blocks(operands, None)):
                merge_block(n, o_t, m, dil, blocks_per_sub)
            return carry

        def merge_block(n, o_t, m, dil, blocks_per_sub):
            r = n // blocks_per_sub
            c = n % blocks_per_sub
            row0 = pl.multiple_of(n * BAND, BAND)
            acc = _pair_tile(o_t).T
            m_b = _pair_stat_tile(m).T
            l_b = _pair_stat_tile(o_t[LANES:LANES + 1, :]).T
            rows = pl.ds(row0, BAND) if dil == 1 else pl.ds(c * (BAND * dil) + r, BAND, stride=dil)
            if dil == DILATIONS[0]:
                m_s[rows, :] = m_b
                l_s[rows, :] = l_b
                acc_s[rows, :] = acc
            else:
                m_old = m_s[rows, :]
                m_new = jnp.maximum(m_old, m_b)
                a_old = jnp.exp2(m_old - m_new)
                a_new = jnp.exp2(m_b - m_new)
                m_s[rows, :] = m_new
                l_s[rows, :] = a_old * l_s[rows, :] + a_new * l_b
                acc_s[rows, :] = a_old * acc_s[rows, :] + a_new * acc

        lax.fori_loop(0, n_blocks // DILATED_UNROLL, body, 0)

    o_ref[...] = (acc_s[...] / l_s[...]).astype(o_ref.dtype)


def _attn_a(proj, batch, seq):
    n_pairs = A_WIDTH // LANES

    def col_spec(group):
        return pl.BlockSpec((seq, LANES), lambda b, p, group=group: (b, group * n_pairs + p))

    f32 = jnp.float32
    bf16 = jnp.bfloat16
    return pl.pallas_call(
        _attn_a_kernel,
        grid=(batch, n_pairs),
        in_specs=[col_spec(0), col_spec(1), col_spec(2)],
        out_specs=pl.BlockSpec((seq, LANES), lambda b, p: (b, p)),
        out_shape=jax.ShapeDtypeStruct((batch * seq, A_WIDTH), bf16),
        scratch_shapes=[
            pltpu.VMEM((seq, LANES), f32), pltpu.VMEM((seq, LANES), f32), pltpu.VMEM((seq, LANES), f32),
            pltpu.VMEM((seq, LANES), f32), pltpu.VMEM((seq, LANES), f32), pltpu.VMEM((seq, LANES), f32),
            pltpu.VMEM((seq, LANES), bf16),
            pltpu.VMEM((seq + BAND, LANES), bf16),
            pltpu.VMEM((LANES + SUM_ROWS, seq + BAND), bf16),
            pltpu.VMEM((2, 2 * BAND, PAIR * BAND), f32),
            pltpu.VMEM((seq, LANES), f32), pltpu.VMEM((seq, LANES), f32), pltpu.VMEM((seq, LANES), f32),
        ],
        compiler_params=pltpu.CompilerParams(
            dimension_semantics=("parallel", "parallel"), vmem_limit_bytes=VMEM_LIMIT),
        name="attn_dilated",
    )(proj, proj, proj)


def _attn_c_kernel(sinks_ref, q_ref, k_ref, v_ref, o_ref, kd, vdt, bias_s):
    seq = q_ref.shape[0]
    pairs_per_kv = q_ref.shape[1] // LANES
    piece = 2 * BAND
    kd[0:BAND, :] = jnp.zeros((BAND, LANES), kd.dtype)
    kd[BAND:, :] = k_ref[...]
    vdt[:, 0:BAND] = jnp.zeros((LANES + SUM_ROWS, BAND), vdt.dtype)
    vdt[LANES:, BAND:] = jnp.ones((SUM_ROWS, seq), vdt.dtype)

    def transpose_v(i, carry):
        rows = pl.ds(pl.multiple_of(i * piece, piece), piece)
        vdt[0:LANES, pl.ds(pl.multiple_of(BAND + i * piece, BAND), piece)] = (
            v_ref[rows, :].astype(jnp.float32).T.astype(vdt.dtype))
        return carry

    lax.fori_loop(0, seq // piece, transpose_v, 0, unroll=PREP_UNROLL)
    _fill_band_bias(bias_s, 1)
    col = lax.broadcasted_iota(jnp.int32, (1, PAIR * BAND), 1)
    iters_per_pair = seq // (BAND * SWA_UNROLL)

    def body(step, carry):
        pair = step // iters_per_pair
        it = step % iters_per_pair
        lanes = pl.ds(pl.multiple_of(pair * LANES, LANES), LANES)
        head0 = PAIR * (pl.program_id(1) * pairs_per_kv + pair)
        sink_row = jnp.where(col < BAND, sinks_ref[head0], sinks_ref[head0 + 1]) * LOG2E
        blocks = [it * SWA_UNROLL + u for u in range(SWA_UNROLL)]
        operands = []
        for n in blocks:
            row0 = pl.multiple_of(n * BAND, BAND)
            operands.append((q_ref[pl.ds(row0, BAND), lanes], kd[pl.ds(row0, 2 * BAND), :],
                             vdt[:, pl.ds(row0, 2 * BAND)], bias_s[jnp.where(n == 0, 1, 0)]))
        for n, (o_t, m) in zip(blocks, _band_blocks(operands, sink_row)):
            l = o_t[LANES:LANES + 1, :] + jnp.exp2(sink_row - m)
            out_t = _pair_tile(o_t) * _pair_stat_tile(1.0 / l)
            o_ref[pl.ds(pl.multiple_of(n * BAND, BAND), BAND), lanes] = out_t.T.astype(o_ref.dtype)
        return carry

    lax.fori_loop(0, pairs_per_kv * iters_per_pair, body, 0)


def _attn_c(proj, sinks, batch, seq):
    group_w = C_GROUP * HEAD_DIM
    k_tile0 = C_WIDTH // LANES
    v_tile0 = k_tile0 + C_KV_HEADS
    bf16 = jnp.bfloat16
    return pl.pallas_call(
        _attn_c_kernel,
        grid=(batch, C_KV_HEADS),
        in_specs=[
            pl.BlockSpec(memory_space=pltpu.SMEM),
            pl.BlockSpec((seq, group_w), lambda b, g: (b, g)),
            pl.BlockSpec((seq, LANES), lambda b, g: (b, k_tile0 + g)),
            pl.BlockSpec((seq, LANES), lambda b, g: (b, v_tile0 + g)),
        ],
        out_specs=pl.BlockSpec((seq, group_w), lambda b, g: (b, g)),
        out_shape=jax.ShapeDtypeStruct((batch * seq, C_WIDTH), bf16),
        scratch_shapes=[
            pltpu.VMEM((seq + BAND, LANES), bf16),
            pltpu.VMEM((LANES + SUM_ROWS, seq + BAND), bf16),
            pltpu.VMEM((2, 2 * BAND, PAIR * BAND), jnp.float32),
        ],
        compiler_params=pltpu.CompilerParams(
            dimension_semantics=("parallel", "parallel"), vmem_limit_bytes=VMEM_LIMIT),
        name="attn_swa",
    )(sinks, proj, proj, proj)


_T_TILE, _T_CH, _T_FIRST, _T_ROWS = range(4)


def _moba_step_table(n_tiles):
    tiles, chs, firsts = [], [], []
    for tile in range(n_tiles):
        for i, ch in enumerate([tile] + list(range(tile))):
            tiles.append(tile)
            chs.append(ch)
            firsts.append(int(i == 0))
    n_steps = len(tiles)
    rows = [None] * _T_ROWS
    rows[_T_TILE] = tiles + [tiles[-1]]
    rows[_T_CH] = chs + [chs[-1]]
    rows[_T_FIRST] = firsts + [0]
    return n_steps, [v for row in rows for v in row]


def _moba_kernel(tbl_ref, q_ref, k_ref, v_ref, o_ref, vt, q2_all, kmean, bias_s, causal_s, m_s, acc_s,
                 s_a, s_b, *, n_steps):
    seq = q_ref.shape[0]
    nb = seq // MOBA_BLOCK
    nq2 = PAIR * MOBA_BLOCK
    nq = MOBA_CHUNK_BLOCKS * nq2
    n_tiles = nb // MOBA_CHUNK_BLOCKS
    chunk = MOBA_CHUNK_BLOCKS * MOBA_BLOCK

    def tbl(row, t):
        return tbl_ref[row * (n_steps + 1) + t]

    def prep(j, carry):
        rows = pl.ds(pl.multiple_of(j * MOBA_BLOCK, MOBA_BLOCK), MOBA_BLOCK)
        v_t = v_ref[rows, :].astype(jnp.float32).T.astype(vt.dtype)
        for h in range(PAIR):
            vt[h, 0:HEAD_DIM, rows] = v_t[h * HEAD_DIM:(h + 1) * HEAD_DIM]
        kmean[pl.ds(j, 1), :] = jnp.mean(k_ref[rows, :].astype(jnp.float32), axis=0, keepdims=True)
        q2_all[pl.ds(pl.multiple_of(j * nq2, nq2), nq2), :] = _split_heads_rows(q_ref[rows, :])
        return carry

    lax.fori_loop(0, nb, prep, 0, unroll=PREP_UNROLL)
    for h in range(PAIR):
        vt[h, HEAD_DIM:, :] = jnp.ones((SUM_ROWS, seq), vt.dtype)
    km = kmean[...]
    km_hi = km.astype(jnp.bfloat16)
    km_hilo = jnp.concatenate([km_hi, (km - km_hi.astype(jnp.float32)).astype(jnp.bfloat16)], axis=0)
    blk_id = lax.broadcasted_iota(jnp.int32, (nb, nq2), 0)
    key_id = lax.broadcasted_iota(jnp.int32, (chunk, nq), 0)
    col_id = lax.broadcasted_iota(jnp.int32, (chunk, nq), 1)
    causal_s[...] = jnp.where(key_id <= col_id % MOBA_BLOCK + (col_id // nq2) * MOBA_BLOCK, 0.0, NEG)

    def chunk_rows(ch):
        return pl.ds(pl.multiple_of(ch * chunk, chunk), chunk)

    def scores_t(t):
        q2 = q2_all[pl.ds(pl.multiple_of(tbl(_T_TILE, t) * nq, nq), nq), :]
        return lax.dot_general(k_ref[chunk_rows(tbl(_T_CH, t)), :], q2, _NT,
                               preferred_element_type=jnp.float32)

    def pv_t(t, p):
        p = p.astype(vt.dtype)
        rows = chunk_rows(tbl(_T_CH, t))
        return jnp.concatenate(
            [jnp.dot(vt[g % PAIR, :, rows], p[:, g * MOBA_BLOCK:(g + 1) * MOBA_BLOCK],
                     preferred_element_type=jnp.float32) for g in range(nq // MOBA_BLOCK)], axis=1)

    def block_bias(t):
        row0 = tbl(_T_TILE, t) * nb + tbl(_T_CH, t) * MOBA_CHUNK_BLOCKS
        return [bias_s[pl.ds(row0 + i, 1), :] for i in range(MOBA_CHUNK_BLOCKS)]

    def per_block(x, fn):
        return [fn(x[i * MOBA_BLOCK:(i + 1) * MOBA_BLOCK], i) for i in range(MOBA_CHUNK_BLOCKS)]

    def select_blocks(tile, carry):
        for i in range(MOBA_CHUNK_BLOCKS):
            qb = tile * MOBA_CHUNK_BLOCKS + i
            q2 = q2_all[pl.ds(pl.multiple_of(qb * nq2, nq2), nq2), :]
            g2 = lax.dot_general(km_hilo, q2, _NT, preferred_element_type=jnp.float32)
            past = blk_id < qb
            gate = jnp.where(past, g2[0:nb] + g2[nb:2 * nb], NEG)
            rank = jnp.zeros((nb, nq2), jnp.int32)
            for j in range(nb - 1):
                gj = gate[j:j + 1, :]
                tie_ref = jnp.where(blk_id > j, gate, NO_TIE)
                rank = rank + jnp.where(gj > gate, 1, 0) + jnp.where(gj == tie_ref, 1, 0)
            allowed = (past & (rank < MOBA_TOPK)) | (blk_id == qb)
            bias_s[pl.ds(pl.multiple_of(tile * nb, nb), nb), i * nq2:(i + 1) * nq2] = jnp.where(allowed, 0.0, NEG)
        return carry

    lax.fori_loop(0, n_tiles, select_blocks, 0, unroll=2)

    def attend_first(t, s_cur):
        bias = block_bias(t)

        def left_pad(x, i, fill):
            return x if i == 0 else jnp.concatenate([jnp.full((x.shape[0], i * nq2), fill, x.dtype), x], axis=1)

        parts = []
        for i in range(MOBA_CHUNK_BLOCKS):
            rows, cols = slice(i * MOBA_BLOCK, (i + 1) * MOBA_BLOCK), slice(i * nq2, nq)
            parts.append(s_cur[rows, cols] + causal_s[rows, cols] + bias[i][:, cols])
        m0 = functools.reduce(jnp.maximum, [left_pad(jnp.max(x, axis=0, keepdims=True), i, NEG)
                                            for i, x in enumerate(parts)])
        p = jnp.concatenate([left_pad(jnp.exp2(x - m0[:, i * nq2:]), i, 0.0) for i, x in enumerate(parts)], axis=0)
        m_s[...] = m0
        acc_s[...] = pv_t(t, p)

    def attend_next(t, s_cur):
        s = s_cur[...]
        bias = block_bias(t)
        m_old = m_s[...]
        m_blk = per_block(s, lambda x, i: jnp.max(x, axis=0, keepdims=True) + bias[i])
        m_new = functools.reduce(jnp.maximum, m_blk, m_old)
        alpha = jnp.exp2(m_old - m_new)
        p = jnp.concatenate(per_block(s, lambda x, i: jnp.exp2(x - (m_new - bias[i]))), axis=0)
        m_s[...] = m_new
        acc_s[...] = alpha * acc_s[...] + pv_t(t, p)

    def finalize(tile):
        o_all = acc_s[0:HEAD_DIM, :] * (1.0 / acc_s[HEAD_DIM:HEAD_DIM + 1, :])
        for i in range(MOBA_CHUNK_BLOCKS):
            c0 = i * nq2
            o_t = jnp.concatenate([o_all[:, c0:c0 + MOBA_BLOCK], o_all[:, c0 + MOBA_BLOCK:c0 + nq2]],
                                  axis=0)
            rows = pl.ds(pl.multiple_of((tile * MOBA_CHUNK_BLOCKS + i) * MOBA_BLOCK, MOBA_BLOCK), MOBA_BLOCK)
            o_ref[rows, :] = o_t.T.astype(o_ref.dtype)

    def step(t, s_cur, s_nxt):
        def first():
            finalize(jnp.maximum(tbl(_T_TILE, t) - 1, 0))
            s_nxt[...] = scores_t(t + 1)
            attend_first(t, s_cur)

        def later():
            s_nxt[...] = scores_t(t + 1)
            attend_next(t, s_cur)

        lax.cond(tbl(_T_FIRST, t) == 1, first, later)

    s_a[...] = scores_t(0)
    acc_s[...] = jnp.ones_like(acc_s)

    def two_steps(i, carry):
        step(2 * i, s_a, s_b)
        step(2 * i + 1, s_b, s_a)
        return carry

    lax.fori_loop(0, n_steps // 2, two_steps, 0)
    finalize(n_tiles - 1)


def _attn_b(proj, batch, seq):
    n_pairs = B_WIDTH // LANES
    tile0 = 4 * (A_WIDTH // LANES)

    def col_spec(group):
        return pl.BlockSpec((seq, LANES), lambda b, p, group=group: (b, tile0 + group * n_pairs + p))

    nb = seq // MOBA_BLOCK
    nq2 = PAIR * MOBA_BLOCK
    chunk = MOBA_CHUNK_BLOCKS * MOBA_BLOCK
    assert nb % MOBA_CHUNK_BLOCKS == 0
    n_tiles = nb // MOBA_CHUNK_BLOCKS
    nq = MOBA_CHUNK_BLOCKS * nq2
    n_steps, table = _moba_step_table(n_tiles)
    assert n_steps % 2 == 0
    f32 = jnp.float32
    bf16 = jnp.bfloat16
    return pl.pallas_call(
        functools.partial(_moba_kernel, n_steps=n_steps),
        grid=(batch, n_pairs),
        in_specs=[pl.BlockSpec(memory_space=pltpu.SMEM), col_spec(0), col_spec(1), col_spec(2)],
        out_specs=pl.BlockSpec((seq, LANES), lambda b, p: (b, p)),
        out_shape=jax.ShapeDtypeStruct((batch * seq, B_WIDTH), bf16),
        scratch_shapes=[
            pltpu.VMEM((PAIR, HEAD_DIM + SUM_ROWS, seq), bf16),
            pltpu.VMEM((nb * nq2, LANES), bf16),
            pltpu.VMEM((nb, LANES), f32),
            pltpu.VMEM((n_tiles * nb, nq), f32),
            pltpu.VMEM((chunk, nq), f32),
            pltpu.VMEM((1, nq), f32),
            pltpu.VMEM((HEAD_DIM + SUM_ROWS, nq), f32),
            pltpu.VMEM((chunk, nq), f32), pltpu.VMEM((chunk, nq), f32),
        ],
        compiler_params=pltpu.CompilerParams(
            dimension_semantics=("parallel", "parallel"), vmem_limit_bytes=VMEM_LIMIT),
        name="attn_moba",
    )(jnp.asarray(table, jnp.int32), proj, proj, proj)


def _rope_tables(seq):
    pos = jnp.arange(seq, dtype=jnp.float32)
    inv_freq = ROPE_THETA ** (-jnp.arange(0, HEAD_DIM, 2, dtype=jnp.float32) / HEAD_DIM)
    ang = pos[:, None] * inv_freq[None, :]
    cos = jnp.concatenate([jnp.cos(ang)] * (2 * PAIR), axis=-1)
    sin = jnp.concatenate([-jnp.sin(ang)] * PAIR + [jnp.sin(ang)] * PAIR, axis=-1)
    return cos, sin


def _tile_gain(g, width):
    return _qk_columns(jnp.tile(g, width // HEAD_DIM)[None, :])[0]


def _qk_columns(w):
    lead = w.shape[:-1]
    tiles = w.reshape(lead + (w.shape[-1] // LANES, PAIR, 2, HALF_DIM))
    return jnp.swapaxes(tiles, -3, -2).reshape(w.shape)


def kernel(x, norm_even, w_in_even, w_out_even, qnorm_a, knorm_a, qnorm_b, knorm_b,
           norm_odd, w_in_odd, w_out_odd, qnorm_c, knorm_c, sinks_c):
    batch, seq, d = x.shape
    bf16 = jnp.bfloat16
    scale = HEAD_DIM ** -0.5 * LOG2E
    cos, sin = _rope_tables(seq)
    x2 = x.reshape(batch * seq, d)

    ones = jnp.ones((A_WIDTH,), jnp.float32)
    gain0 = jnp.concatenate([
        _tile_gain(qnorm_a[0], A_WIDTH) * scale, _tile_gain(knorm_a[0], A_WIDTH), ones, ones,
        _tile_gain(qnorm_b[0], B_WIDTH) * scale, _tile_gain(knorm_b[0], B_WIDTH), ones, ones])[None, :]
    tiles_per_group = A_WIDTH // LANES
    rope0 = frozenset(t for grp in (0, 1, 4, 5) for t in range(grp * tiles_per_group, (grp + 1) * tiles_per_group))
    w0 = w_in_even[0]
    groups = [w0[:, g * A_WIDTH:(g + 1) * A_WIDTH] for g in range(8)]
    w0r = jnp.concatenate([_qk_columns(w) if g in (0, 1, 4, 5) else w for g, w in enumerate(groups)], axis=1)
    proj0 = _inproj(x2, norm_even[0][None, :], w0r.astype(bf16), gain0, cos, sin, rope0, seq)
    oa = _attn_a(proj0, batch, seq)
    ob = _attn_b(proj0, batch, seq)

    w1 = w_in_odd[0]
    kv_w = C_KV_HEADS * HEAD_DIM
    wq, wk, wv, wz = (w1[:, :C_WIDTH], w1[:, C_WIDTH:C_WIDTH + kv_w],
                      w1[:, C_WIDTH + kv_w:C_WIDTH + 2 * kv_w], w1[:, C_WIDTH + 2 * kv_w:])

    def dup_heads(w):
        return jnp.concatenate([w[:, h * HEAD_DIM:(h + 1) * HEAD_DIM]
                                for h in range(C_KV_HEADS) for _ in range(PAIR)], axis=1)

    w1r = jnp.concatenate([_qk_columns(wq), _qk_columns(dup_heads(wk)), dup_heads(wv), wz],
                          axis=1).astype(bf16)
    dup_w = PAIR * kv_w
    gain1 = jnp.concatenate([
        _tile_gain(qnorm_c[0], C_WIDTH) * scale, _tile_gain(knorm_c[0], dup_w),
        jnp.ones((dup_w + C_WIDTH,), jnp.float32)])[None, :]
    rope1 = frozenset(range((C_WIDTH + dup_w) // LANES))
    x1, proj1 = _outproj_inproj(x2, [(oa, 0), (ob, 0)], [(proj0, 3), (proj0, 7)], w_out_even[0].astype(bf16),
                                A_WIDTH, norm_odd[0][None, :], w1r, gain1, cos, sin, rope1, seq)
    oc = _attn_c(proj1, sinks_c[0], batch, seq)
    half = C_WIDTH // 2
    z_blk = (C_WIDTH + 2 * dup_w) // half
    out = _outproj(x1, [(oc, 0), (oc, 1)], [(proj1, z_blk), (proj1, z_blk + 1)],
                   w_out_odd[0].astype(bf16), half)
    return out.reshape(batch, seq, d)
```

```python
import functools

import jax
import jax.numpy as jnp
from jax import lax
from jax.experimental import pallas as pl
from jax.experimental.pallas import tpu as pltpu

D_MODEL = 1024
HEAD_DIM = 64
LANES = 128
PAIR = LANES // HEAD_DIM
HALF_DIM = HEAD_DIM // 2
NORM_COLS = 2 * LANES
A_WIDTH = 512
B_WIDTH = 512
C_WIDTH = 1024
C_KV_HEADS = 2
C_GROUP = 8
DILATIONS = (16, 4, 1)
MID_DILATION = 4
BAND = 128
MOBA_BLOCK = 256
MOBA_TOPK = 3
MOBA_CHUNK_BLOCKS = 2
DILATED_UNROLL = 8
SWA_UNROLL = 32
PREP_UNROLL = 4
SUM_ROWS = 16
LOG2E = 1.4426950408889634
ROPE_THETA = 10000.0
NORM_EPS = 1e-6
NEG = -1e30
NO_TIE = 3e38
VMEM_LIMIT = 56 * 1024 * 1024
PROJ_ROWS = 512
PROJ_COLS = 512

_NT = (((1,), (1,)), ((), ()))


def _lane_ids(shape):
    return lax.broadcasted_iota(jnp.int32, shape, len(shape) - 1)


def _qk_head_of_lane(lane):
    return (lane // HALF_DIM) % PAIR


def _split_heads_rows(t):
    head = _qk_head_of_lane(_lane_ids(t.shape))
    zero = jnp.zeros_like(t)
    return jnp.concatenate([jnp.where(head == 0, t, zero), jnp.where(head == 1, t, zero)], axis=0)


def _rms_rows(x, g_ref):
    ms = jnp.mean(x * x, axis=-1, keepdims=True)
    return (x * lax.rsqrt(ms + NORM_EPS) * g_ref[...]).astype(jnp.bfloat16)


def _project_columns(h, w_ref, gain_ref, cos_ref, sin_ref, o_ref, rope_tiles):
    n_cols = o_ref.shape[1]
    li = lax.broadcasted_iota(jnp.int32, (NORM_COLS, NORM_COLS), 0)
    lj = lax.broadcasted_iota(jnp.int32, (NORM_COLS, NORM_COLS), 1)
    same_head = ((li // LANES == lj // LANES)
                 & (_qk_head_of_lane(li) == _qk_head_of_lane(lj))).astype(jnp.bfloat16)
    for c0 in range(0, n_cols, PROJ_COLS):
        r = jnp.dot(h, w_ref[:, c0:c0 + PROJ_COLS], preferred_element_type=jnp.float32)
        for p0 in range(0, PROJ_COLS, NORM_COLS):
            col = c0 + p0
            tiles = range(col // LANES, (col + NORM_COLS) // LANES)
            t = r[:, p0:p0 + NORM_COLS]
            if all(tile in rope_tiles for tile in tiles):
                ss = jnp.dot((t * t).astype(jnp.bfloat16), same_head, preferred_element_type=jnp.float32)
                t = t * lax.rsqrt(ss * (1.0 / HEAD_DIM) + NORM_EPS) * gain_ref[:, col:col + NORM_COLS]
                for t0 in range(0, NORM_COLS, LANES):
                    tn = t[:, t0:t0 + LANES]
                    tn = tn * cos_ref[...] + pltpu.roll(tn, LANES // 2, 1) * sin_ref[...]
                    o_ref[:, col + t0:col + t0 + LANES] = tn.astype(o_ref.dtype)
            else:
                assert not any(tile in rope_tiles for tile in tiles)
                o_ref[:, col:col + NORM_COLS] = t.astype(o_ref.dtype)


def _gated_residual(x, y_refs, z_refs, w_ref):
    acc = x
    k0 = 0
    for y_ref, z_ref in zip(y_refs, z_refs):
        z = z_ref[...].astype(jnp.float32)
        gated = y_ref[...].astype(jnp.float32) * (z / (1.0 + jnp.exp(-z)))
        kw = y_ref.shape[1]
        acc = acc + jnp.dot(gated.astype(jnp.bfloat16), w_ref[k0:k0 + kw, :],
                            preferred_element_type=jnp.float32)
        k0 += kw
    return acc


def _inproj_kernel(x_ref, g_ref, w_ref, gain_ref, cos_ref, sin_ref, o_ref, *, rope_tiles):
    _project_columns(_rms_rows(x_ref[...], g_ref), w_ref, gain_ref, cos_ref, sin_ref, o_ref, rope_tiles)


def _outproj_kernel(*refs, n_parts):
    x_ref, y_refs, z_refs = refs[0], refs[1:1 + n_parts], refs[1 + n_parts:1 + 2 * n_parts]
    w_ref, o_ref = refs[1 + 2 * n_parts:]
    o_ref[...] = _gated_residual(x_ref[...], y_refs, z_refs, w_ref)


def _out_in_kernel(*refs, n_parts, rope_tiles):
    x_ref, y_refs, z_refs = refs[0], refs[1:1 + n_parts], refs[1 + n_parts:1 + 2 * n_parts]
    w_out_ref, g_ref, w_in_ref, gain_ref, cos_ref, sin_ref, x_out_ref, proj_ref = refs[1 + 2 * n_parts:]
    x_new = _gated_residual(x_ref[...], y_refs, z_refs, w_out_ref)
    x_out_ref[...] = x_new
    _project_columns(_rms_rows(x_new, g_ref), w_in_ref, gain_ref, cos_ref, sin_ref, proj_ref, rope_tiles)


def _row_spec(width, blk=0):
    return pl.BlockSpec((PROJ_ROWS, width), lambda i, blk=blk: (i, blk))


def _whole_spec(a):
    return pl.BlockSpec(a.shape, lambda i: (0,) * a.ndim)


def _proj_params():
    return pltpu.CompilerParams(dimension_semantics=("parallel",), vmem_limit_bytes=VMEM_LIMIT)


def _inproj_operands(g, w, gain, cos, sin, seq):
    seq_tiles = seq // PROJ_ROWS
    rope_spec = pl.BlockSpec((PROJ_ROWS, LANES), lambda i: (i % seq_tiles, 0))
    return [g, w, gain, cos, sin], [_whole_spec(g), _whole_spec(w), _whole_spec(gain), rope_spec, rope_spec]


def _gated_operands(ys, zs, w, part_width):
    arrays = [a for a, _ in list(ys) + list(zs)] + [w]
    specs = [_row_spec(part_width, blk) for _, blk in list(ys) + list(zs)] + [_whole_spec(w)]
    return arrays, specs


def _inproj(x2, g, w, gain, cos, sin, rope_tiles, seq):
    rows, d = x2.shape
    arrays, specs = _inproj_operands(g, w, gain, cos, sin, seq)
    return pl.pallas_call(
        functools.partial(_inproj_kernel, rope_tiles=rope_tiles),
        grid=(rows // PROJ_ROWS,),
        in_specs=[_row_spec(d)] + specs,
        out_specs=_row_spec(w.shape[1]),
        out_shape=jax.ShapeDtypeStruct((rows, w.shape[1]), jnp.bfloat16),
        compiler_params=_proj_params(),
        name="inproj",
    )(x2, *arrays)


def _outproj(x2, ys, zs, w, part_width):
    rows, d = x2.shape
    arrays, specs = _gated_operands(ys, zs, w, part_width)
    return pl.pallas_call(
        functools.partial(_outproj_kernel, n_parts=len(ys)),
        grid=(rows // PROJ_ROWS,),
        in_specs=[_row_spec(d)] + specs,
        out_specs=_row_spec(d),
        out_shape=jax.ShapeDtypeStruct((rows, d), jnp.float32),
        compiler_params=_proj_params(),
        name="outproj",
    )(x2, *arrays)


def _outproj_inproj(x2, ys, zs, w_out, part_width, g, w_in, gain, cos, sin, rope_tiles, seq):
    rows, d = x2.shape
    out_arrays, out_specs = _gated_operands(ys, zs, w_out, part_width)
    in_arrays, in_specs = _inproj_operands(g, w_in, gain, cos, sin, seq)
    return pl.pallas_call(
        functools.partial(_out_in_kernel, n_parts=len(ys), rope_tiles=rope_tiles),
        grid=(rows // PROJ_ROWS,),
        in_specs=[_row_spec(d)] + out_specs + in_specs,
        out_specs=[_row_spec(d), _row_spec(w_in.shape[1])],
        out_shape=[jax.ShapeDtypeStruct((rows, d), jnp.float32),
                   jax.ShapeDtypeStruct((rows, w_in.shape[1]), jnp.bfloat16)],
        compiler_params=_proj_params(),
        name="outproj_inproj",
    )(x2, *out_arrays, *in_arrays)


def _fill_band_bias(bias_ref, first_key):
    kj = lax.broadcasted_iota(jnp.int32, (2 * BAND, PAIR * BAND), 0)
    qi = lax.broadcasted_iota(jnp.int32, (2 * BAND, PAIR * BAND), 1) % BAND
    band = (kj >= qi + first_key) & (kj <= qi + BAND)
    bias_ref[0] = jnp.where(band, 0.0, NEG)
    bias_ref[1] = jnp.where(band & (kj >= BAND), 0.0, NEG)


def _band_blocks(operands, sink_row):
    scores = [lax.dot_general(kw, _split_heads_rows(q), _NT, preferred_element_type=jnp.float32) + bias
              for q, kw, _, bias in operands]
    probs = []
    for s in scores:
        m = jnp.max(s, axis=0, keepdims=True)
        if sink_row is not None:
            m = jnp.maximum(m, sink_row)
        probs.append((jnp.exp2(s - m).astype(jnp.bfloat16), m))
    return [(jnp.dot(vtw, p, preferred_element_type=jnp.float32), m)
            for (_, _, vtw, _), (p, m) in zip(operands, probs)]


def _pair_tile(x):
    return jnp.concatenate([x[0:HEAD_DIM, 0:BAND], x[HEAD_DIM:LANES, BAND:PAIR * BAND]], axis=0)


def _pair_stat_tile(v):
    return jnp.concatenate([jnp.broadcast_to(v[:, 0:BAND], (HEAD_DIM, BAND)),
                            jnp.broadcast_to(v[:, BAND:PAIR * BAND], (HEAD_DIM, BAND))], axis=0)


def _attn_a_kernel(q_ref, k_ref, v_ref, o_ref, qf, kf, vf, qf4, kf4, vf4, qd, kd, vdt, bias_s, m_s, l_s, acc_s):
    seq = q_ref.shape[0]
    n_blocks = seq // BAND
    piece = 2 * BAND
    mid = MID_DILATION
    qf[...] = q_ref[...].astype(jnp.float32)
    kf[...] = k_ref[...].astype(jnp.float32)
    vf[...] = v_ref[...].astype(jnp.float32)
    for r in range(mid):
        dst = pl.ds(r * (seq // mid), seq // mid)
        src = pl.ds(r, seq // mid, stride=mid)
        qf4[dst, :] = qf[src, :]
        kf4[dst, :] = kf[src, :]
        vf4[dst, :] = vf[src, :]
    kd[0:BAND, :] = jnp.zeros((BAND, LANES), kd.dtype)
    vdt[:, 0:BAND] = jnp.zeros((LANES + SUM_ROWS, BAND), vdt.dtype)
    vdt[LANES:, BAND:] = jnp.ones((SUM_ROWS, seq), vdt.dtype)
    _fill_band_bias(bias_s, 0)

    for dil in DILATIONS:
        sub = seq // dil
        blocks_per_sub = sub // BAND
        pieces_per_sub = sub // piece

        def deinterleave(i, carry, dil=dil, pieces_per_sub=pieces_per_sub):
            dst = pl.multiple_of(i * piece, piece)
            if dil == 1:
                src = pl.ds(dst, piece)
                q_src, k_src, v_src = q_ref, k_ref, vf
            elif dil == mid:
                src = pl.ds(dst, piece)
                q_src, k_src, v_src = qf4, kf4, vf4
            else:
                r = i // pieces_per_sub
                j = i % pieces_per_sub
                step = dil // mid
                src = pl.ds((r % mid) * (seq // mid) + r // mid + j * (piece * step), piece, stride=step)
                q_src, k_src, v_src = qf4, kf4, vf4
            qd[pl.ds(dst, piece), :] = q_src[src, :].astype(qd.dtype)
            kd[pl.ds(BAND + dst, piece), :] = k_src[src, :].astype(kd.dtype)
            vdt[0:LANES, pl.ds(BAND + dst, piece)] = v_src[src, :].astype(jnp.float32).T.astype(vdt.dtype)
            return carry

        lax.fori_loop(0, seq // piece, deinterleave, 0, unroll=PREP_UNROLL)

        def body(it, carry, dil=dil, blocks_per_sub=blocks_per_sub):
            blocks = [it * DILATED_UNROLL + u for u in range(DILATED_UNROLL)]
            operands = []
            for n in blocks:
                row0 = pl.multiple_of(n * BAND, BAND)
                first = (n % blocks_per_sub) == 0
                operands.append((qd[pl.ds(row0, BAND), :], kd[pl.ds(row0, 2 * BAND), :],
                                 vdt[:, pl.ds(row0, 2 * BAND)], bias_s[jnp.where(first, 1, 0)]))
            for n, (o_t, m) in zip(blocks, _band_blocks(operands, None)):
                merge_block(n, o_t, m, dil, blocks_per_sub)
            return carry

        def merge_block(n, o_t, m, dil, blocks_per_sub):
            r = n // blocks_per_sub
            c = n % blocks_per_sub
            row0 = pl.multiple_of(n * BAND, BAND)
            acc = _pair_tile(o_t).T
            m_b = _pair_stat_tile(m).T
            l_b = _pair_stat_tile(o_t[LANES:LANES + 1, :]).T
            rows = pl.ds(row0, BAND) if dil == 1 else pl.ds(c * (BAND * dil) + r, BAND, stride=dil)
            if dil == DILATIONS[0]:
                m_s[rows, :] = m_b
                l_s[rows, :] = l_b
                acc_s[rows, :] = acc
            else:
                m_old = m_s[rows, :]
                m_new = jnp.maximum(m_old, m_b)
                a_old = jnp.exp2(m_old - m_new)
                a_new = jnp.exp2(m_b - m_new)
                m_s[rows, :] = m_new
                l_s[rows, :] = a_old * l_s[rows, :] + a_new * l_b
                acc_s[rows, :] = a_old * acc_s[rows, :] + a_new * acc

        lax.fori_loop(0, n_blocks // DILATED_UNROLL, body, 0)

    o_ref[...] = (acc_s[...] / l_s[...]).astype(o_ref.dtype)


def _attn_a(proj, batch, seq):
    n_pairs = A_WIDTH // LANES

    def col_spec(group):
        return pl.BlockSpec((seq, LANES), lambda b, p, group=group: (b, group * n_pairs + p))

    f32 = jnp.float32
    bf16 = jnp.bfloat16
    return pl.pallas_call(
        _attn_a_kernel,
        grid=(batch, n_pairs),
        in_specs=[col_spec(0), col_spec(1), col_spec(2)],
        out_specs=pl.BlockSpec((seq, LANES), lambda b, p: (b, p)),
        out_shape=jax.ShapeDtypeStruct((batch * seq, A_WIDTH), bf16),
        scratch_shapes=[
            pltpu.VMEM((seq, LANES), f32), pltpu.VMEM((seq, LANES), f32), pltpu.VMEM((seq, LANES), f32),
            pltpu.VMEM((seq, LANES), f32), pltpu.VMEM((seq, LANES), f32), pltpu.VMEM((seq, LANES), f32),
            pltpu.VMEM((seq, LANES), bf16),
            pltpu.VMEM((seq + BAND, LANES), bf16),
            pltpu.VMEM((LANES + SUM_ROWS, seq + BAND), bf16),
            pltpu.VMEM((2, 2 * BAND, PAIR * BAND), f32),
            pltpu.VMEM((seq, LANES), f32), pltpu.VMEM((seq, LANES), f32), pltpu.VMEM((seq, LANES), f32),
        ],
        compiler_params=pltpu.CompilerParams(
            dimension_semantics=("parallel", "parallel"), vmem_limit_bytes=VMEM_LIMIT),
        name="attn_dilated",
    )(proj, proj, proj)


def _attn_c_kernel(sinks_ref, q_ref, k_ref, v_ref, o_ref, kd, vdt, bias_s):
    seq = q_ref.shape[0]
    pairs_per_kv = q_ref.shape[1] // LANES
    piece = 2 * BAND
    kd[0:BAND, :] = jnp.zeros((BAND, LANES), kd.dtype)
    kd[BAND:, :] = k_ref[...]
    vdt[:, 0:BAND] = jnp.zeros((LANES + SUM_ROWS, BAND), vdt.dtype)
    vdt[LANES:, BAND:] = jnp.ones((SUM_ROWS, seq), vdt.dtype)

    def transpose_v(i, carry):
        rows = pl.ds(pl.multiple_of(i * piece, piece), piece)
        vdt[0:LANES, pl.ds(pl.multiple_of(BAND + i * piece, BAND), piece)] = (
            v_ref[rows, :].astype(jnp.float32).T.astype(vdt.dtype))
        return carry

    lax.fori_loop(0, seq // piece, transpose_v, 0, unroll=PREP_UNROLL)
    _fill_band_bias(bias_s, 1)
    col = lax.broadcasted_iota(jnp.int32, (1, PAIR * BAND), 1)
    iters_per_pair = seq // (BAND * SWA_UNROLL)

    def body(step, carry):
        pair = step // iters_per_pair
        it = step % iters_per_pair
        lanes = pl.ds(pl.multiple_of(pair * LANES, LANES), LANES)
        head0 = PAIR * (pl.program_id(1) * pairs_per_kv + pair)
        sink_row = jnp.where(col < BAND, sinks_ref[head0], sinks_ref[head0 + 1]) * LOG2E
        blocks = [it * SWA_UNROLL + u for u in range(SWA_UNROLL)]
        operands = []
        for n in blocks:
            row0 = pl.multiple_of(n * BAND, BAND)
            operands.append((q_ref[pl.ds(row0, BAND), lanes], kd[pl.ds(row0, 2 * BAND), :],
                             vdt[:, pl.ds(row0, 2 * BAND)], bias_s[jnp.where(n == 0, 1, 0)]))
        for n, (o_t, m) in zip(blocks, _band_blocks(operands, sink_row)):
            l = o_t[LANES:LANES + 1, :] + jnp.exp2(sink_row - m)
            out_t = _pair_tile(o_t) * _pair_stat_tile(1.0 / l)
            o_ref[pl.ds(pl.multiple_of(n * BAND, BAND), BAND), lanes] = out_t.T.astype(o_ref.dtype)
        return carry

    lax.fori_loop(0, pairs_per_kv * iters_per_pair, body, 0)


def _attn_c(proj, sinks, batch, seq):
    group_w = C_GROUP * HEAD_DIM
    k_tile0 = C_WIDTH // LANES
    v_tile0 = k_tile0 + C_KV_HEADS
    bf16 = jnp.bfloat16
    return pl.pallas_call(
        _attn_c_kernel,
        grid=(batch, C_KV_HEADS),
        in_specs=[
            pl.BlockSpec(memory_space=pltpu.SMEM),
            pl.BlockSpec((seq, group_w), lambda b, g: (b, g)),
            pl.BlockSpec((seq, LANES), lambda b, g: (b, k_tile0 + g)),
            pl.BlockSpec((seq, LANES), lambda b, g: (b, v_tile0 + g)),
        ],
        out_specs=pl.BlockSpec((seq, group_w), lambda b, g: (b, g)),
        out_shape=jax.ShapeDtypeStruct((batch * seq, C_WIDTH), bf16),
        scratch_shapes=[
            pltpu.VMEM((seq + BAND, LANES), bf16),
            pltpu.VMEM((LANES + SUM_ROWS, seq + BAND), bf16),
            pltpu.VMEM((2, 2 * BAND, PAIR * BAND), jnp.float32),
        ],
        compiler_params=pltpu.CompilerParams(
            dimension_semantics=("parallel", "parallel"), vmem_limit_bytes=VMEM_LIMIT),
        name="attn_swa",
    )(sinks, proj, proj, proj)


_T_TILE, _T_CH, _T_FIRST, _T_ROWS = range(4)


def _moba_step_table(n_tiles):
    tiles, chs, firsts = [], [], []
    for tile in range(n_tiles):
        for i, ch in enumerate([tile] + list(range(tile))):
            tiles.append(tile)
            chs.append(ch)
            firsts.append(int(i == 0))
    n_steps = len(tiles)
    rows = [None] * _T_ROWS
    rows[_T_TILE] = tiles + [tiles[-1]]
    rows[_T_CH] = chs + [chs[-1]]
    rows[_T_FIRST] = firsts + [0]
    return n_steps, [v for row in rows for v in row]


def _moba_kernel(tbl_ref, q_ref, k_ref, v_ref, o_ref, vt, q2_all, kmean, bias_s, causal_s, m_s, acc_s,
                 s_a, s_b, *, n_steps):
    seq = q_ref.shape[0]
    nb = seq // MOBA_BLOCK
    nq2 = PAIR * MOBA_BLOCK
    nq = MOBA_CHUNK_BLOCKS * nq2
    n_tiles = nb // MOBA_CHUNK_BLOCKS
    chunk = MOBA_CHUNK_BLOCKS * MOBA_BLOCK

    def tbl(row, t):
        return tbl_ref[row * (n_steps + 1) + t]

    def prep(j, carry):
        rows = pl.ds(pl.multiple_of(j * MOBA_BLOCK, MOBA_BLOCK), MOBA_BLOCK)
        v_t = v_ref[rows, :].astype(jnp.float32).T.astype(vt.dtype)
        for h in range(PAIR):
            vt[h, 0:HEAD_DIM, rows] = v_t[h * HEAD_DIM:(h + 1) * HEAD_DIM]
        kmean[pl.ds(j, 1), :] = jnp.mean(k_ref[rows, :].astype(jnp.float32), axis=0, keepdims=True)
        q2_all[pl.ds(pl.multiple_of(j * nq2, nq2), nq2), :] = _split_heads_rows(q_ref[rows, :])
        return carry

    lax.fori_loop(0, nb, prep, 0, unroll=PREP_UNROLL)
    for h in range(PAIR):
        vt[h, HEAD_DIM:, :] = jnp.ones((SUM_ROWS, seq), vt.dtype)
    km = kmean[...]
    km_hi = km.astype(jnp.bfloat16)
    km_hilo = jnp.concatenate([km_hi, (km - km_hi.astype(jnp.float32)).astype(jnp.bfloat16)], axis=0)
    blk_id = lax.broadcasted_iota(jnp.int32, (nb, nq2), 0)
    key_id = lax.broadcasted_iota(jnp.int32, (chunk, nq), 0)
    col_id = lax.broadcasted_iota(jnp.int32, (chunk, nq), 1)
    causal_s[...] = jnp.where(key_id <= col_id % MOBA_BLOCK + (col_id // nq2) * MOBA_BLOCK, 0.0, NEG)

    def chunk_rows(ch):
        return pl.ds(pl.multiple_of(ch * chunk, chunk), chunk)

    def scores_t(t):
        q2 = q2_all[pl.ds(pl.multiple_of(tbl(_T_TILE, t) * nq, nq), nq), :]
        return lax.dot_general(k_ref[chunk_rows(tbl(_T_CH, t)), :], q2, _NT,
                               preferred_element_type=jnp.float32)

    def pv_t(t, p):
        p = p.astype(vt.dtype)
        rows = chunk_rows(tbl(_T_CH, t))
        return jnp.concatenate(
            [jnp.dot(vt[g % PAIR, :, rows], p[:, g * MOBA_BLOCK:(g + 1) * MOBA_BLOCK],
                     preferred_element_type=jnp.float32) for g in range(nq // MOBA_BLOCK)], axis=1)

    def block_bias(t):
        row0 = tbl(_T_TILE, t) * nb + tbl(_T_CH, t) * MOBA_CHUNK_BLOCKS
        return [bias_s[pl.ds(row0 + i, 1), :] for i in range(MOBA_CHUNK_BLOCKS)]

    def per_block(x, fn):
        return [fn(x[i * MOBA_BLOCK:(i + 1) * MOBA_BLOCK], i) for i in range(MOBA_CHUNK_BLOCKS)]

    def select_blocks(tile, carry):
        for i in range(MOBA_CHUNK_BLOCKS):
            qb = tile * MOBA_CHUNK_BLOCKS + i
            q2 = q2_all[pl.ds(pl.multiple_of(qb * nq2, nq2), nq2), :]
            g2 = lax.dot_general(km_hilo, q2, _NT, preferred_element_type=jnp.float32)
            past = blk_id < qb
            gate = jnp.where(past, g2[0:nb] + g2[nb:2 * nb], NEG)
            rank = jnp.zeros((nb, nq2), jnp.int32)
            for j in range(nb - 1):
                gj = gate[j:j + 1, :]
                tie_ref = jnp.where(blk_id > j, gate, NO_TIE)
                rank = rank + jnp.where(gj > gate, 1, 0) + jnp.where(gj == tie_ref, 1, 0)
            allowed = (past & (rank < MOBA_TOPK)) | (blk_id == qb)
            bias_s[pl.ds(pl.multiple_of(tile * nb, nb), nb), i * nq2:(i + 1) * nq2] = jnp.where(allowed, 0.0, NEG)
        return carry

    lax.fori_loop(0, n_tiles, select_blocks, 0, unroll=2)

    def attend_first(t, s_cur):
        bias = block_bias(t)

        def left_pad(x, i, fill):
            return x if i == 0 else jnp.concatenate([jnp.full((x.shape[0], i * nq2), fill, x.dtype), x], axis=1)

        parts = []
        for i in range(MOBA_CHUNK_BLOCKS):
            rows, cols = slice(i * MOBA_BLOCK, (i + 1) * MOBA_BLOCK), slice(i * nq2, nq)
            parts.append(s_cur[rows, cols] + causal_s[rows, cols] + bias[i][:, cols])
        m0 = functools.reduce(jnp.maximum, [left_pad(jnp.max(x, axis=0, keepdims=True), i, NEG)
                                            for i, x in enumerate(parts)])
        p = jnp.concatenate([left_pad(jnp.exp2(x - m0[:, i * nq2:]), i, 0.0) for i, x in enumerate(parts)], axis=0)
        m_s[...] = m0
        acc_s[...] = pv_t(t, p)

    def attend_next(t, s_cur):
        s = s_cur[...]
        bias = block_bias(t)
        m_old = m_s[...]
        m_blk = per_block(s, lambda x, i: jnp.max(x, axis=0, keepdims=True) + bias[i])
        m_new = functools.reduce(jnp.maximum, m_blk, m_old)
        alpha = jnp.exp2(m_old - m_new)
        p = jnp.concatenate(per_block(s, lambda x, i: jnp.exp2(x - (m_new - bias[i]))), axis=0)
        m_s[...] = m_new
        acc_s[...] = alpha * acc_s[...] + pv_t(t, p)

    def finalize(tile):
        o_all = acc_s[0:HEAD_DIM, :] * (1.0 / acc_s[HEAD_DIM:HEAD_DIM + 1, :])
        for i in range(MOBA_CHUNK_BLOCKS):
            c0 = i * nq2
            o_t = jnp.concatenate([o_all[:, c0:c0 + MOBA_BLOCK], o_all[:, c0 + MOBA_BLOCK:c0 + nq2]],
                                  axis=0)
            rows = pl.ds(pl.multiple_of((tile * MOBA_CHUNK_BLOCKS + i) * MOBA_BLOCK, MOBA_BLOCK), MOBA_BLOCK)
            o_ref[rows, :] = o_t.T.astype(o_ref.dtype)

    def step(t, s_cur, s_nxt):
        def first():
            finalize(jnp.maximum(tbl(_T_TILE, t) - 1, 0))
            s_nxt[...] = scores_t(t + 1)
            attend_first(t, s_cur)

        def later():
            s_nxt[...] = scores_t(t + 1)
            attend_next(t, s_cur)

        lax.cond(tbl(_T_FIRST, t) == 1, first, later)

    s_a[...] = scores_t(0)
    acc_s[...] = jnp.ones_like(acc_s)

    def two_steps(i, carry):
        step(2 * i, s_a, s_b)
        step(2 * i + 1, s_b, s_a)
        return carry

    lax.fori_loop(0, n_steps // 2, two_steps, 0)
    finalize(n_tiles - 1)


def _attn_b(proj, batch, seq):
    n_pairs = B_WIDTH // LANES
    tile0 = 4 * (A_WIDTH // LANES)

    def col_spec(group):
        return pl.BlockSpec((seq, LANES), lambda b, p, group=group: (b, tile0 + group * n_pairs + p))

    nb = seq // MOBA_BLOCK
    nq2 = PAIR * MOBA_BLOCK
    chunk = MOBA_CHUNK_BLOCKS * MOBA_BLOCK
    assert nb % MOBA_CHUNK_BLOCKS == 0
    n_tiles = nb // MOBA_CHUNK_BLOCKS
    nq = MOBA_CHUNK_BLOCKS * nq2
    n_steps, table = _moba_step_table(n_tiles)
    assert n_steps % 2 == 0
    f32 = jnp.float32
    bf16 = jnp.bfloat16
    return pl.pallas_call(
        functools.partial(_moba_kernel, n_steps=n_steps),
        grid=(batch, n_pairs),
        in_specs=[pl.BlockSpec(memory_space=pltpu.SMEM), col_spec(0), col_spec(1), col_spec(2)],
        out_specs=pl.BlockSpec((seq, LANES), lambda b, p: (b, p)),
        out_shape=jax.ShapeDtypeStruct((batch * seq, B_WIDTH), bf16),
        scratch_shapes=[
            pltpu.VMEM((PAIR, HEAD_DIM + SUM_ROWS, seq), bf16),
            pltpu.VMEM((nb * nq2, LANES), bf16),
            pltpu.VMEM((nb, LANES), f32),
            pltpu.VMEM((n_tiles * nb, nq), f32),
            pltpu.VMEM((chunk, nq), f32),
            pltpu.VMEM((1, nq), f32),
            pltpu.VMEM((HEAD_DIM + SUM_ROWS, nq), f32),
            pltpu.VMEM((chunk, nq), f32), pltpu.VMEM((chunk, nq), f32),
        ],
        compiler_params=pltpu.CompilerParams(
            dimension_semantics=("parallel", "parallel"), vmem_limit_bytes=VMEM_LIMIT),
        name="attn_moba",
    )(jnp.asarray(table, jnp.int32), proj, proj, proj)


def _rope_tables(seq):
    pos = jnp.arange(seq, dtype=jnp.float32)
    inv_freq = ROPE_THETA ** (-jnp.arange(0, HEAD_DIM, 2, dtype=jnp.float32) / HEAD_DIM)
    ang = pos[:, None] * inv_freq[None, :]
    cos = jnp.concatenate([jnp.cos(ang)] * (2 * PAIR), axis=-1)
    sin = jnp.concatenate([-jnp.sin(ang)] * PAIR + [jnp.sin(ang)] * PAIR, axis=-1)
    return cos, sin


def _tile_gain(g, width):
    return _qk_columns(jnp.tile(g, width // HEAD_DIM)[None, :])[0]


def _qk_columns(w):
    lead = w.shape[:-1]
    tiles = w.reshape(lead + (w.shape[-1] // LANES, PAIR, 2, HALF_DIM))
    return jnp.swapaxes(tiles, -3, -2).reshape(w.shape)


def kernel(x, norm_even, w_in_even, w_out_even, qnorm_a, knorm_a, qnorm_b, knorm_b,
           norm_odd, w_in_odd, w_out_odd, qnorm_c, knorm_c, sinks_c):
    batch, seq, d = x.shape
    bf16 = jnp.bfloat16
    scale = HEAD_DIM ** -0.5 * LOG2E
    cos, sin = _rope_tables(seq)
    x2 = x.reshape(batch * seq, d)

    ones = jnp.ones((A_WIDTH,), jnp.float32)
    gain0 = jnp.concatenate([
        _tile_gain(qnorm_a[0], A_WIDTH) * scale, _tile_gain(knorm_a[0], A_WIDTH), ones, ones,
        _tile_gain(qnorm_b[0], B_WIDTH) * scale, _tile_gain(knorm_b[0], B_WIDTH), ones, ones])[None, :]
    tiles_per_group = A_WIDTH // LANES
    rope0 = frozenset(t for grp in (0, 1, 4, 5) for t in range(grp * tiles_per_group, (grp + 1) * tiles_per_group))
    w0 = w_in_even[0]
    groups = [w0[:, g * A_WIDTH:(g + 1) * A_WIDTH] for g in range(8)]
    w0r = jnp.concatenate([_qk_columns(w) if g in (0, 1, 4, 5) else w for g, w in enumerate(groups)], axis=1)
    proj0 = _inproj(x2, norm_even[0][None, :], w0r.astype(bf16), gain0, cos, sin, rope0, seq)
    oa = _attn_a(proj0, batch, seq)
    ob = _attn_b(proj0, batch, seq)

    w1 = w_in_odd[0]
    kv_w = C_KV_HEADS * HEAD_DIM
    wq, wk, wv, wz = (w1[:, :C_WIDTH], w1[:, C_WIDTH:C_WIDTH + kv_w],
                      w1[:, C_WIDTH + kv_w:C_WIDTH + 2 * kv_w], w1[:, C_WIDTH + 2 * kv_w:])

    def dup_heads(w):
        return jnp.concatenate([w[:, h * HEAD_DIM:(h + 1) * HEAD_DIM]
                                for h in range(C_KV_HEADS) for _ in range(PAIR)], axis=1)

    w1r = jnp.concatenate([_qk_columns(wq), _qk_columns(dup_heads(wk)), dup_heads(wv), wz],
                          axis=1).astype(bf16)
    dup_w = PAIR * kv_w
    gain1 = jnp.concatenate([
        _tile_gain(qnorm_c[0], C_WIDTH) * scale, _tile_gain(knorm_c[0], dup_w),
        jnp.ones((dup_w + C_WIDTH,), jnp.float32)])[None, :]
    rope1 = frozenset(range((C_WIDTH + dup_w) // LANES))
    x1, proj1 = _outproj_inproj(x2, [(oa, 0), (ob, 0)], [(proj0, 3), (proj0, 7)], w_out_even[0].astype(bf16),
                                A_WIDTH, norm_odd[0][None, :], w1r, gain1, cos, sin, rope1, seq)
    oc = _attn_c(proj1, sinks_c[0], batch, seq)
    half = C_WIDTH // 2
    z_blk = (C_WIDTH + 2 * dup_w) // half
    out = _outproj(x1, [(oc, 0), (oc, 1)], [(proj1, z_blk), (proj1, z_blk + 1)],
                   w_out_odd[0].astype(bf16), half)
    return out.reshape(batch, seq, d)
```

```python
import functools

import jax
import jax.numpy as jnp
from jax import lax
from jax.experimental import pallas as pl
from jax.experimental.pallas import tpu as pltpu

D_MODEL = 1024
HEAD_DIM = 64
LANES = 128
PAIR = LANES // HEAD_DIM
HALF_DIM = HEAD_DIM // 2
NORM_COLS = 2 * LANES
A_WIDTH = 512
B_WIDTH = 512
C_WIDTH = 1024
C_KV_HEADS = 2
C_GROUP = 8
DILATIONS = (16, 4, 1)
MID_DILATION = 4
BAND = 128
MOBA_BLOCK = 256
MOBA_TOPK = 3
MOBA_CHUNK_BLOCKS = 2
DILATED_UNROLL = 16
SWA_UNROLL = 32
PREP_UNROLL = 4
SUM_ROWS = 16
LOG2E = 1.4426950408889634
ROPE_THETA = 10000.0
NORM_EPS = 1e-6
NEG = -1e30
NO_TIE = 3e38
VMEM_LIMIT = 56 * 1024 * 1024
PROJ_ROWS = 512
PROJ_COLS = 512

_NT = (((1,), (1,)), ((), ()))


def _lane_ids(shape):
    return lax.broadcasted_iota(jnp.int32, shape, len(shape) - 1)


def _qk_head_of_lane(lane):
    return (lane // HALF_DIM) % PAIR


def _split_heads_rows(t):
    head = _qk_head_of_lane(_lane_ids(t.shape))
    zero = jnp.zeros_like(t)
    return jnp.concatenate([jnp.where(head == 0, t, zero), jnp.where(head == 1, t, zero)], axis=0)


def _rms_rows(x, g_ref):
    ms = jnp.mean(x * x, axis=-1, keepdims=True)
    return (x * lax.rsqrt(ms + NORM_EPS) * g_ref[...]).astype(jnp.bfloat16)


def _project_columns(h, w_ref, gain_ref, cos_ref, sin_ref, o_ref, rope_tiles):
    n_cols = o_ref.shape[1]
    li = lax.broadcasted_iota(jnp.int32, (NORM_COLS, NORM_COLS), 0)
    lj = lax.broadcasted_iota(jnp.int32, (NORM_COLS, NORM_COLS), 1)
    same_head = ((li // LANES == lj // LANES)
                 & (_qk_head_of_lane(li) == _qk_head_of_lane(lj))).astype(jnp.bfloat16)
    for c0 in range(0, n_cols, PROJ_COLS):
        r = jnp.dot(h, w_ref[:, c0:c0 + PROJ_COLS], preferred_element_type=jnp.float32)
        for p0 in range(0, PROJ_COLS, NORM_COLS):
            col = c0 + p0
            tiles = range(col // LANES, (col + NORM_COLS) // LANES)
            t = r[:, p0:p0 + NORM_COLS]
            if all(tile in rope_tiles for tile in tiles):
                ss = jnp.dot((t * t).astype(jnp.bfloat16), same_head, preferred_element_type=jnp.float32)
                t = t * lax.rsqrt(ss * (1.0 / HEAD_DIM) + NORM_EPS) * gain_ref[:, col:col + NORM_COLS]
                for t0 in range(0, NORM_COLS, LANES):
                    tn = t[:, t0:t0 + LANES]
                    tn = tn * cos_ref[...] + pltpu.roll(tn, LANES // 2, 1) * sin_ref[...]
                    o_ref[:, col + t0:col + t0 + LANES] = tn.astype(o_ref.dtype)
            else:
                assert not any(tile in rope_tiles for tile in tiles)
                o_ref[:, col:col + NORM_COLS] = t.astype(o_ref.dtype)


def _gated_residual(x, y_refs, z_refs, w_ref):
    acc = x
    k0 = 0
    for y_ref, z_ref in zip(y_refs, z_refs):
        z = z_ref[...].astype(jnp.float32)
        gated = y_ref[...].astype(jnp.float32) * (z / (1.0 + jnp.exp(-z)))
        kw = y_ref.shape[1]
        acc = acc + jnp.dot(gated.astype(jnp.bfloat16), w_ref[k0:k0 + kw, :],
                            preferred_element_type=jnp.float32)
        k0 += kw
    return acc


def _inproj_kernel(x_ref, g_ref, w_ref, gain_ref, cos_ref, sin_ref, o_ref, *, rope_tiles):
    _project_columns(_rms_rows(x_ref[...], g_ref), w_ref, gain_ref, cos_ref, sin_ref, o_ref, rope_tiles)


def _outproj_kernel(*refs, n_parts):
    x_ref, y_refs, z_refs = refs[0], refs[1:1 + n_parts], refs[1 + n_parts:1 + 2 * n_parts]
    w_ref, o_ref = refs[1 + 2 * n_parts:]
    o_ref[...] = _gated_residual(x_ref[...], y_refs, z_refs, w_ref)


def _out_in_kernel(*refs, n_parts, rope_tiles):
    x_ref, y_refs, z_refs = refs[0], refs[1:1 + n_parts], refs[1 + n_parts:1 + 2 * n_parts]
    w_out_ref, g_ref, w_in_ref, gain_ref, cos_ref, sin_ref, x_out_ref, proj_ref = refs[1 + 2 * n_parts:]
    x_new = _gated_residual(x_ref[...], y_refs, z_refs, w_out_ref)
    x_out_ref[...] = x_new
    _project_columns(_rms_rows(x_new, g_ref), w_in_ref, gain_ref, cos_ref, sin_ref, proj_ref, rope_tiles)


def _row_spec(width, blk=0):
    return pl.BlockSpec((PROJ_ROWS, width), lambda i, blk=blk: (i, blk))


def _whole_spec(a):
    return pl.BlockSpec(a.shape, lambda i: (0,) * a.ndim)


def _proj_params():
    return pltpu.CompilerParams(dimension_semantics=("parallel",), vmem_limit_bytes=VMEM_LIMIT)


def _inproj_operands(g, w, gain, cos, sin, seq):
    seq_tiles = seq // PROJ_ROWS
    rope_spec = pl.BlockSpec((PROJ_ROWS, LANES), lambda i: (i % seq_tiles, 0))
    return [g, w, gain, cos, sin], [_whole_spec(g), _whole_spec(w), _whole_spec(gain), rope_spec, rope_spec]


def _gated_operands(ys, zs, w, part_width):
    arrays = [a for a, _ in list(ys) + list(zs)] + [w]
    specs = [_row_spec(part_width, blk) for _, blk in list(ys) + list(zs)] + [_whole_spec(w)]
    return arrays, specs


def _inproj(x2, g, w, gain, cos, sin, rope_tiles, seq):
    rows, d = x2.shape
    arrays, specs = _inproj_operands(g, w, gain, cos, sin, seq)
    return pl.pallas_call(
        functools.partial(_inproj_kernel, rope_tiles=rope_tiles),
        grid=(rows // PROJ_ROWS,),
        in_specs=[_row_spec(d)] + specs,
        out_specs=_row_spec(w.shape[1]),
        out_shape=jax.ShapeDtypeStruct((rows, w.shape[1]), jnp.bfloat16),
        compiler_params=_proj_params(),
        name="inproj",
    )(x2, *arrays)


def _outproj(x2, ys, zs, w, part_width):
    rows, d = x2.shape
    arrays, specs = _gated_operands(ys, zs, w, part_width)
    return pl.pallas_call(
        functools.partial(_outproj_kernel, n_parts=len(ys)),
        grid=(rows // PROJ_ROWS,),
        in_specs=[_row_spec(d)] + specs,
        out_specs=_row_spec(d),
        out_shape=jax.ShapeDtypeStruct((rows, d), jnp.float32),
        compiler_params=_proj_params(),
        name="outproj",
    )(x2, *arrays)


def _outproj_inproj(x2, ys, zs, w_out, part_width, g, w_in, gain, cos, sin, rope_tiles, seq):
    rows, d = x2.shape
    out_arrays, out_specs = _gated_operands(ys, zs, w_out, part_width)
    in_arrays, in_specs = _inproj_operands(g, w_in, gain, cos, sin, seq)
    return pl.pallas_call(
        functools.partial(_out_in_kernel, n_parts=len(ys), rope_tiles=rope_tiles),
        grid=(rows // PROJ_ROWS,),
        in_specs=[_row_spec(d)] + out_specs + in_specs,
        out_specs=[_row_spec(d), _row_spec(w_in.shape[1])],
        out_shape=[jax.ShapeDtypeStruct((rows, d), jnp.float32),
                   jax.ShapeDtypeStruct((rows, w_in.shape[1]), jnp.bfloat16)],
        compiler_params=_proj_params(),
        name="outproj_inproj",
    )(x2, *out_arrays, *in_arrays)


def _fill_band_bias(bias_ref, first_key):
    kj = lax.broadcasted_iota(jnp.int32, (2 * BAND, PAIR * BAND), 0)
    qi = lax.broadcasted_iota(jnp.int32, (2 * BAND, PAIR * BAND), 1) % BAND
    band = (kj >= qi + first_key) & (kj <= qi + BAND)
    bias_ref[0] = jnp.where(band, 0.0, NEG)
    bias_ref[1] = jnp.where(band & (kj >= BAND), 0.0, NEG)


def _band_blocks(operands, sink_row):
    scores = [lax.dot_general(kw, _split_heads_rows(q), _NT, preferred_element_type=jnp.float32) + bias
              for q, kw, _, bias in operands]
    probs = []
    for s in scores:
        m = jnp.max(s, axis=0, keepdims=True)
        if sink_row is not None:
            m = jnp.maximum(m, sink_row)
        probs.append((jnp.exp2(s - m).astype(jnp.bfloat16), m))
    return [(jnp.dot(vtw, p, preferred_element_type=jnp.float32), m)
            for (_, _, vtw, _), (p, m) in zip(operands, probs)]


def _pair_tile(x):
    return jnp.concatenate([x[0:HEAD_DIM, 0:BAND], x[HEAD_DIM:LANES, BAND:PAIR * BAND]], axis=0)


def _pair_stat_tile(v):
    return jnp.concatenate([jnp.broadcast_to(v[:, 0:BAND], (HEAD_DIM, BAND)),
                            jnp.broadcast_to(v[:, BAND:PAIR * BAND], (HEAD_DIM, BAND))], axis=0)


def _attn_a_kernel(q_ref, k_ref, v_ref, o_ref, qf, kf, vf, qf4, kf4, vf4, qd, kd, vdt, bias_s, m_s, l_s, acc_s):
    seq = q_ref.shape[0]
    n_blocks = seq // BAND
    piece = 2 * BAND
    mid = MID_DILATION
    qf[...] = q_ref[...].astype(jnp.float32)
    kf[...] = k_ref[...].astype(jnp.float32)
    vf[...] = v_ref[...].astype(jnp.float32)
    for r in range(mid):
        dst = pl.ds(r * (seq // mid), seq // mid)
        src = pl.ds(r, seq // mid, stride=mid)
        qf4[dst, :] = qf[src, :]
        kf4[dst, :] = kf[src, :]
        vf4[dst, :] = vf[src, :]
    kd[0:BAND, :] = jnp.zeros((BAND, LANES), kd.dtype)
    vdt[:, 0:BAND] = jnp.zeros((LANES + SUM_ROWS, BAND), vdt.dtype)
    vdt[LANES:, BAND:] = jnp.ones((SUM_ROWS, seq), vdt.dtype)
    _fill_band_bias(bias_s, 0)

    for dil in DILATIONS:
        sub = seq // dil
        blocks_per_sub = sub // BAND
        pieces_per_sub = sub // piece

        def deinterleave(i, carry, dil=dil, pieces_per_sub=pieces_per_sub):
            dst = pl.multiple_of(i * piece, piece)
            if dil == 1:
                src = pl.ds(dst, piece)
                q_src, k_src, v_src = q_ref, k_ref, vf
            elif dil == mid:
                src = pl.ds(dst, piece)
                q_src, k_src, v_src = qf4, kf4, vf4
            else:
                r = i // pieces_per_sub
                j = i % pieces_per_sub
                step = dil // mid
                src = pl.ds((r % mid) * (seq // mid) + r // mid + j * (piece * step), piece, stride=step)
                q_src, k_src, v_src = qf4, kf4, vf4
            qd[pl.ds(dst, piece), :] = q_src[src, :].astype(qd.dtype)
            kd[pl.ds(BAND + dst, piece), :] = k_src[src, :].astype(kd.dtype)
            vdt[0:LANES, pl.ds(BAND + dst, piece)] = v_src[src, :].astype(jnp.float32).T.astype(vdt.dtype)
            return carry

        lax.fori_loop(0, seq // piece, deinterleave, 0, unroll=PREP_UNROLL)

        def body(it, carry, dil=dil, blocks_per_sub=blocks_per_sub):
            blocks = [it * DILATED_UNROLL + u for u in range(DILATED_UNROLL)]
            operands = []
            for n in blocks:
                row0 = pl.multiple_of(n * BAND, BAND)
                first = (n % blocks_per_sub) == 0
                operands.append((qd[pl.ds(row0, BAND), :], kd[pl.ds(row0, 2 * BAND), :],
                                 vdt[:, pl.ds(row0, 2 * BAND)], bias_s[jnp.where(first, 1, 0)]))
            for n, (o_t, m) in zip(blocks, _band_blocks(operands, None)):
                merge_block(n, o_t, m, dil, blocks_per_sub)
            return carry

        def merge_block(n, o_t, m, dil, blocks_per_sub):
            r = n // blocks_per_sub
            c = n % blocks_per_sub
            row0 = pl.multiple_of(n * BAND, BAND)
            acc = _pair_tile(o_t).T
            m_b = _pair_stat_tile(m).T
            l_b = _pair_stat_tile(o_t[LANES:LANES + 1, :]).T
            rows = pl.ds(row0, BAND) if dil == 1 else pl.ds(c * (BAND * dil) + r, BAND, stride=dil)
            if dil == DILATIONS[0]:
                m_s[rows, :] = m_b
                l_s[rows, :] = l_b
                acc_s[rows, :] = acc
            else:
                m_old = m_s[rows, :]
                m_new = jnp.maximum(m_old, m_b)
                a_old = jnp.exp2(m_old - m_new)
                a_new = jnp.exp2(m_b - m_new)
                m_s[rows, :] = m_new
                l_s[rows, :] = a_old * l_s[rows, :] + a_new * l_b
                acc_s[rows, :] = a_old * acc_s[rows, :] + a_new * acc

        lax.fori_loop(0, n_blocks // DILATED_UNROLL, body, 0)

    o_ref[...] = (acc_s[...] / l_s[...]).astype(o_ref.dtype)


def _attn_a(proj, batch, seq):
    n_pairs = A_WIDTH // LANES

    def col_spec(group):
        return pl.BlockSpec((seq, LANES), lambda b, p, group=group: (b, group * n_pairs + p))

    f32 = jnp.float32
    bf16 = jnp.bfloat16
    return pl.pallas_call(
        _attn_a_kernel,
        grid=(batch, n_pairs),
        in_specs=[col_spec(0), col_spec(1), col_spec(2)],
        out_specs=pl.BlockSpec((seq, LANES), lambda b, p: (b, p)),
        out_shape=jax.ShapeDtypeStruct((batch * seq, A_WIDTH), bf16),
        scratch_shapes=[
            pltpu.VMEM((seq, LANES), f32), pltpu.VMEM((seq, LANES), f32), pltpu.VMEM((seq, LANES), f32),
            pltpu.VMEM((seq, LANES), f32), pltpu.VMEM((seq, LANES), f32), pltpu.VMEM((seq, LANES), f32),
            pltpu.VMEM((seq, LANES), bf16),
            pltpu.VMEM((seq + BAND, LANES), bf16),
            pltpu.VMEM((LANES + SUM_ROWS, seq + BAND), bf16),
            pltpu.VMEM((2, 2 * BAND, PAIR * BAND), f32),
            pltpu.VMEM((seq, LANES), f32), pltpu.VMEM((seq, LANES), f32), pltpu.VMEM((seq, LANES), f32),
        ],
        compiler_params=pltpu.CompilerParams(
            dimension_semantics=("parallel", "parallel"), vmem_limit_bytes=VMEM_LIMIT),
        name="attn_dilated",
    )(proj, proj, proj)


def _attn_c_kernel(sinks_ref, q_ref, k_ref, v_ref, o_ref, kd, vdt, bias_s):
    seq = q_ref.shape[0]
    pairs_per_kv = q_ref.shape[1] // LANES
    piece = 2 * BAND
    kd[0:BAND, :] = jnp.zeros((BAND, LANES), kd.dtype)
    kd[BAND:, :] = k_ref[...]
    vdt[:, 0:BAND] = jnp.zeros((LANES + SUM_ROWS, BAND), vdt.dtype)
    vdt[LANES:, BAND:] = jnp.ones((SUM_ROWS, seq), vdt.dtype)

    def transpose_v(i, carry):
        rows = pl.ds(pl.multiple_of(i * piece, piece), piece)
        vdt[0:LANES, pl.ds(pl.multiple_of(BAND + i * piece, BAND), piece)] = (
            v_ref[rows, :].astype(jnp.float32).T.astype(vdt.dtype))
        return carry

    lax.fori_loop(0, seq // piece, transpose_v, 0, unroll=PREP_UNROLL)
    _fill_band_bias(bias_s, 1)
    col = lax.broadcasted_iota(jnp.int32, (1, PAIR * BAND), 1)
    iters_per_pair = seq // (BAND * SWA_UNROLL)

    def body(step, carry):
        pair = step // iters_per_pair
        it = step % iters_per_pair
        lanes = pl.ds(pl.multiple_of(pair * LANES, LANES), LANES)
        head0 = PAIR * (pl.program_id(1) * pairs_per_kv + pair)
        sink_row = jnp.where(col < BAND, sinks_ref[head0], sinks_ref[head0 + 1]) * LOG2E
        blocks = [it * SWA_UNROLL + u for u in range(SWA_UNROLL)]
        operands = []
        for n in blocks:
            row0 = pl.multiple_of(n * BAND, BAND)
            operands.append((q_ref[pl.ds(row0, BAND), lanes], kd[pl.ds(row0, 2 * BAND), :],
                             vdt[:, pl.ds(row0, 2 * BAND)], bias_s[jnp.where(n == 0, 1, 0)]))
        for n, (o_t, m) in zip(blocks, _band_blocks(operands, sink_row)):
            l = o_t[LANES:LANES + 1, :] + jnp.exp2(sink_row - m)
            out_t = _pair_tile(o_t) * _pair_stat_tile(1.0 / l)
            o_ref[pl.ds(pl.multiple_of(n * BAND, BAND), BAND), lanes] = out_t.T.astype(o_ref.dtype)
        return carry

    lax.fori_loop(0, pairs_per_kv * iters_per_pair, body, 0)


def _attn_c(proj, sinks, batch, seq):
    group_w = C_GROUP * HEAD_DIM
    k_tile0 = C_WIDTH // LANES
    v_tile0 = k_tile0 + C_KV_HEADS
    bf16 = jnp.bfloat16
    return pl.pallas_call(
        _attn_c_kernel,
        grid=(batch, C_KV_HEADS),
        in_specs=[
            pl.BlockSpec(memory_space=pltpu.SMEM),
            pl.BlockSpec((seq, group_w), lambda b, g: (b, g)),
            pl.BlockSpec((seq, LANES), lambda b, g: (b, k_tile0 + g)),
            pl.BlockSpec((seq, LANES), lambda b, g: (b, v_tile0 + g)),
        ],
        out_specs=pl.BlockSpec((seq, group_w), lambda b, g: (b, g)),
        out_shape=jax.ShapeDtypeStruct((batch * seq, C_WIDTH), bf16),
        scratch_shapes=[
            pltpu.VMEM((seq + BAND, LANES), bf16),
            pltpu.VMEM((LANES + SUM_ROWS, seq + BAND), bf16),
            pltpu.VMEM((2, 2 * BAND, PAIR * BAND), jnp.float32),
        ],
        compiler_params=pltpu.CompilerParams(
            dimension_semantics=("parallel", "parallel"), vmem_limit_bytes=VMEM_LIMIT),
        name="attn_swa",
    )(sinks, proj, proj, proj)


_T_TILE, _T_CH, _T_FIRST, _T_ROWS = range(4)


def _moba_step_table(n_tiles):
    tiles, chs, firsts = [], [], []
    for tile in range(n_tiles):
        for i, ch in enumerate([tile] + list(range(tile))):
            tiles.append(tile)
            chs.append(ch)
            firsts.append(int(i == 0))
    n_steps = len(tiles)
    rows = [None] * _T_ROWS
    rows[_T_TILE] = tiles + [tiles[-1]]
    rows[_T_CH] = chs + [chs[-1]]
    rows[_T_FIRST] = firsts + [0]
    return n_steps, [v for row in rows for v in row]


def _moba_kernel(tbl_ref, q_ref, k_ref, v_ref, o_ref, vt, q2_all, kmean, bias_s, causal_s, m_s, acc_s,
                 s_a, s_b, *, n_steps):
    seq = q_ref.shape[0]
    nb = seq // MOBA_BLOCK
    nq2 = PAIR * MOBA_BLOCK
    nq = MOBA_CHUNK_BLOCKS * nq2
    n_tiles = nb // MOBA_CHUNK_BLOCKS
    chunk = MOBA_CHUNK_BLOCKS * MOBA_BLOCK

    def tbl(row, t):
        return tbl_ref[row * (n_steps + 1) + t]

    def prep(j, carry):
        rows = pl.ds(pl.multiple_of(j * MOBA_BLOCK, MOBA_BLOCK), MOBA_BLOCK)
        v_t = v_ref[rows, :].astype(jnp.float32).T.astype(vt.dtype)
        for h in range(PAIR):
            vt[h, 0:HEAD_DIM, rows] = v_t[h * HEAD_DIM:(h + 1) * HEAD_DIM]
        kmean[pl.ds(j, 1), :] = jnp.mean(k_ref[rows, :].astype(jnp.float32), axis=0, keepdims=True)
        q2_all[pl.ds(pl.multiple_of(j * nq2, nq2), nq2), :] = _split_heads_rows(q_ref[rows, :])
        return carry

    lax.fori_loop(0, nb, prep, 0, unroll=PREP_UNROLL)
    for h in range(PAIR):
        vt[h, HEAD_DIM:, :] = jnp.ones((SUM_ROWS, seq), vt.dtype)
    km = kmean[...]
    km_hi = km.astype(jnp.bfloat16)
    km_hilo = jnp.concatenate([km_hi, (km - km_hi.astype(jnp.float32)).astype(jnp.bfloat16)], axis=0)
    blk_id = lax.broadcasted_iota(jnp.int32, (nb, nq2), 0)
    key_id = lax.broadcasted_iota(jnp.int32, (chunk, nq), 0)
    col_id = lax.broadcasted_iota(jnp.int32, (chunk, nq), 1)
    causal_s[...] = jnp.where(key_id <= col_id % MOBA_BLOCK + (col_id // nq2) * MOBA_BLOCK, 0.0, NEG)

    def chunk_rows(ch):
        return pl.ds(pl.multiple_of(ch * chunk, chunk), chunk)

    def scores_t(t):
        q2 = q2_all[pl.ds(pl.multiple_of(tbl(_T_TILE, t) * nq, nq), nq), :]
        return lax.dot_general(k_ref[chunk_rows(tbl(_T_CH, t)), :], q2, _NT,
                               preferred_element_type=jnp.float32)

    def pv_t(t, p):
        p = p.astype(vt.dtype)
        rows = chunk_rows(tbl(_T_CH, t))
        return jnp.concatenate(
            [jnp.dot(vt[g % PAIR, :, rows], p[:, g * MOBA_BLOCK:(g + 1) * MOBA_BLOCK],
                     preferred_element_type=jnp.float32) for g in range(nq // MOBA_BLOCK)], axis=1)

    def block_bias(t):
        row0 = tbl(_T_TILE, t) * nb + tbl(_T_CH, t) * MOBA_CHUNK_BLOCKS
        return [bias_s[pl.ds(row0 + i, 1), :] for i in range(MOBA_CHUNK_BLOCKS)]

    def per_block(x, fn):
        return [fn(x[i * MOBA_BLOCK:(i + 1) * MOBA_BLOCK], i) for i in range(MOBA_CHUNK_BLOCKS)]

    def select_blocks(tile, carry):
        for i in range(MOBA_CHUNK_BLOCKS):
            qb = tile * MOBA_CHUNK_BLOCKS + i
            q2 = q2_all[pl.ds(pl.multiple_of(qb * nq2, nq2), nq2), :]
            g2 = lax.dot_general(km_hilo, q2, _NT, preferred_element_type=jnp.float32)
            past = blk_id < qb
            gate = jnp.where(past, g2[0:nb] + g2[nb:2 * nb], NEG)
            rank = jnp.zeros((nb, nq2), jnp.int32)
            for j in range(nb - 1):
                gj = gate[j:j + 1, :]
                tie_ref = jnp.where(blk_id > j, gate, NO_TIE)
                rank = rank + jnp.where(gj > gate, 1, 0) + jnp.where(gj == tie_ref, 1, 0)
            allowed = (past & (rank < MOBA_TOPK)) | (blk_id == qb)
            bias_s[pl.ds(pl.multiple_of(tile * nb, nb), nb), i * nq2:(i + 1) * nq2] = jnp.where(allowed, 0.0, NEG)
        return carry

    lax.fori_loop(0, n_tiles, select_blocks, 0, unroll=2)

    def attend_first(t, s_cur):
        bias = block_bias(t)

        def left_pad(x, i, fill):
            return x if i == 0 else jnp.concatenate([jnp.full((x.shape[0], i * nq2), fill, x.dtype), x], axis=1)

        parts = []
        for i in range(MOBA_CHUNK_BLOCKS):
            rows, cols = slice(i * MOBA_BLOCK, (i + 1) * MOBA_BLOCK), slice(i * nq2, nq)
            parts.append(s_cur[rows, cols] + causal_s[rows, cols] + bias[i][:, cols])
        m0 = functools.reduce(jnp.maximum, [left_pad(jnp.max(x, axis=0, keepdims=True), i, NEG)
                                            for i, x in enumerate(parts)])
        p = jnp.concatenate([left_pad(jnp.exp2(x - m0[:, i * nq2:]), i, 0.0) for i, x in enumerate(parts)], axis=0)
        m_s[...] = m0
        acc_s[...] = pv_t(t, p)

    def attend_next(t, s_cur):
        s = s_cur[...]
        bias = block_bias(t)
        m_old = m_s[...]
        m_blk = per_block(s, lambda x, i: jnp.max(x, axis=0, keepdims=True) + bias[i])
        m_new = functools.reduce(jnp.maximum, m_blk, m_old)
        alpha = jnp.exp2(m_old - m_new)
        p = jnp.concatenate(per_block(s, lambda x, i: jnp.exp2(x - (m_new - bias[i]))), axis=0)
        m_s[...] = m_new
        acc_s[...] = alpha * acc_s[...] + pv_t(t, p)

    def finalize(tile):
        o_all = acc_s[0:HEAD_DIM, :] * (1.0 / acc_s[HEAD_DIM:HEAD_DIM + 1, :])
        for i in range(MOBA_CHUNK_BLOCKS):
            c0 = i * nq2
            o_t = jnp.concatenate([o_all[:, c0:c0 + MOBA_BLOCK], o_all[:, c0 + MOBA_BLOCK:c0 + nq2]],
                                  axis=0)
            rows = pl.ds(pl.multiple_of((tile * MOBA_CHUNK_BLOCKS + i) * MOBA_BLOCK, MOBA_BLOCK), MOBA_BLOCK)
            o_ref[rows, :] = o_t.T.astype(o_ref.dtype)

    def step(t, s_cur, s_nxt):
        def first():
            finalize(jnp.maximum(tbl(_T_TILE, t) - 1, 0))
            s_nxt[...] = scores_t(t + 1)
            attend_first(t, s_cur)

        def later():
            s_nxt[...] = scores_t(t + 1)
            attend_next(t, s_cur)

        lax.cond(tbl(_T_FIRST, t) == 1, first, later)

    s_a[...] = scores_t(0)
    acc_s[...] = jnp.ones_like(acc_s)

    def two_steps(i, carry):
        step(2 * i, s_a, s_b)
        step(2 * i + 1, s_b, s_a)
        return carry

    lax.fori_loop(0, n_steps // 2, two_steps, 0)
    finalize(n_tiles - 1)


def _attn_b(proj, batch, seq):
    n_pairs = B_WIDTH // LANES
    tile0 = 4 * (A_WIDTH // LANES)

    def col_spec(group):
        return pl.BlockSpec((seq, LANES), lambda b, p, group=group: (b, tile0 + group * n_pairs + p))

    nb = seq // MOBA_BLOCK
    nq2 = PAIR * MOBA_BLOCK
    chunk = MOBA_CHUNK_BLOCKS * MOBA_BLOCK
    assert nb % MOBA_CHUNK_BLOCKS == 0
    n_tiles = nb // MOBA_CHUNK_BLOCKS
    nq = MOBA_CHUNK_BLOCKS * nq2
    n_steps, table = _moba_step_table(n_tiles)
    assert n_steps % 2 == 0
    f32 = jnp.float32
    bf16 = jnp.bfloat16
    return pl.pallas_call(
        functools.partial(_moba_kernel, n_steps=n_steps),
        grid=(batch, n_pairs),
        in_specs=[pl.BlockSpec(memory_space=pltpu.SMEM), col_spec(0), col_spec(1), col_spec(2)],
        out_specs=pl.BlockSpec((seq, LANES), lambda b, p: (b, p)),
        out_shape=jax.ShapeDtypeStruct((batch * seq, B_WIDTH), bf16),
        scratch_shapes=[
            pltpu.VMEM((PAIR, HEAD_DIM + SUM_ROWS, seq), bf16),
            pltpu.VMEM((nb * nq2, LANES), bf16),
            pltpu.VMEM((nb, LANES), f32),
            pltpu.VMEM((n_tiles * nb, nq), f32),
            pltpu.VMEM((chunk, nq), f32),
            pltpu.VMEM((1, nq), f32),
            pltpu.VMEM((HEAD_DIM + SUM_ROWS, nq), f32),
            pltpu.VMEM((chunk, nq), f32), pltpu.VMEM((chunk, nq), f32),
        ],
        compiler_params=pltpu.CompilerParams(
            dimension_semantics=("parallel", "parallel"), vmem_limit_bytes=VMEM_LIMIT),
        name="attn_moba",
    )(jnp.asarray(table, jnp.int32), proj, proj, proj)


def _rope_tables(seq):
    pos = jnp.arange(seq, dtype=jnp.float32)
    inv_freq = ROPE_THETA ** (-jnp.arange(0, HEAD_DIM, 2, dtype=jnp.float32) / HEAD_DIM)
    ang = pos[:, None] * inv_freq[None, :]
    cos = jnp.concatenate([jnp.cos(ang)] * (2 * PAIR), axis=-1)
    sin = jnp.concatenate([-jnp.sin(ang)] * PAIR + [jnp.sin(ang)] * PAIR, axis=-1)
    return cos, sin


def _tile_gain(g, width):
    return _qk_columns(jnp.tile(g, width // HEAD_DIM)[None, :])[0]


def _qk_columns(w):
    lead = w.shape[:-1]
    tiles = w.reshape(lead + (w.shape[-1] // LANES, PAIR, 2, HALF_DIM))
    return jnp.swapaxes(tiles, -3, -2).reshape(w.shape)


def kernel(x, norm_even, w_in_even, w_out_even, qnorm_a, knorm_a, qnorm_b, knorm_b,
           norm_odd, w_in_odd, w_out_odd, qnorm_c, knorm_c, sinks_c):
    batch, seq, d = x.shape
    bf16 = jnp.bfloat16
    scale = HEAD_DIM ** -0.5 * LOG2E
    cos, sin = _rope_tables(seq)
    x2 = x.reshape(batch * seq, d)

    ones = jnp.ones((A_WIDTH,), jnp.float32)
    gain0 = jnp.concatenate([
        _tile_gain(qnorm_a[0], A_WIDTH) * scale, _tile_gain(knorm_a[0], A_WIDTH), ones, ones,
        _tile_gain(qnorm_b[0], B_WIDTH) * scale, _tile_gain(knorm_b[0], B_WIDTH), ones, ones])[None, :]
    tiles_per_group = A_WIDTH // LANES
    rope0 = frozenset(t for grp in (0, 1, 4, 5) for t in range(grp * tiles_per_group, (grp + 1) * tiles_per_group))
    w0 = w_in_even[0]
    groups = [w0[:, g * A_WIDTH:(g + 1) * A_WIDTH] for g in range(8)]
    w0r = jnp.concatenate([_qk_columns(w) if g in (0, 1, 4, 5) else w for g, w in enumerate(groups)], axis=1)
    proj0 = _inproj(x2, norm_even[0][None, :], w0r.astype(bf16), gain0, cos, sin, rope0, seq)
    oa = _attn_a(proj0, batch, seq)
    ob = _attn_b(proj0, batch, seq)

    w1 = w_in_odd[0]
    kv_w = C_KV_HEADS * HEAD_DIM
    wq, wk, wv, wz = (w1[:, :C_WIDTH], w1[:, C_WIDTH:C_WIDTH + kv_w],
                      w1[:, C_WIDTH + kv_w:C_WIDTH + 2 * kv_w], w1[:, C_WIDTH + 2 * kv_w:])

    def dup_heads(w):
        return jnp.concatenate([w[:, h * HEAD_DIM:(h + 1) * HEAD_DIM]
                                for h in range(C_KV_HEADS) for _ in range(PAIR)], axis=1)

    w1r = jnp.concatenate([_qk_columns(wq), _qk_columns(dup_heads(wk)), dup_heads(wv), wz],
                          axis=1).astype(bf16)
    dup_w = PAIR * kv_w
    gain1 = jnp.concatenate([
        _tile_gain(qnorm_c[0], C_WIDTH) * scale, _tile_gain(knorm_c[0], dup_w),
        jnp.ones((dup_w + C_WIDTH,), jnp.float32)])[None, :]
    rope1 = frozenset(range((C_WIDTH + dup_w) // LANES))
    x1, proj1 = _outproj_inproj(x2, [(oa, 0), (ob, 0)], [(proj0, 3), (proj0, 7)], w_out_even[0].astype(bf16),
                                A_WIDTH, norm_odd[0][None, :], w1r, gain1, cos, sin, rope1, seq)
    oc = _attn_c(proj1, sinks_c[0], batch, seq)
    half = C_WIDTH // 2
    z_blk = (C_WIDTH + 2 * dup_w) // half
    out = _outproj(x1, [(oc, 0), (oc, 1)], [(proj1, z_blk), (proj1, z_blk + 1)],
                   w_out_odd[0].astype(bf16), half)
    return out.reshape(batch, seq, d)
```
